```python
import jax, jax.numpy as jnp
from jax import lax
import numpy as np

D_MODEL = 1024
BATCH = 8
SEQ = 2048
DEPTH = 2

GRID_W = 64
CTX_LEN = 256
NA_HEADS = 16
NA_HEAD_DIM = D_MODEL // NA_HEADS
NA_KR = 8
NA_KC = 16
SSD_D_INNER = 2 * D_MODEL
SSD_HEAD_DIM = 64
SSD_HEADS = SSD_D_INNER // SSD_HEAD_DIM
SSD_GROUPS = 8
SSD_HEADS_PER_GROUP = SSD_HEADS // SSD_GROUPS
SSD_STATE = 128
SSD_CONV_W = 5
SSD_CHUNK = 128
SSD_CONV_DIM = SSD_D_INNER + 2 * SSD_GROUPS * SSD_STATE
SSD_IN_W = SSD_D_INNER + SSD_CONV_DIM + 2 * SSD_HEADS
N_EXPERTS = 32
TOP_K = 4
D_FF = D_MODEL
SWIGLU_ALPHA = 1.702
SWIGLU_LIMIT = 7.0
MOE_BLOCK = 256
NORM_EPS = 1e-6

kernel_name = 'hybrid_natten_ssd_moe_dit'


def rmsnorm(x, g):
    xf = x.astype(jnp.float32)
    y = xf * lax.rsqrt(jnp.mean(xf * xf, axis=-1, keepdims=True) + NORM_EPS)
    return y.astype(x.dtype) * g


def modulate(h, shift, scale):
    return h * (1 + scale) + shift


def neighbourhood_attention(hx, hc, w_qkv, q_g, k_g, rpb, w_o, need_ctx):
    b, s, _ = hx.shape
    rows = s // GRID_W
    kr = min(NA_KR, rows)
    scale = NA_HEAD_DIM ** -0.5

    def qkv(h):
        q, k, v = jnp.split(h @ w_qkv, 3, axis=-1)
        shp = h.shape[:2] + (NA_HEADS, NA_HEAD_DIM)
        return rmsnorm(q.reshape(shp), q_g), rmsnorm(k.reshape(shp), k_g), v.reshape(shp)

    q, k, v = qkv(hx)
    qc, kc, vc = qkv(hc)
    grid = (b, rows, GRID_W, NA_HEADS, NA_HEAD_DIM)
    qg, kg, vg = q.reshape(grid), k.reshape(grid), v.reshape(grid)

    col = jnp.arange(GRID_W)
    col_start = jnp.clip(col - NA_KC // 2, 0, GRID_W - NA_KC)
    col_ok = (col[None, :] >= col_start[:, None]) & (col[None, :] < col_start[:, None] + NA_KC)
    col_idx = jnp.clip(col[None, :] - col[:, None] + NA_KC - 1, 0, 2 * NA_KC - 2)

    def row_block(r):
        rs = jnp.clip(r - kr // 2, 0, rows - kr)
        qr = lax.dynamic_index_in_dim(qg, r, axis=1, keepdims=False)
        kb = lax.dynamic_slice_in_dim(kg, rs, kr, axis=1)
        vb = lax.dynamic_slice_in_dim(vg, rs, kr, axis=1)
        row_idx = rs + jnp.arange(kr) - r + NA_KR - 1
        bias = rpb[:, row_idx][:, :, col_idx].transpose(0, 2, 1, 3)
        s_lat = jnp.einsum('bqhd,bnkhd->bhqnk', qr, kb).astype(jnp.float32) * scale + bias.astype(jnp.float32)
        s_lat = jnp.where(col_ok[:, None, :], s_lat, -jnp.inf)
        s_ctx = jnp.einsum('bqhd,bchd->bhqc', qr, kc).astype(jnp.float32) * scale
        logits = jnp.concatenate([s_lat.reshape(b, NA_HEADS, GRID_W, kr * GRID_W), s_ctx], axis=-1)
        p = jax.nn.softmax(logits, axis=-1).astype(vb.dtype)
        p_lat = p[..., :kr * GRID_W].reshape(b, NA_HEADS, GRID_W, kr, GRID_W)
        p_ctx = p[..., kr * GRID_W:]
        return jnp.einsum('bhqnk,bnkhd->bqhd', p_lat, vb) + jnp.einsum('bhqc,bchd->bqhd', p_ctx, vc)

    o = lax.map(row_block, jnp.arange(rows))
    out_x = jnp.moveaxis(o, 0, 1).reshape(b, s, D_MODEL) @ w_o
    out_c = None
    if need_ctx:
        sc = jnp.einsum('bqhd,bkhd->bhqk', qc, kc).astype(jnp.float32) * scale
        pc = jax.nn.softmax(sc, axis=-1).astype(vc.dtype)
        out_c = jnp.einsum('bhqk,bkhd->bqhd', pc, vc).reshape(hc.shape) @ w_o
    return out_x, out_c


def depthwise_conv(u, w, bias):
    k = w.shape[0]
    out = lax.conv_general_dilated(u, w[:, None, :], window_strides=(1,),
                                   padding=[(k // 2, k // 2)],
                                   dimension_numbers=('NWC', 'WIO', 'NWC'),
                                   feature_group_count=u.shape[-1])
    return out + bias


def ssd_chunked(X, dA, Bm, Cm, h0, want_y):
    b, L, g, r, p = X.shape
    n = Bm.shape[-1]
    nc, l = L // SSD_CHUNK, SSD_CHUNK
    Xc = X.reshape(b, nc, l, g, r, p)
    Bc = Bm.reshape(b, nc, l, g, n).astype(jnp.float32)
    A_cum = jnp.cumsum(dA.reshape(b, nc, l, g, r), axis=2)
    decay_to_end = jnp.exp(A_cum[:, :, -1:] - A_cum)
    states = jnp.einsum('bclgn,bclgr,bclgrp->bcgrpn', Bc, decay_to_end, Xc)
    chunk_decay = jnp.exp(A_cum[:, :, -1])

    def step(h, inp):
        s_k, a_k = inp
        return a_k[..., None, None] * h + s_k, h

    h_last, h_in = lax.scan(step, h0, (jnp.moveaxis(states, 1, 0), jnp.moveaxis(chunk_decay, 1, 0)))
    if not want_y:
        return None, h_last
    h_in = jnp.moveaxis(h_in, 0, 1)
    Cc = Cm.reshape(b, nc, l, g, n).astype(jnp.float32)
    A_t = jnp.moveaxis(A_cum, 2, -1)
    seg = A_t[..., :, None] - A_t[..., None, :]
    lower = jnp.tril(jnp.ones((l, l), dtype=bool))
    decay = jnp.exp(jnp.where(lower, seg, -jnp.inf))
    cb = jnp.einsum('bclgn,bcsgn->bcgls', Cc, Bc)
    y_diag = jnp.einsum('bcgrls,bcsgrp->bclgrp', cb[:, :, :, None] * decay, Xc)
    y_off = jnp.einsum('bclgn,bcgrpn->bclgrp', Cc, h_in) * jnp.exp(A_cum)[..., None]
    return (y_diag + y_off).reshape(b, L, g, r, p), h_last


def bidirectional_ssd(hx, hc, w_in, conv_w, conv_b, dt_bias, a_log, d_skip, norm_g, w_out, need_ctx):
    b = hx.shape[0]
    grp = (SSD_GROUPS, SSD_HEADS_PER_GROUP)
    a = -jnp.exp(a_log.astype(jnp.float32)).reshape((2,) + grp)

    def split_inputs(h):
        l = h.shape[1]
        z, xbc, dt = jnp.split(h @ w_in, [SSD_D_INNER, SSD_D_INNER + SSD_CONV_DIM], axis=-1)
        xbc = jax.nn.silu(depthwise_conv(xbc, conv_w, conv_b))
        xs, bm, cm = jnp.split(xbc, [SSD_D_INNER, SSD_D_INNER + SSD_GROUPS * SSD_STATE], axis=-1)
        xs = xs.reshape((b, l) + grp + (SSD_HEAD_DIM,))
        bm = bm.reshape(b, l, SSD_GROUPS, SSD_STATE)
        cm = cm.reshape(b, l, SSD_GROUPS, SSD_STATE)
        dt = jax.nn.softplus(dt.astype(jnp.float32).reshape((b, l, 2) + grp)
                             + dt_bias.astype(jnp.float32).reshape((2,) + grp))
        return z, xs, bm, cm, dt

    zx, xx, bx, cx, dtx = split_inputs(hx)
    zc, xc, bc, cc, dtc = split_inputs(hc)

    def scan_dir(d, xs, bm, cm, dt, h0, want_y):
        if d == 1:
            xs, bm, cm, dt = [jnp.flip(t, axis=1) for t in (xs, bm, cm, dt)]
        dt_d = dt[:, :, d]
        y, h_last = ssd_chunked(xs * dt_d[..., None], dt_d * a[d], bm, cm, h0, want_y)
        if d == 1 and y is not None:
            y = jnp.flip(y, axis=1)
        return y, h_last

    h0 = jnp.zeros((b,) + grp + (SSD_HEAD_DIM, SSD_STATE), jnp.float32)
    yc_f, hc_f = scan_dir(0, xc, bc, cc, dtc, h0, need_ctx)
    yc_b, hc_b = scan_dir(1, xc, bc, cc, dtc, h0, need_ctx)
    yx_f, _ = scan_dir(0, xx, bx, cx, dtx, hc_f, True)
    yx_b, _ = scan_dir(1, xx, bx, cx, dtx, hc_b, True)

    def out_proj(y, xs, z):
        y = (y + d_skip.astype(jnp.float32).reshape(grp + (1,)) * xs).astype(z.dtype)
        gated = (y.reshape(z.shape) * jax.nn.silu(z)).reshape(z.shape[:2] + (SSD_GROUPS, SSD_D_INNER // SSD_GROUPS))
        gated = rmsnorm(gated, norm_g.reshape(SSD_GROUPS, -1)).reshape(z.shape)
        return gated @ w_out

    out_x = out_proj(yx_f + yx_b, xx, zx)
    out_c = out_proj(yc_f + yc_b, xc, zc) if need_ctx else None
    return out_x, out_c


def moe(xt, w_r, b_r, w1, b1, w2, b2):
    n_tok, d = xt.shape
    nk = n_tok * TOP_K
    logits = (xt @ w_r + b_r).astype(jnp.float32)
    top_v, top_e = lax.top_k(logits, TOP_K)
    gates = jax.nn.softmax(top_v, axis=-1)
    flat_e = top_e.reshape(-1)
    flat_tok = jnp.arange(nk, dtype=jnp.int32) // TOP_K
    order = jnp.argsort(flat_e)
    se, st, sg = flat_e[order], flat_tok[order], gates.reshape(-1)[order]
    counts = jnp.bincount(flat_e, length=N_EXPERTS)
    padded = (counts + MOE_BLOCK - 1) // MOE_BLOCK * MOE_BLOCK
    start = jnp.cumsum(counts) - counts
    pend = jnp.cumsum(padded)
    pstart = pend - padded
    dest = pstart[se] + jnp.arange(nk, dtype=jnp.int32) - start[se]
    n_blocks = -(-(nk + N_EXPERTS * (MOE_BLOCK - 1)) // MOE_BLOCK)
    n_slots = n_blocks * MOE_BLOCK
    slot_tok = jnp.full((n_slots,), n_tok, jnp.int32).at[dest].set(st)
    slot_gate = jnp.zeros((n_slots,), jnp.float32).at[dest].set(sg)
    block_expert = jnp.minimum(jnp.searchsorted(pend, jnp.arange(n_blocks, dtype=jnp.int32) * MOE_BLOCK, side='right'), N_EXPERTS - 1)
    x_pad = jnp.concatenate([xt, jnp.zeros((1, d), xt.dtype)], axis=0)
    xb = x_pad[slot_tok].reshape(n_blocks, MOE_BLOCK, d)

    def expert_block(args):
        xblk, e = args
        h = xblk @ w1[e] + b1[e]
        glu = jnp.minimum(h[:, :D_FF], SWIGLU_LIMIT)
        lin = jnp.clip(h[:, D_FF:], -SWIGLU_LIMIT, SWIGLU_LIMIT)
        act = glu * jax.nn.sigmoid(SWIGLU_ALPHA * glu) * (lin + 1)
        return act @ w2[e] + b2[e]

    yb = lax.map(expert_block, (xb, block_expert)).reshape(n_slots, d)
    yb = yb * slot_gate[:, None].astype(yb.dtype)
    return jax.ops.segment_sum(yb, slot_tok, num_segments=n_tok + 1)[:n_tok]


def setup_inputs(seed: int = 0) -> dict:
    key = jax.random.key(seed)
    ks = iter(jax.random.split(key, 40))
    n_na = (DEPTH + 1) // 2
    n_ssd = DEPTH // 2
    D = D_MODEL

    def nrm(shape, scale):
        return jax.random.normal(next(ks), shape, jnp.float32) * scale

    dt0 = jnp.exp(jax.random.uniform(next(ks), (n_ssd, 2, SSD_HEADS), jnp.float32, np.log(1e-3), np.log(1e-1)))
    return {
        'x': nrm((BATCH, SEQ, D), 1.0),
        'c': nrm((BATCH, D), 1.0),
        'ctx': nrm((BATCH, CTX_LEN, D), 1.0),
        'c_ctx': nrm((D,), 1.0),
        'ada_w': nrm((DEPTH, D, 6 * D), 0.5 * D ** -0.5),
        'ada_b': nrm((DEPTH, 6 * D), 0.02),
        'norm1_g': 1.0 + nrm((DEPTH, D), 0.02),
        'norm2_g': 1.0 + nrm((DEPTH, D), 0.02),
        'na_w_qkv': nrm((n_na, D, 3 * D), D ** -0.5),
        'na_q_g': 1.0 + nrm((n_na, NA_HEAD_DIM), 0.02),
        'na_k_g': 1.0 + nrm((n_na, NA_HEAD_DIM), 0.02),
        'na_rpb': nrm((n_na, NA_HEADS, 2 * NA_KR - 1, 2 * NA_KC - 1), 0.1),
        'na_w_o': nrm((n_na, D, D), D ** -0.5),
        'ssd_w_in': nrm((n_ssd, D, SSD_IN_W), D ** -0.5),
        'ssd_conv_w': nrm((n_ssd, SSD_CONV_W, SSD_CONV_DIM), SSD_CONV_W ** -0.5),
        'ssd_conv_b': nrm((n_ssd, SSD_CONV_DIM), 0.02),
        'ssd_dt_bias': dt0 + jnp.log(-jnp.expm1(-dt0)),
        'ssd_a_log': jnp.log(jax.random.uniform(next(ks), (n_ssd, 2, SSD_HEADS), jnp.float32, 1.0, 16.0)),
        'ssd_d': 1.0 + nrm((n_ssd, SSD_HEADS), 0.02),
        'ssd_norm_g': 1.0 + nrm((n_ssd, SSD_D_INNER), 0.02),
        'ssd_w_out': nrm((n_ssd, SSD_D_INNER, D), SSD_D_INNER ** -0.5),
        'moe_w_router': nrm((DEPTH, D, N_EXPERTS), D ** -0.5),
        'moe_b_router': nrm((DEPTH, N_EXPERTS), 0.01),
        'moe_w1': nrm((DEPTH, N_EXPERTS, D, 2 * D_FF), D ** -0.5),
        'moe_b1': nrm((DEPTH, N_EXPERTS, 2 * D_FF), 0.01),
        'moe_w2': nrm((DEPTH, N_EXPERTS, D_FF, D), D_FF ** -0.5),
        'moe_b2': nrm((DEPTH, N_EXPERTS, D), 0.01),
    }


def reference(x, c, ctx, c_ctx, ada_w, ada_b, norm1_g, norm2_g,
              na_w_qkv, na_q_g, na_k_g, na_rpb, na_w_o,
              ssd_w_in, ssd_conv_w, ssd_conv_b, ssd_dt_bias, ssd_a_log, ssd_d, ssd_norm_g, ssd_w_out,
              moe_w_router, moe_b_router, moe_w1, moe_b1, moe_w2, moe_b2):
    b, s, d = x.shape
    cx = ctx
    silu_c = jax.nn.silu(c)
    silu_cc = jax.nn.silu(c_ctx)
    for i in range(DEPTH):
        last = i == DEPTH - 1
        mx = jnp.split((silu_c @ ada_w[i] + ada_b[i])[:, None, :], 6, axis=-1)
        mc = jnp.split(silu_cc @ ada_w[i] + ada_b[i], 6, axis=-1)
        hx = modulate(rmsnorm(x, norm1_g[i]), mx[0], mx[1])
        hc = modulate(rmsnorm(cx, norm1_g[i]), mc[0], mc[1])
        j = i // 2
        if i % 2 == 0:
            dx, dc = neighbourhood_attention(hx, hc, na_w_qkv[j], na_q_g[j], na_k_g[j], na_rpb[j], na_w_o[j], not last)
        else:
            dx, dc = bidirectional_ssd(hx, hc, ssd_w_in[j], ssd_conv_w[j], ssd_conv_b[j], ssd_dt_bias[j],
                                       ssd_a_log[j], ssd_d[j], ssd_norm_g[j], ssd_w_out[j], not last)
        x = x + mx[2] * dx
        hx = modulate(rmsnorm(x, norm2_g[i]), mx[3], mx[4]).reshape(b * s, d)
        moe_p = (moe_w_router[i], moe_b_router[i], moe_w1[i], moe_b1[i], moe_w2[i], moe_b2[i])
        if last:
            x = x + mx[5] * moe(hx, *moe_p).reshape(b, s, d)
        else:
            cx = cx + mc[2] * dc
            hc = modulate(rmsnorm(cx, norm2_g[i]), mc[3], mc[4]).reshape(-1, d)
            y = moe(jnp.concatenate([hx, hc], axis=0), *moe_p)
            x = x + mx[5] * y[:b * s].reshape(b, s, d)
            cx = cx + mc[5] * y[b * s:].reshape(cx.shape)
    return x
```

```python
import functools

import jax
import jax.numpy as jnp
from jax import lax
from jax.experimental import pallas as pl
from jax.experimental.pallas import tpu as pltpu

F32 = jnp.float32
BF16 = jnp.bfloat16

D_MODEL = 1024
BATCH = 8
SEQ = 2048
DEPTH = 2
GRID_W = 64
CTX_LEN = 256
N_LAT = BATCH * SEQ
N_CTX = BATCH * CTX_LEN
N_ALL = N_LAT + N_CTX

NA_HEADS = 16
NA_HEAD_DIM = 64
NA_KR = 8
NA_KC = 16
NA_ROWS = SEQ // GRID_W
NA_QROWS = 4
NA_BAND = 12
NA_QBLK = NA_QROWS * GRID_W
NA_KBLK = NA_BAND * GRID_W

SSD_D_INNER = 2048
SSD_HEAD_DIM = 64
SSD_HEADS = 32
SSD_GROUPS = 8
SSD_HPG = 4
SSD_STATE = 128
SSD_CONV_W = 5
SSD_CHUNK = 128
SSD_CONV_DIM = SSD_D_INNER + 2 * SSD_GROUPS * SSD_STATE
SSD_MAIN_W = SSD_D_INNER + SSD_CONV_DIM

N_EXPERTS = 32
TOP_K = 4
D_FF = 1024
SWIGLU_ALPHA = 1.702
SWIGLU_LIMIT = 7.0
MOE_BLOCK = 256
NORM_EPS = 1e-6
NEG_BIG = -1e30

VMEM_LIMIT = 56 * 1024 * 1024


def _cparams(sem):
    return pltpu.CompilerParams(dimension_semantics=sem, vmem_limit_bytes=VMEM_LIMIT)


def _mod_row(i, tm):
    return jnp.minimum((i * tm) // SEQ, BATCH)


def _normmod_kernel(x_ref, g_ref, sh_ref, sc_ref, o_ref):
    x = x_ref[...]
    ms = jnp.mean(x * x, axis=-1, keepdims=True)
    y = x * lax.rsqrt(ms + NORM_EPS) * g_ref[...]
    o_ref[...] = (y * (1.0 + sc_ref[0]) + sh_ref[0]).astype(o_ref.dtype)


def _normmod_router_kernel(x_ref, g_ref, sh_ref, sc_ref, wr_ref, br_ref, o_ref, lg_ref):
    x = x_ref[...]
    ms = jnp.mean(x * x, axis=-1, keepdims=True)
    y = x * lax.rsqrt(ms + NORM_EPS) * g_ref[...]
    h = y * (1.0 + sc_ref[0]) + sh_ref[0]
    o_ref[...] = h.astype(o_ref.dtype)
    lg_ref[...] = jnp.dot(h, wr_ref[...], preferred_element_type=F32,
                          precision=lax.Precision.HIGHEST) + br_ref[...]


def _normmod(x, g, shift, scale, n_rows, router=None, tm=512):
    grid = (n_rows // tm,)
    x_spec = pl.BlockSpec((tm, D_MODEL), lambda i: (i, 0))
    g_spec = pl.BlockSpec((1, D_MODEL), lambda i: (0, 0))
    m_spec = pl.BlockSpec((1, 1, D_MODEL), lambda i: (_mod_row(i, tm), 0, 0))
    o_spec = pl.BlockSpec((tm, D_MODEL), lambda i: (i, 0))
    g2 = g.reshape(1, D_MODEL)
    if router is None:
        return pl.pallas_call(
            _normmod_kernel, grid=grid,
            in_specs=[x_spec, g_spec, m_spec, m_spec], out_specs=o_spec,
            out_shape=jax.ShapeDtypeStruct((n_rows, D_MODEL), BF16),
            compiler_params=_cparams(("arbitrary",)), name="normmod",
        )(x, g2, shift, scale)
    w_r, b_r = router
    return pl.pallas_call(
        _normmod_router_kernel, grid=grid,
        in_specs=[x_spec, g_spec, m_spec, m_spec,
                  pl.BlockSpec((D_MODEL, N_EXPERTS), lambda i: (0, 0)),
                  pl.BlockSpec((1, N_EXPERTS), lambda i: (0, 0))],
        out_specs=[o_spec, pl.BlockSpec((tm, N_EXPERTS), lambda i: (i, 0))],
        out_shape=[jax.ShapeDtypeStruct((n_rows, D_MODEL), BF16),
                   jax.ShapeDtypeStruct((n_rows, N_EXPERTS), F32)],
        compiler_params=_cparams(("arbitrary",)), name="normmod_router",
    )(x, g2, shift, scale, w_r, b_r.reshape(1, N_EXPERTS))


def _mm_kernel(a_ref, w_ref, o_ref, *, precision):
    o_ref[...] = jnp.dot(a_ref[...], w_ref[...], preferred_element_type=F32,
                         precision=precision).astype(o_ref.dtype)


def _mm_bias_kernel(a_ref, w_ref, b_ref, o_ref, *, precision):
    o_ref[...] = (jnp.dot(a_ref[...], w_ref[...], preferred_element_type=F32,
                          precision=precision) + b_ref[...]).astype(o_ref.dtype)


def _mm_res_kernel(a_ref, w_ref, res_ref, gate_ref, o_ref):
    acc = jnp.dot(a_ref[...], w_ref[...], preferred_element_type=F32)
    o_ref[...] = res_ref[...] + gate_ref[0] * acc


def _matmul(a, w, *, tm, tn, out_dtype=F32, bias=None, precision=None, name="mm"):
    m, k = a.shape
    n = w.shape[1]
    grid = (n // tn, m // tm)
    in_specs = [pl.BlockSpec((tm, k), lambda j, i: (i, 0)),
                pl.BlockSpec((k, tn), lambda j, i: (0, j))]
    args = [a, w]
    if bias is None:
        body = functools.partial(_mm_kernel, precision=precision)
    else:
        body = functools.partial(_mm_bias_kernel, precision=precision)
        in_specs.append(pl.BlockSpec((1, tn), lambda j, i: (0, j)))
        args.append(bias.reshape(1, n))
    return pl.pallas_call(
        body, grid=grid, in_specs=in_specs,
        out_specs=pl.BlockSpec((tm, tn), lambda j, i: (i, j)),
        out_shape=jax.ShapeDtypeStruct((m, n), out_dtype),
        compiler_params=_cparams(("arbitrary", "arbitrary")), name=name,
    )(*args)


def _matmul_residual(a, w, res, gate, *, tm, name):
    m, k = a.shape
    n = w.shape[1]
    return pl.pallas_call(
        _mm_res_kernel, grid=(m // tm,),
        in_specs=[pl.BlockSpec((tm, k), lambda i: (i, 0)),
                  pl.BlockSpec((k, n), lambda i: (0, 0)),
                  pl.BlockSpec((tm, n), lambda i: (i, 0)),
                  pl.BlockSpec((1, 1, n), lambda i: (_mod_row(i, tm), 0, 0))],
        out_specs=pl.BlockSpec((tm, n), lambda i: (i, 0)),
        out_shape=jax.ShapeDtypeStruct((m, n), F32),
        compiler_params=_cparams(("arbitrary",)), name=name,
    )(a, w, res, gate)


def _na_bias_table(rpb):
    col = jnp.arange(GRID_W)
    col_start = jnp.clip(col - NA_KC // 2, 0, GRID_W - NA_KC)
    col_ok = (col[None, :] >= col_start[:, None]) & (col[None, :] < col_start[:, None] + NA_KC)
    col_idx = jnp.clip(col[None, :] - col[:, None] + NA_KC - 1, 0, 2 * NA_KC - 2)
    tables = []
    for rb in (0, 1, NA_ROWS // NA_QROWS - 1):
        r0 = rb * NA_QROWS
        ks = min(max(r0 - NA_KR // 2, 0), NA_ROWS - NA_BAND)
        per_row = []
        for i in range(NA_QROWS):
            r = r0 + i
            rs = min(max(r - NA_KR // 2, 0), NA_ROWS - NA_KR)
            per_n = []
            for n in range(NA_BAND):
                kr = ks + n
                if rs <= kr < rs + NA_KR:
                    dr = kr - r + NA_KR - 1
                    t = rpb[:, dr][:, col_idx]
                    t = jnp.where(col_ok[None], t, NEG_BIG)
                else:
                    t = jnp.full((NA_HEADS, GRID_W, GRID_W), NEG_BIG, F32)
                per_n.append(t)
            per_row.append(jnp.concatenate(per_n, axis=-1))
        tables.append(jnp.concatenate(per_row, axis=1))
    return jnp.stack(tables, axis=0).astype(F32)


_NT = (((1,), (1,)), ((), ()))
_TN = (((0,), (0,)), ((), ()))


def _softmax_pv(s_list, v_list):
    m = s_list[0].max(axis=-1, keepdims=True)
    for s in s_list[1:]:
        m = jnp.maximum(m, s.max(axis=-1, keepdims=True))
    acc = None
    den = None
    for s, v in zip(s_list, v_list):
        p = jnp.exp(s - m)
        l = p.sum(axis=-1, keepdims=True)
        o = jnp.dot(p.astype(BF16), v, preferred_element_type=F32)
        acc = o if acc is None else acc + o
        den = l if den is None else den + l
    return acc / den


def _na_kernel(q_ref, k_ref, v_ref, kc_ref, vc_ref, tbl_ref, o_ref):
    rb = pl.program_id(2)
    ks = jnp.clip(rb * NA_QROWS - NA_KR // 2, 0, NA_ROWS - NA_BAND)
    kstart = pl.multiple_of(ks * GRID_W, GRID_W)
    q2 = q_ref[...]
    kb = k_ref[pl.ds(kstart, NA_KBLK), :]
    vb = v_ref[pl.ds(kstart, NA_KBLK), :]
    kc = kc_ref[...]
    vc = vc_ref[...]
    lane = lax.broadcasted_iota(jnp.int32, q2.shape, 1)
    outs = []
    for sub in range(2):
        sel = (lane < NA_HEAD_DIM) if sub == 0 else (lane >= NA_HEAD_DIM)
        qm = jnp.where(sel, q2, jnp.zeros_like(q2))
        s_lat = lax.dot_general(qm, kb, _NT, preferred_element_type=F32) + tbl_ref[0, sub]
        s_ctx = lax.dot_general(qm, kc, _NT, preferred_element_type=F32)
        outs.append(_softmax_pv([s_lat, s_ctx], [vb, vc]))
    o_ref[...] = jnp.where(lane < NA_HEAD_DIM, outs[0], outs[1]).astype(o_ref.dtype)


def _na_ctx_kernel(q_ref, k_ref, v_ref, o_ref):
    q2 = q_ref[...]
    kc = k_ref[...]
    vc = v_ref[...]
    lane = lax.broadcasted_iota(jnp.int32, q2.shape, 1)
    outs = []
    for sub in range(2):
        sel = (lane < NA_HEAD_DIM) if sub == 0 else (lane >= NA_HEAD_DIM)
        qm = jnp.where(sel, q2, jnp.zeros_like(q2))
        s_ctx = lax.dot_general(qm, kc, _NT, preferred_element_type=F32)
        outs.append(_softmax_pv([s_ctx], [vc]))
    o_ref[...] = jnp.where(lane < NA_HEAD_DIM, outs[0], outs[1]).astype(o_ref.dtype)


def _na_block_type(rb):
    nrb = NA_ROWS // NA_QROWS
    return jnp.where(rb == 0, 0, jnp.where(rb == nrb - 1, 2, 1))


def _neighbourhood_attention(q, k, v, table):
    nrb = NA_ROWS // NA_QROWS
    hp = NA_HEADS // 2
    lat_blocks = SEQ // NA_QBLK
    ctx_blk0 = N_LAT // CTX_LEN
    o_lat = pl.pallas_call(
        _na_kernel, grid=(BATCH, hp, nrb),
        in_specs=[
            pl.BlockSpec((NA_QBLK, 128), lambda b, h, r: (b * lat_blocks + r, h)),
            pl.BlockSpec((SEQ, 128), lambda b, h, r: (b, h)),
            pl.BlockSpec((SEQ, 128), lambda b, h, r: (b, h)),
            pl.BlockSpec((CTX_LEN, 128), lambda b, h, r: (ctx_blk0 + b, h)),
            pl.BlockSpec((CTX_LEN, 128), lambda b, h, r: (ctx_blk0 + b, h)),
            pl.BlockSpec((1, 2, NA_QBLK, NA_KBLK), lambda b, h, r: (_na_block_type(r), h, 0, 0)),
        ],
        out_specs=pl.BlockSpec((NA_QBLK, 128), lambda b, h, r: (b * lat_blocks + r, h)),
        out_shape=jax.ShapeDtypeStruct((N_LAT, D_MODEL), BF16),
        compiler_params=_cparams(("arbitrary", "arbitrary", "arbitrary")), name="na_attention",
    )(q, k, v, k, v, table)
    o_ctx = pl.pallas_call(
        _na_ctx_kernel, grid=(BATCH, hp),
        in_specs=[pl.BlockSpec((CTX_LEN, 128), lambda b, h: (ctx_blk0 + b, h))] * 3,
        out_specs=pl.BlockSpec((CTX_LEN, 128), lambda b, h: (b, h)),
        out_shape=jax.ShapeDtypeStruct((N_CTX, D_MODEL), BF16),
        compiler_params=_cparams(("arbitrary", "arbitrary")), name="na_ctx_attention",
    )(q, k, v)
    return o_lat, o_ctx


SSD_STEPS = (CTX_LEN + SEQ) // SSD_CHUNK
SSD_CTX_CHUNKS = CTX_LEN // SSD_CHUNK
SSD_LAT_CHUNKS = SEQ // SSD_CHUNK


def _ssd_chunk(b, d, j):
    jj = jnp.where(d == 0, j, jnp.where(j < SSD_CTX_CHUNKS, SSD_CTX_CHUNKS - 1 - j,
                                        SSD_STEPS + SSD_CTX_CHUNKS - 1 - j))
    return jnp.where(jj < SSD_CTX_CHUNKS,
                     N_LAT // SSD_CHUNK + SSD_CTX_CHUNKS * b + jj,
                     SSD_LAT_CHUNKS * b + jj - SSD_CTX_CHUNKS)


def _ssd_kernel(xd_ref, b_ref, c_ref, cum_ref, cumt_ref, y_ref, state_ref):
    d = pl.program_id(1)
    j = pl.program_id(2)

    @pl.when(j == 0)
    def _():
        state_ref[...] = jnp.zeros_like(state_ref)

    cum = cum_ref[...]
    cumt = cumt_ref[0]
    li = lax.broadcasted_iota(jnp.int32, (SSD_CHUNK, SSD_CHUNK), 0)
    si = lax.broadcasted_iota(jnp.int32, (SSD_CHUNK, SSD_CHUNK), 1)
    fwd = d == 0
    mask = jnp.where(fwd, li - si, si - li) >= 0
    tot_row = jnp.where(fwd, cum[SSD_CHUNK - 1:SSD_CHUNK, :], cum[0:1, :])
    for g in range(SSD_GROUPS):
        bg = b_ref[:, g * SSD_STATE:(g + 1) * SSD_STATE]
        cg = c_ref[:, g * SSD_STATE:(g + 1) * SSD_STATE]
        cb = lax.dot_general(cg, bg, _NT, preferred_element_type=F32)
        for r in range(SSD_HPG):
            h = g * SSD_HPG + r
            ccol = cum[:, h:h + 1]
            crow = cumt[h:h + 1, :]
            decay = jnp.exp(jnp.where(mask, ccol - crow, NEG_BIG))
            mh = (cb * decay).astype(BF16)
            xh = xd_ref[:, h * SSD_HEAD_DIM:(h + 1) * SSD_HEAD_DIM]
            st = state_ref[h]
            y_diag = jnp.dot(mh, xh, preferred_element_type=F32)
            y_off = lax.dot_general(cg, st.astype(BF16), _NT, preferred_element_type=F32)
            y_ref[:, h * SSD_HEAD_DIM:(h + 1) * SSD_HEAD_DIM] = y_diag + y_off * jnp.exp(ccol)
            tot = tot_row[:, h:h + 1]
            xdec = (xh.astype(F32) * jnp.exp(tot - ccol)).astype(BF16)
            s_new = lax.dot_general(xdec, bg, _TN, preferred_element_type=F32)
            state_ref[h] = jnp.exp(tot) * st + s_new


def _ssd_scan(xd, bm, cm, cum, cumt):
    return pl.pallas_call(
        _ssd_kernel, grid=(BATCH, 2, SSD_STEPS),
        in_specs=[
            pl.BlockSpec((None, SSD_CHUNK, SSD_D_INNER), lambda b, d, j: (d, _ssd_chunk(b, d, j), 0)),
            pl.BlockSpec((SSD_CHUNK, SSD_GROUPS * SSD_STATE), lambda b, d, j: (_ssd_chunk(b, d, j), 0)),
            pl.BlockSpec((SSD_CHUNK, SSD_GROUPS * SSD_STATE), lambda b, d, j: (_ssd_chunk(b, d, j), 0)),
            pl.BlockSpec((None, SSD_CHUNK, SSD_HEADS), lambda b, d, j: (d, _ssd_chunk(b, d, j), 0)),
            pl.BlockSpec((None, 1, SSD_HEADS, SSD_CHUNK), lambda b, d, j: (d, _ssd_chunk(b, d, j), 0, 0)),
        ],
        out_specs=pl.BlockSpec((None, SSD_CHUNK, SSD_D_INNER), lambda b, d, j: (d, _ssd_chunk(b, d, j), 0)),
        out_shape=jax.ShapeDtypeStruct((2, N_ALL, SSD_D_INNER), F32),
        scratch_shapes=[pltpu.VMEM((SSD_HEADS, SSD_HEAD_DIM, SSD_STATE), F32)],
        compiler_params=_cparams(("arbitrary", "arbitrary", "arbitrary")), name="ssd_scan",
    )(xd, bm, cm, cum, cumt)


def _depthwise_conv(u, w, bias, seq_len):
    rows, ch = u.shape
    u3 = u.reshape(rows // seq_len, seq_len, ch)
    pad = SSD_CONV_W // 2
    up = jnp.pad(u3, ((0, 0), (pad, pad), (0, 0)))
    out = bias[None, None, :]
    for t in range(SSD_CONV_W):
        out = out + up[:, t:t + seq_len, :] * w[t][None, None, :]
    return out.reshape(rows, ch)


def _moe_kernel(be_ref, nreal_ref, x_ref, w1_ref, b1_ref, w2_ref, b2_ref, g_ref, o_ref):
    i = pl.program_id(0)

    @pl.when(i < nreal_ref[0])
    def _():
        h = jnp.dot(x_ref[...], w1_ref[...], preferred_element_type=F32) + b1_ref[0]
        glu = jnp.minimum(h[:, :D_FF], SWIGLU_LIMIT)
        lin = jnp.clip(h[:, D_FF:], -SWIGLU_LIMIT, SWIGLU_LIMIT)
        act = glu * jax.nn.sigmoid(SWIGLU_ALPHA * glu) * (lin + 1.0)
        y = jnp.dot(act.astype(BF16), w2_ref[...], preferred_element_type=F32) + b2_ref[0]
        o_ref[...] = y * g_ref[...]

    @pl.when(i >= nreal_ref[0])
    def _():
        o_ref[...] = jnp.zeros_like(o_ref)


def _moe_experts(xb, slot_gate, block_expert, n_real, w1, b1, w2, b2):
    n_slots = xb.shape[0]
    n_blocks = n_slots // MOE_BLOCK
    grid_spec = pltpu.PrefetchScalarGridSpec(
        num_scalar_prefetch=2, grid=(n_blocks,),
        in_specs=[
            pl.BlockSpec((MOE_BLOCK, D_MODEL), lambda i, be, nr: (i, 0)),
            pl.BlockSpec((None, D_MODEL, 2 * D_FF), lambda i, be, nr: (be[i], 0, 0)),
            pl.BlockSpec((1, 1, 2 * D_FF), lambda i, be, nr: (be[i], 0, 0)),
            pl.BlockSpec((None, D_FF, D_MODEL), lambda i, be, nr: (be[i], 0, 0)),
            pl.BlockSpec((1, 1, D_MODEL), lambda i, be, nr: (be[i], 0, 0)),
            pl.BlockSpec((MOE_BLOCK, 1), lambda i, be, nr: (i, 0)),
        ],
        out_specs=pl.BlockSpec((MOE_BLOCK, D_MODEL), lambda i, be, nr: (i, 0)),
    )
    return pl.pallas_call(
        _moe_kernel, grid_spec=grid_spec,
        out_shape=jax.ShapeDtypeStruct((n_slots, D_MODEL), F32),
        compiler_params=_cparams(("arbitrary",)), name="moe_experts",
    )(block_expert, n_real, xb, w1, b1.reshape(N_EXPERTS, 1, 2 * D_FF), w2,
      b2.reshape(N_EXPERTS, 1, D_MODEL), slot_gate)


def _moe(h, logits, w1, b1, w2, b2):
    n_tok = h.shape[0]
    nk = n_tok * TOP_K
    top_v, top_e = lax.top_k(logits, TOP_K)
    gates = jax.nn.softmax(top_v, axis=-1)
    flat_e = top_e.reshape(-1)
    onehot = (flat_e[:, None] == jnp.arange(N_EXPERTS, dtype=flat_e.dtype)[None, :]).astype(jnp.int32)
    csum = jnp.cumsum(onehot, axis=0)
    rank = jnp.take_along_axis(csum, flat_e[:, None], axis=1)[:, 0] - 1
    counts = csum[-1]
    padded = (counts + MOE_BLOCK - 1) // MOE_BLOCK * MOE_BLOCK
    pend = jnp.cumsum(padded)
    pstart = pend - padded
    dest = (pstart[flat_e] + rank).astype(jnp.int32)
    n_blocks = -(-(nk + N_EXPERTS * (MOE_BLOCK - 1)) // MOE_BLOCK)
    n_slots = n_blocks * MOE_BLOCK
    flat_tok = jnp.arange(nk, dtype=jnp.int32) // TOP_K
    slot_tok = jnp.zeros((n_slots,), jnp.int32).at[dest].set(flat_tok)
    slot_gate = jnp.zeros((n_slots,), F32).at[dest].set(gates.reshape(-1))
    block_expert = jnp.minimum(
        jnp.searchsorted(pend, jnp.arange(n_blocks, dtype=jnp.int32) * MOE_BLOCK, side='right'),
        N_EXPERTS - 1).astype(jnp.int32)
    n_real = (pend[-1] // MOE_BLOCK).astype(jnp.int32).reshape(1)
    xb = jnp.take(h, slot_tok, axis=0)
    yb = _moe_experts(xb, slot_gate.reshape(n_slots, 1), block_expert, n_real, w1, b1, w2, b2)
    return jnp.take(yb, dest.reshape(n_tok, TOP_K), axis=0).sum(axis=1)


def _rms(x, g):
    return x * lax.rsqrt(jnp.mean(x * x, axis=-1, keepdims=True) + NORM_EPS) * g


def _expand_rows(t):
    lat = jnp.broadcast_to(t[:BATCH, None, :], (BATCH, SEQ, D_MODEL)).reshape(N_LAT, D_MODEL)
    ctx = jnp.broadcast_to(t[BATCH:BATCH + 1], (N_CTX, D_MODEL))
    return jnp.concatenate([lat, ctx], axis=0)


def kernel(x, c, ctx, c_ctx, ada_w, ada_b, norm1_g, norm2_g, na_w_qkv, na_q_g, na_k_g, na_rpb, na_w_o,
           ssd_w_in, ssd_conv_w, ssd_conv_b, ssd_dt_bias, ssd_a_log, ssd_d, ssd_norm_g, ssd_w_out,
           moe_w_router, moe_b_router, moe_w1, moe_b1, moe_w2, moe_b2):
    xa = jnp.concatenate([x.reshape(N_LAT, D_MODEL), ctx.reshape(N_CTX, D_MODEL)], axis=0)
    cond = jnp.concatenate([jax.nn.silu(c), jax.nn.silu(c_ctx)[None, :],
                            jnp.zeros((16 - BATCH - 1, D_MODEL), F32)], axis=0)
    for i in range(DEPTH):
        last = i == DEPTH - 1
        mod = _matmul(cond, ada_w[i], tm=16, tn=D_MODEL, bias=ada_b[i],
                      precision=lax.Precision.HIGHEST, name="ada_mod")
        m = [mod[:, t * D_MODEL:(t + 1) * D_MODEL] for t in range(6)]
        mt = [t.reshape(16, 1, D_MODEL) for t in m]
        h1 = _normmod(xa, norm1_g[i], mt[0], mt[1], N_ALL)
        j = i // 2
        if i % 2 == 0:
            qkv = _matmul(h1, na_w_qkv[j].astype(BF16), tm=512, tn=1024, name="qkv_proj")
            q, k, v = jnp.split(qkv, 3, axis=-1)
            scale = NA_HEAD_DIM ** -0.5
            q = (_rms(q.reshape(N_ALL, NA_HEADS, NA_HEAD_DIM), na_q_g[j]) * scale).reshape(N_ALL, D_MODEL)
            k = _rms(k.reshape(N_ALL, NA_HEADS, NA_HEAD_DIM), na_k_g[j]).reshape(N_ALL, D_MODEL)
            o_lat, o_ctx = _neighbourhood_attention(q.astype(BF16), k.astype(BF16), v.astype(BF16),
                                                    _na_bias_table(na_rpb[j]))
            n_out = N_LAT if last else N_ALL
            o_all = jnp.concatenate([o_lat, o_ctx], axis=0)[:n_out]
            xa_new = _matmul_residual(o_all, na_w_o[j].astype(BF16), xa, mt[2], tm=512, name="na_out_proj")
        else:
            w_in = ssd_w_in[j]
            zxbc = _matmul(h1, w_in[:, :SSD_MAIN_W].astype(BF16), tm=512, tn=1024, name="ssd_in_proj")
            dt_raw = _matmul(h1, w_in[:, SSD_MAIN_W:].astype(BF16), tm=512, tn=2 * SSD_HEADS,
                             name="ssd_dt_proj")
            z = zxbc[:, :SSD_D_INNER]
            xbc = zxbc[:, SSD_D_INNER:]
            xbc = jnp.concatenate([
                _depthwise_conv(xbc[:N_LAT], ssd_conv_w[j], ssd_conv_b[j], SEQ),
                _depthwise_conv(xbc[N_LAT:], ssd_conv_w[j], ssd_conv_b[j], CTX_LEN)], axis=0)
            xbc = jax.nn.silu(xbc)
            xs = xbc[:, :SSD_D_INNER]
            bm = xbc[:, SSD_D_INNER:SSD_D_INNER + SSD_GROUPS * SSD_STATE]
            cm = xbc[:, SSD_D_INNER + SSD_GROUPS * SSD_STATE:]
            dt = jax.nn.softplus(dt_raw.reshape(N_ALL, 2, SSD_HEADS) + ssd_dt_bias[j][None])
            dt = jnp.moveaxis(dt, 1, 0)
            a = -jnp.exp(ssd_a_log[j])
            da = (dt * a[:, None, :]).reshape(2, N_ALL // SSD_CHUNK, SSD_CHUNK, SSD_HEADS)
            cum_f = jnp.cumsum(da[0], axis=1)
            cum_b = jnp.flip(jnp.cumsum(jnp.flip(da[1], axis=1), axis=1), axis=1)
            cum = jnp.stack([cum_f, cum_b], axis=0)
            cumt = jnp.swapaxes(cum, 2, 3)
            xd = (xs.reshape(1, N_ALL, SSD_HEADS, SSD_HEAD_DIM) * dt[..., None]).reshape(2, N_ALL, SSD_D_INNER)
            y2 = _ssd_scan(xd.astype(BF16), bm.astype(BF16), cm.astype(BF16),
                           cum.reshape(2, N_ALL, SSD_HEADS), cumt)
            n_out = N_LAT if last else N_ALL
            y = (y2[0, :n_out] + y2[1, :n_out]
                 + (ssd_d[j][None, :, None] * xs[:n_out].reshape(n_out, SSD_HEADS, SSD_HEAD_DIM)
                    ).reshape(n_out, SSD_D_INNER))
            gated = (y * jax.nn.silu(z[:n_out])).reshape(n_out, SSD_GROUPS, SSD_D_INNER // SSD_GROUPS)
            gated = _rms(gated, ssd_norm_g[j].reshape(SSD_GROUPS, -1)).reshape(n_out, SSD_D_INNER)
            xa_new = _matmul_residual(gated.astype(BF16), ssd_w_out[j].astype(BF16), xa, mt[2], tm=512,
                                      name="ssd_out_proj")
        n_tok = xa_new.shape[0]
        h2, logits = _normmod(xa_new, norm2_g[i], mt[3], mt[4], n_tok,
                              router=(moe_w_router[i], moe_b_router[i]))
        y_moe = _moe(h2, logits, moe_w1[i].astype(BF16), moe_b1[i], moe_w2[i].astype(BF16), moe_b2[i])
        xa = xa_new + _expand_rows(m[5])[:n_tok] * y_moe
    return xa[:N_LAT].reshape(BATCH, SEQ, D_MODEL)
```

```python
import functools

import jax
import jax.numpy as jnp
from jax import lax
from jax.experimental import pallas as pl
from jax.experimental.pallas import tpu as pltpu

F32 = jnp.float32
BF16 = jnp.bfloat16

D_MODEL = 1024
BATCH = 8
SEQ = 2048
DEPTH = 2
GRID_W = 64
CTX_LEN = 256
N_LAT = BATCH * SEQ
N_CTX = BATCH * CTX_LEN
N_ALL = N_LAT + N_CTX

NA_HEADS = 16
NA_HEAD_DIM = 64
NA_KR = 8
NA_KC = 16
NA_ROWS = SEQ // GRID_W
NA_QROWS = 4
NA_BAND = 12
NA_QBLK = NA_QROWS * GRID_W
NA_KBLK = NA_BAND * GRID_W

SSD_D_INNER = 2048
SSD_HEAD_DIM = 64
SSD_HEADS = 32
SSD_GROUPS = 8
SSD_HPG = 4
SSD_STATE = 128
SSD_CONV_W = 5
SSD_CHUNK = 128
SSD_CONV_DIM = SSD_D_INNER + 2 * SSD_GROUPS * SSD_STATE
SSD_MAIN_W = SSD_D_INNER + SSD_CONV_DIM

N_EXPERTS = 32
TOP_K = 4
D_FF = 1024
SWIGLU_ALPHA = 1.702
SWIGLU_LIMIT = 7.0
MOE_BLOCK = 256
NORM_EPS = 1e-6
NEG_BIG = -1e30

VMEM_LIMIT = 56 * 1024 * 1024


def _cparams(sem):
    return pltpu.CompilerParams(dimension_semantics=sem, vmem_limit_bytes=VMEM_LIMIT)


def _mod_row(i, tm):
    return jnp.minimum((i * tm) // SEQ, BATCH)


def _normmod_kernel(x_ref, g_ref, sh_ref, sc_ref, o_ref):
    x = x_ref[...]
    ms = jnp.mean(x * x, axis=-1, keepdims=True)
    y = x * lax.rsqrt(ms + NORM_EPS) * g_ref[...]
    o_ref[...] = (y * (1.0 + sc_ref[0]) + sh_ref[0]).astype(o_ref.dtype)


def _normmod_router_kernel(x_ref, g_ref, sh_ref, sc_ref, wr_ref, br_ref, o_ref, lg_ref):
    x = x_ref[...]
    ms = jnp.mean(x * x, axis=-1, keepdims=True)
    y = x * lax.rsqrt(ms + NORM_EPS) * g_ref[...]
    h = y * (1.0 + sc_ref[0]) + sh_ref[0]
    o_ref[...] = h.astype(o_ref.dtype)
    lg_ref[...] = jnp.dot(h, wr_ref[...], preferred_element_type=F32,
                          precision=lax.Precision.HIGHEST) + br_ref[...]


def _normmod(x, g, shift, scale, n_rows, router=None, tm=512):
    grid = (n_rows // tm,)
    x_spec = pl.BlockSpec((tm, D_MODEL), lambda i: (i, 0))
    g_spec = pl.BlockSpec((1, D_MODEL), lambda i: (0, 0))
    m_spec = pl.BlockSpec((1, 1, D_MODEL), lambda i: (_mod_row(i, tm), 0, 0))
    o_spec = pl.BlockSpec((tm, D_MODEL), lambda i: (i, 0))
    g2 = g.reshape(1, D_MODEL)
    if router is None:
        return pl.pallas_call(
            _normmod_kernel, grid=grid,
            in_specs=[x_spec, g_spec, m_spec, m_spec], out_specs=o_spec,
            out_shape=jax.ShapeDtypeStruct((n_rows, D_MODEL), BF16),
            compiler_params=_cparams(("arbitrary",)), name="normmod",
        )(x, g2, shift, scale)
    w_r, b_r = router
    return pl.pallas_call(
        _normmod_router_kernel, grid=grid,
        in_specs=[x_spec, g_spec, m_spec, m_spec,
                  pl.BlockSpec((D_MODEL, N_EXPERTS), lambda i: (0, 0)),
                  pl.BlockSpec((1, N_EXPERTS), lambda i: (0, 0))],
        out_specs=[o_spec, pl.BlockSpec((tm, N_EXPERTS), lambda i: (i, 0))],
        out_shape=[jax.ShapeDtypeStruct((n_rows, D_MODEL), F32),
                   jax.ShapeDtypeStruct((n_rows, N_EXPERTS), F32)],
        compiler_params=_cparams(("arbitrary",)), name="normmod_router",
    )(x, g2, shift, scale, w_r, b_r.reshape(1, N_EXPERTS))


def _mm_kernel(a_ref, w_ref, o_ref, *, precision):
    o_ref[...] = jnp.dot(a_ref[...], w_ref[...], preferred_element_type=F32,
                         precision=precision).astype(o_ref.dtype)


def _mm_bias_kernel(a_ref, w_ref, b_ref, o_ref, *, precision):
    o_ref[...] = (jnp.dot(a_ref[...], w_ref[...], preferred_element_type=F32,
                          precision=precision) + b_ref[...]).astype(o_ref.dtype)


def _mm_res_kernel(a_ref, w_ref, res_ref, gate_ref, o_ref):
    acc = jnp.dot(a_ref[...], w_ref[...], preferred_element_type=F32)
    o_ref[...] = res_ref[...] + gate_ref[0] * acc


def _matmul(a, w, *, tm, tn, out_dtype=F32, bias=None, precision=None, name="mm"):
    m, k = a.shape
    n = w.shape[1]
    grid = (n // tn, m // tm)
    in_specs = [pl.BlockSpec((tm, k), lambda j, i: (i, 0)),
                pl.BlockSpec((k, tn), lambda j, i: (0, j))]
    args = [a, w]
    if bias is None:
        body = functools.partial(_mm_kernel, precision=precision)
    else:
        body = functools.partial(_mm_bias_kernel, precision=precision)
        in_specs.append(pl.BlockSpec((1, tn), lambda j, i: (0, j)))
        args.append(bias.reshape(1, n))
    return pl.pallas_call(
        body, grid=grid, in_specs=in_specs,
        out_specs=pl.BlockSpec((tm, tn), lambda j, i: (i, j)),
        out_shape=jax.ShapeDtypeStruct((m, n), out_dtype),
        compiler_params=_cparams(("arbitrary", "arbitrary")), name=name,
    )(*args)


def _matmul_residual(a, w, res, gate, *, tm, name):
    m, k = a.shape
    n = w.shape[1]
    return pl.pallas_call(
        _mm_res_kernel, grid=(m // tm,),
        in_specs=[pl.BlockSpec((tm, k), lambda i: (i, 0)),
                  pl.BlockSpec((k, n), lambda i: (0, 0)),
                  pl.BlockSpec((tm, n), lambda i: (i, 0)),
                  pl.BlockSpec((1, 1, n), lambda i: (_mod_row(i, tm), 0, 0))],
        out_specs=pl.BlockSpec((tm, n), lambda i: (i, 0)),
        out_shape=jax.ShapeDtypeStruct((m, n), F32),
        compiler_params=_cparams(("arbitrary",)), name=name,
    )(a, w, res, gate)


def _split3(x):
    hi = x.astype(BF16)
    r1 = x - hi.astype(F32)
    mid = r1.astype(BF16)
    lo = (r1 - mid.astype(F32)).astype(BF16)
    return hi, mid, lo


def _dot3(x, sel):
    hi, mid, lo = _split3(x)
    return (jnp.dot(hi, sel, preferred_element_type=F32) + jnp.dot(mid, sel, preferred_element_type=F32)
            + jnp.dot(lo, sel, preferred_element_type=F32))


def _qkv_kernel(a_ref, w_ref, gain_ref, seg_ref, segt_ref, o_ref):
    j = pl.program_id(0)
    acc = jnp.dot(a_ref[...], w_ref[...], preferred_element_type=F32)

    @pl.when(j < 2)
    def _():
        ss = _dot3(acc * acc, seg_ref[...])
        inv = lax.rsqrt(ss * (1.0 / NA_HEAD_DIM) + NORM_EPS)
        o_ref[...] = (acc * _dot3(inv, segt_ref[...]) * gain_ref[0]).astype(o_ref.dtype)

    @pl.when(j == 2)
    def _():
        o_ref[...] = acc.astype(o_ref.dtype)


def _qkv_proj(h, w_qkv, q_g, k_g, tm=512):
    m = h.shape[0]
    head_of_col = jnp.arange(D_MODEL) // NA_HEAD_DIM
    seg = (head_of_col[:, None] == jnp.arange(128)[None, :]).astype(BF16)
    gain = jnp.stack([jnp.tile(q_g, NA_HEADS) * NA_HEAD_DIM ** -0.5, jnp.tile(k_g, NA_HEADS),
                      jnp.ones((D_MODEL,), F32)]).reshape(3, 1, D_MODEL)
    return pl.pallas_call(
        _qkv_kernel, grid=(3, m // tm),
        in_specs=[pl.BlockSpec((tm, D_MODEL), lambda j, i: (i, 0)),
                  pl.BlockSpec((D_MODEL, D_MODEL), lambda j, i: (0, j)),
                  pl.BlockSpec((1, 1, D_MODEL), lambda j, i: (j, 0, 0)),
                  pl.BlockSpec((D_MODEL, 128), lambda j, i: (0, 0)),
                  pl.BlockSpec((128, D_MODEL), lambda j, i: (0, 0))],
        out_specs=pl.BlockSpec((tm, D_MODEL), lambda j, i: (i, j)),
        out_shape=jax.ShapeDtypeStruct((m, 3 * D_MODEL), BF16),
        compiler_params=_cparams(("arbitrary", "arbitrary")), name="qkv_proj",
    )(h, w_qkv, gain, seg, seg.T)


def _na_bias_table(rpb):
    col = jnp.arange(GRID_W)
    col_start = jnp.clip(col - NA_KC // 2, 0, GRID_W - NA_KC)
    col_ok = (col[None, :] >= col_start[:, None]) & (col[None, :] < col_start[:, None] + NA_KC)
    col_idx = jnp.clip(col[None, :] - col[:, None] + NA_KC - 1, 0, 2 * NA_KC - 2)
    tables = []
    for rb in (0, 1, NA_ROWS // NA_QROWS - 1):
        r0 = rb * NA_QROWS
        ks = min(max(r0 - NA_KR // 2, 0), NA_ROWS - NA_BAND)
        per_row = []
        for i in range(NA_QROWS):
            r = r0 + i
            rs = min(max(r - NA_KR // 2, 0), NA_ROWS - NA_KR)
            per_n = []
            for n in range(NA_BAND):
                kr = ks + n
                if rs <= kr < rs + NA_KR:
                    dr = kr - r + NA_KR - 1
                    t = rpb[:, dr][:, col_idx]
                    t = jnp.where(col_ok[None], t, NEG_BIG)
                else:
                    t = jnp.full((NA_HEADS, GRID_W, GRID_W), NEG_BIG, F32)
                per_n.append(t)
            per_row.append(jnp.concatenate(per_n, axis=-1))
        tables.append(jnp.concatenate(per_row, axis=1))
    return jnp.stack(tables, axis=0).astype(F32)


_NT = (((1,), (1,)), ((), ()))
_TN = (((0,), (0,)), ((), ()))


def _softmax_pv(s_list, v_list):
    m = s_list[0].max(axis=-1, keepdims=True)
    for s in s_list[1:]:
        m = jnp.maximum(m, s.max(axis=-1, keepdims=True))
    acc = None
    den = None
    for s, v in zip(s_list, v_list):
        p = jnp.exp(s - m)
        l = p.sum(axis=-1, keepdims=True)
        o = jnp.dot(p.astype(BF16), v, preferred_element_type=F32)
        acc = o if acc is None else acc + o
        den = l if den is None else den + l
    return acc / den


NA_LAT_STEPS = NA_ROWS // NA_QROWS


def _na_kernel(q_ref, k_ref, v_ref, kc_ref, vc_ref, tbl_ref, o_ref):
    rb = pl.program_id(2)
    q2 = q_ref[...]
    kc = kc_ref[...]
    vc = vc_ref[...]
    lane = lax.broadcasted_iota(jnp.int32, q2.shape, 1)

    def head_queries(sub):
        sel = (lane < NA_HEAD_DIM) if sub == 0 else (lane >= NA_HEAD_DIM)
        return jnp.where(sel, q2, jnp.zeros_like(q2))

    @pl.when(rb < NA_LAT_STEPS)
    def _():
        ks = jnp.clip(rb * NA_QROWS - NA_KR // 2, 0, NA_ROWS - NA_BAND)
        kstart = pl.multiple_of(ks * GRID_W, GRID_W)
        kb = k_ref[pl.ds(kstart, NA_KBLK), :]
        vb = v_ref[pl.ds(kstart, NA_KBLK), :]
        outs = []
        for sub in range(2):
            qm = head_queries(sub)
            s_lat = lax.dot_general(qm, kb, _NT, preferred_element_type=F32) + tbl_ref[0, sub]
            s_ctx = lax.dot_general(qm, kc, _NT, preferred_element_type=F32)
            outs.append(_softmax_pv([s_lat, s_ctx], [vb, vc]))
        o_ref[...] = jnp.where(lane < NA_HEAD_DIM, outs[0], outs[1]).astype(o_ref.dtype)

    @pl.when(rb == NA_LAT_STEPS)
    def _():
        outs = []
        for sub in range(2):
            s_ctx = lax.dot_general(head_queries(sub), kc, _NT, preferred_element_type=F32)
            outs.append(_softmax_pv([s_ctx], [vc]))
        o_ref[...] = jnp.where(lane < NA_HEAD_DIM, outs[0], outs[1]).astype(o_ref.dtype)


def _na_block_type(rb):
    return jnp.where(rb == 0, 0, jnp.where(rb >= NA_LAT_STEPS - 1, 2, 1))


def _neighbourhood_attention(qkv, table):
    hp = NA_HEADS // 2
    ctx_blk0 = N_LAT // CTX_LEN

    def q_block(b, h, r):
        return (jnp.where(r < NA_LAT_STEPS, b * NA_LAT_STEPS + r, ctx_blk0 + b), h)

    return pl.pallas_call(
        _na_kernel, grid=(BATCH, hp, NA_LAT_STEPS + 1),
        in_specs=[
            pl.BlockSpec((NA_QBLK, 128), q_block),
            pl.BlockSpec((SEQ, 128), lambda b, h, r: (b, hp + h)),
            pl.BlockSpec((SEQ, 128), lambda b, h, r: (b, 2 * hp + h)),
            pl.BlockSpec((CTX_LEN, 128), lambda b, h, r: (ctx_blk0 + b, hp + h)),
            pl.BlockSpec((CTX_LEN, 128), lambda b, h, r: (ctx_blk0 + b, 2 * hp + h)),
            pl.BlockSpec((1, 2, NA_QBLK, NA_KBLK), lambda b, h, r: (_na_block_type(r), h, 0, 0)),
        ],
        out_specs=pl.BlockSpec((NA_QBLK, 128), q_block),
        out_shape=jax.ShapeDtypeStruct((N_ALL, D_MODEL), BF16),
        compiler_params=_cparams(("arbitrary", "arbitrary", "arbitrary")), name="na_attention",
    )(qkv, qkv, qkv, qkv, qkv, table)


SSD_STEPS = (CTX_LEN + SEQ) // SSD_CHUNK
SSD_CTX_CHUNKS = CTX_LEN // SSD_CHUNK
SSD_LAT_CHUNKS = SEQ // SSD_CHUNK


def _ssd_chunk(b, d, j):
    jj = jnp.where(d == 0, j, jnp.where(j < SSD_CTX_CHUNKS, SSD_CTX_CHUNKS - 1 - j,
                                        SSD_STEPS + SSD_CTX_CHUNKS - 1 - j))
    return jnp.where(jj < SSD_CTX_CHUNKS,
                     N_LAT // SSD_CHUNK + SSD_CTX_CHUNKS * b + jj,
                     SSD_LAT_CHUNKS * b + jj - SSD_CTX_CHUNKS)


def _ssd_kernel(xd_ref, b_ref, c_ref, cum_ref, cumt_ref, y_ref, state_ref):
    d = pl.program_id(1)
    j = pl.program_id(2)

    @pl.when(j == 0)
    def _():
        state_ref[...] = jnp.zeros_like(state_ref)

    cum = cum_ref[...]
    cumt = cumt_ref[0]
    li = lax.broadcasted_iota(jnp.int32, (SSD_CHUNK, SSD_CHUNK), 0)
    si = lax.broadcasted_iota(jnp.int32, (SSD_CHUNK, SSD_CHUNK), 1)
    fwd = d == 0
    mask = jnp.where(fwd, li - si, si - li) >= 0
    tot_row = jnp.where(fwd, cum[SSD_CHUNK - 1:SSD_CHUNK, :], cum[0:1, :])
    for g in range(SSD_GROUPS):
        bg = b_ref[:, g * SSD_STATE:(g + 1) * SSD_STATE]
        cg = c_ref[:, g * SSD_STATE:(g + 1) * SSD_STATE]
        cb = lax.dot_general(cg, bg, _NT, preferred_element_type=F32)
        for r in range(SSD_HPG):
            h = g * SSD_HPG + r
            ccol = cum[:, h:h + 1]
            crow = cumt[h:h + 1, :]
            decay = jnp.exp(jnp.where(mask, ccol - crow, NEG_BIG))
            mh = (cb * decay).astype(BF16)
            xh = xd_ref[:, h * SSD_HEAD_DIM:(h + 1) * SSD_HEAD_DIM]
            st = state_ref[h]
            y_diag = jnp.dot(mh, xh, preferred_element_type=F32)
            y_off = lax.dot_general(cg, st.astype(BF16), _NT, preferred_element_type=F32)
            y_ref[:, h * SSD_HEAD_DIM:(h + 1) * SSD_HEAD_DIM] = y_diag + y_off * jnp.exp(ccol)
            tot = tot_row[:, h:h + 1]
            xdec = (xh.astype(F32) * jnp.exp(tot - ccol)).astype(BF16)
            s_new = lax.dot_general(xdec, bg, _TN, preferred_element_type=F32)
            state_ref[h] = jnp.exp(tot) * st + s_new


def _ssd_scan(xd, bm, cm, cum, cumt):
    return pl.pallas_call(
        _ssd_kernel, grid=(BATCH, 2, SSD_STEPS),
        in_specs=[
            pl.BlockSpec((None, SSD_CHUNK, SSD_D_INNER), lambda b, d, j: (d, _ssd_chunk(b, d, j), 0)),
            pl.BlockSpec((SSD_CHUNK, SSD_GROUPS * SSD_STATE), lambda b, d, j: (_ssd_chunk(b, d, j), 0)),
            pl.BlockSpec((SSD_CHUNK, SSD_GROUPS * SSD_STATE), lambda b, d, j: (_ssd_chunk(b, d, j), 0)),
            pl.BlockSpec((None, SSD_CHUNK, SSD_HEADS), lambda b, d, j: (d, _ssd_chunk(b, d, j), 0)),
            pl.BlockSpec((None, 1, SSD_HEADS, SSD_CHUNK), lambda b, d, j: (d, _ssd_chunk(b, d, j), 0, 0)),
        ],
        out_specs=pl.BlockSpec((None, SSD_CHUNK, SSD_D_INNER), lambda b, d, j: (d, _ssd_chunk(b, d, j), 0)),
        out_shape=jax.ShapeDtypeStruct((2, N_ALL, SSD_D_INNER), F32),
        scratch_shapes=[pltpu.VMEM((SSD_HEADS, SSD_HEAD_DIM, SSD_STATE), F32)],
        compiler_params=_cparams(("arbitrary", "arbitrary", "arbitrary")), name="ssd_scan",
    )(xd, bm, cm, cum, cumt)


def _depthwise_conv(u, w, bias, seq_len):
    rows, ch = u.shape
    u3 = u.reshape(rows // seq_len, seq_len, ch)
    pad = SSD_CONV_W // 2
    up = jnp.pad(u3, ((0, 0), (pad, pad), (0, 0)))
    out = bias[None, None, :]
    for t in range(SSD_CONV_W):
        out = out + up[:, t:t + seq_len, :] * w[t][None, None, :]
    return out.reshape(rows, ch)


def _row_gather_copy(src_hbm, row, dst, sem):
    return pltpu.make_async_copy(src_hbm.at[pl.ds(row, 1)], dst, sem)


def _moe_kernel(be_ref, nreal_ref, tok_ref, x_hbm, w1_ref, b1_ref, w2_ref, b2_ref, g_ref, o_ref, xbuf, sem):
    i = pl.program_id(0)
    nreal = nreal_ref[0]

    def issue(blk, slot):
        base = blk * MOE_BLOCK

        def body(r, carry):
            _row_gather_copy(x_hbm, tok_ref[base + r], xbuf.at[slot, pl.ds(r, 1)], sem.at[slot]).start()
            return carry

        lax.fori_loop(0, MOE_BLOCK, body, 0, unroll=8)

    @pl.when(i == 0)
    def _():
        issue(0, 0)

    @pl.when(i + 1 < nreal)
    def _():
        issue(i + 1, (i + 1) % 2)

    @pl.when(i < nreal)
    def _():
        slot = i % 2
        pltpu.make_async_copy(x_hbm.at[pl.ds(0, MOE_BLOCK)], xbuf.at[slot], sem.at[slot]).wait()
        h = jnp.dot(xbuf[slot].astype(BF16), w1_ref[...], preferred_element_type=F32) + b1_ref[0]
        glu = jnp.minimum(h[:, :D_FF], SWIGLU_LIMIT)
        lin = jnp.clip(h[:, D_FF:], -SWIGLU_LIMIT, SWIGLU_LIMIT)
        act = glu * jax.nn.sigmoid(SWIGLU_ALPHA * glu) * (lin + 1.0)
        y = jnp.dot(act.astype(BF16), w2_ref[...], preferred_element_type=F32) + b2_ref[0]
        o_ref[...] = y * g_ref[...]

    @pl.when(i >= nreal)
    def _():
        o_ref[...] = jnp.zeros_like(o_ref)


def _moe_experts(h, slot_tok, slot_gate, block_expert, n_real, w1, b1, w2, b2):
    n_slots = slot_tok.shape[0]
    n_blocks = n_slots // MOE_BLOCK
    grid_spec = pltpu.PrefetchScalarGridSpec(
        num_scalar_prefetch=3, grid=(n_blocks,),
        in_specs=[
            pl.BlockSpec(memory_space=pl.ANY),
            pl.BlockSpec((None, D_MODEL, 2 * D_FF), lambda i, be, nr, tk: (be[i], 0, 0)),
            pl.BlockSpec((1, 1, 2 * D_FF), lambda i, be, nr, tk: (be[i], 0, 0)),
            pl.BlockSpec((None, D_FF, D_MODEL), lambda i, be, nr, tk: (be[i], 0, 0)),
            pl.BlockSpec((1, 1, D_MODEL), lambda i, be, nr, tk: (be[i], 0, 0)),
            pl.BlockSpec((MOE_BLOCK, 1), lambda i, be, nr, tk: (i, 0)),
        ],
        out_specs=pl.BlockSpec((MOE_BLOCK, D_MODEL), lambda i, be, nr, tk: (i, 0)),
        scratch_shapes=[pltpu.VMEM((2, MOE_BLOCK, D_MODEL), F32), pltpu.SemaphoreType.DMA((2,))],
    )
    return pl.pallas_call(
        _moe_kernel, grid_spec=grid_spec,
        out_shape=jax.ShapeDtypeStruct((n_slots, D_MODEL), F32),
        compiler_params=_cparams(("arbitrary",)), name="moe_experts",
    )(block_expert, n_real, slot_tok, h, w1, b1.reshape(N_EXPERTS, 1, 2 * D_FF), w2,
      b2.reshape(N_EXPERTS, 1, D_MODEL), slot_gate)


COMBINE_TOKENS = 128


def _combine_kernel(dest_ref, y_hbm, res_ref, gate_ref, o_ref, buf, sem):
    i = pl.program_id(0)
    n = pl.num_programs(0)

    def issue(tile, slot):
        base = tile * (COMBINE_TOKENS * TOP_K)

        def body(t, carry):
            for k in range(TOP_K):
                _row_gather_copy(y_hbm, dest_ref[base + t * TOP_K + k], buf.at[slot, k, pl.ds(t, 1)],
                                 sem.at[slot]).start()
            return carry

        lax.fori_loop(0, COMBINE_TOKENS, body, 0, unroll=4)

    @pl.when(i == 0)
    def _():
        issue(0, 0)

    @pl.when(i + 1 < n)
    def _():
        issue(i + 1, (i + 1) % 2)

    slot = i % 2
    for k in range(TOP_K):
        pltpu.make_async_copy(y_hbm.at[pl.ds(0, COMBINE_TOKENS)], buf.at[slot, k], sem.at[slot]).wait()
    y = (buf[slot, 0] + buf[slot, 1]) + (buf[slot, 2] + buf[slot, 3])
    o_ref[...] = res_ref[...] + gate_ref[0] * y


def _moe_combine(yb, dest, res, gate):
    n_tok = dest.shape[0] // TOP_K
    tt = COMBINE_TOKENS
    grid_spec = pltpu.PrefetchScalarGridSpec(
        num_scalar_prefetch=1, grid=(n_tok // tt,),
        in_specs=[
            pl.BlockSpec(memory_space=pl.ANY),
            pl.BlockSpec((tt, D_MODEL), lambda i, d: (i, 0)),
            pl.BlockSpec((1, 1, D_MODEL), lambda i, d: (_mod_row(i, tt), 0, 0)),
        ],
        out_specs=pl.BlockSpec((tt, D_MODEL), lambda i, d: (i, 0)),
        scratch_shapes=[pltpu.VMEM((2, TOP_K, tt, D_MODEL), F32), pltpu.SemaphoreType.DMA((2,))],
    )
    return pl.pallas_call(
        _combine_kernel, grid_spec=grid_spec,
        out_shape=jax.ShapeDtypeStruct((n_tok, D_MODEL), F32),
        compiler_params=_cparams(("arbitrary",)), name="moe_combine",
    )(dest, yb, res, gate)


def _moe(h, logits, res, gate, w1, b1, w2, b2):
    n_tok = h.shape[0]
    nk = n_tok * TOP_K
    top_v, top_e = lax.top_k(logits, TOP_K)
    gates = jax.nn.softmax(top_v, axis=-1).reshape(-1)
    flat_e = top_e.reshape(-1).astype(jnp.int32)
    order = jnp.argsort(flat_e, stable=True).astype(jnp.int32)
    inv = jnp.argsort(order).astype(jnp.int32)
    experts = jnp.arange(N_EXPERTS, dtype=jnp.int32)
    sorted_e = flat_e[order]
    start = jnp.searchsorted(sorted_e, experts, side='left').astype(jnp.int32)
    counts = jnp.searchsorted(sorted_e, experts, side='right').astype(jnp.int32) - start
    padded = (counts + MOE_BLOCK - 1) // MOE_BLOCK * MOE_BLOCK
    pend = jnp.cumsum(padded)
    pstart = pend - padded
    dest = (pstart[flat_e] + inv - start[flat_e]).astype(jnp.int32)
    n_blocks = -(-(nk + N_EXPERTS * (MOE_BLOCK - 1)) // MOE_BLOCK)
    n_slots = n_blocks * MOE_BLOCK
    block_expert = jnp.minimum(
        jnp.searchsorted(pend, jnp.arange(n_blocks, dtype=jnp.int32) * MOE_BLOCK, side='right'),
        N_EXPERTS - 1).astype(jnp.int32)
    n_real = (pend[-1] // MOE_BLOCK).astype(jnp.int32).reshape(1)
    slot = jnp.arange(n_slots, dtype=jnp.int32)
    slot_e = block_expert[slot // MOE_BLOCK]
    off = slot - pstart[slot_e]
    valid = off < counts[slot_e]
    flat_of_slot = order[jnp.clip(start[slot_e] + off, 0, nk - 1)]
    slot_tok = jnp.where(valid, flat_of_slot // TOP_K, 0).astype(jnp.int32)
    slot_gate = jnp.where(valid, gates[flat_of_slot], 0.0)
    yb = _moe_experts(h, slot_tok, slot_gate.reshape(n_slots, 1), block_expert, n_real, w1, b1, w2, b2)
    return _moe_combine(yb, dest, res, gate)


def _rms(x, g):
    return x * lax.rsqrt(jnp.mean(x * x, axis=-1, keepdims=True) + NORM_EPS) * g


def kernel(x, c, ctx, c_ctx, ada_w, ada_b, norm1_g, norm2_g, na_w_qkv, na_q_g, na_k_g, na_rpb, na_w_o,
           ssd_w_in, ssd_conv_w, ssd_conv_b, ssd_dt_bias, ssd_a_log, ssd_d, ssd_norm_g, ssd_w_out,
           moe_w_router, moe_b_router, moe_w1, moe_b1, moe_w2, moe_b2):
    xa = jnp.concatenate([x.reshape(N_LAT, D_MODEL), ctx.reshape(N_CTX, D_MODEL)], axis=0)
    cond = jnp.concatenate([jax.nn.silu(c), jax.nn.silu(c_ctx)[None, :],
                            jnp.zeros((16 - BATCH - 1, D_MODEL), F32)], axis=0)
    for i in range(DEPTH):
        last = i == DEPTH - 1
        mod = _matmul(cond, ada_w[i], tm=16, tn=D_MODEL, bias=ada_b[i],
                      precision=lax.Precision.HIGHEST, name="ada_mod")
        m = [mod[:, t * D_MODEL:(t + 1) * D_MODEL] for t in range(6)]
        mt = [t.reshape(16, 1, D_MODEL) for t in m]
        h1 = _normmod(xa, norm1_g[i], mt[0], mt[1], N_ALL)
        j = i // 2
        if i % 2 == 0:
            qkv = _qkv_proj(h1, na_w_qkv[j].astype(BF16), na_q_g[j], na_k_g[j])
            o_all = _neighbourhood_attention(qkv, _na_bias_table(na_rpb[j]))
            n_out = N_LAT if last else N_ALL
            xa_new = _matmul_residual(o_all[:n_out], na_w_o[j].astype(BF16), xa, mt[2], tm=512,
                                      name="na_out_proj")
        else:
            w_in = ssd_w_in[j]
            zxbc = _matmul(h1, w_in[:, :SSD_MAIN_W].astype(BF16), tm=512, tn=1024, name="ssd_in_proj")
            dt_raw = _matmul(h1, w_in[:, SSD_MAIN_W:].astype(BF16), tm=512, tn=2 * SSD_HEADS,
                             name="ssd_dt_proj")
            z = zxbc[:, :SSD_D_INNER]
            xbc = zxbc[:, SSD_D_INNER:]
            xbc = jnp.concatenate([
                _depthwise_conv(xbc[:N_LAT], ssd_conv_w[j], ssd_conv_b[j], SEQ),
                _depthwise_conv(xbc[N_LAT:], ssd_conv_w[j], ssd_conv_b[j], CTX_LEN)], axis=0)
            xbc = jax.nn.silu(xbc)
            xs = xbc[:, :SSD_D_INNER]
            bm = xbc[:, SSD_D_INNER:SSD_D_INNER + SSD_GROUPS * SSD_STATE]
            cm = xbc[:, SSD_D_INNER + SSD_GROUPS * SSD_STATE:]
            dt = jax.nn.softplus(dt_raw.reshape(N_ALL, 2, SSD_HEADS) + ssd_dt_bias[j][None])
            dt = jnp.moveaxis(dt, 1, 0)
            a = -jnp.exp(ssd_a_log[j])
            da = (dt * a[:, None, :]).reshape(2, N_ALL // SSD_CHUNK, SSD_CHUNK, SSD_HEADS)
            cum_f = jnp.cumsum(da[0], axis=1)
            cum_b = jnp.flip(jnp.cumsum(jnp.flip(da[1], axis=1), axis=1), axis=1)
            cum = jnp.stack([cum_f, cum_b], axis=0)
            cumt = jnp.swapaxes(cum, 2, 3)
            xd = (xs.reshape(1, N_ALL, SSD_HEADS, SSD_HEAD_DIM) * dt[..., None]).reshape(2, N_ALL, SSD_D_INNER)
            y2 = _ssd_scan(xd.astype(BF16), bm.astype(BF16), cm.astype(BF16),
                           cum.reshape(2, N_ALL, SSD_HEADS), cumt)
            n_out = N_LAT if last else N_ALL
            y = (y2[0, :n_out] + y2[1, :n_out]
                 + (ssd_d[j][None, :, None] * xs[:n_out].reshape(n_out, SSD_HEADS, SSD_HEAD_DIM)
                    ).reshape(n_out, SSD_D_INNER))
            gated = (y * jax.nn.silu(z[:n_out])).reshape(n_out, SSD_GROUPS, SSD_D_INNER // SSD_GROUPS)
            gated = _rms(gated, ssd_norm_g[j].reshape(SSD_GROUPS, -1)).reshape(n_out, SSD_D_INNER)
            xa_new = _matmul_residual(gated.astype(BF16), ssd_w_out[j].astype(BF16), xa, mt[2], tm=512,
                                      name="ssd_out_proj")
        n_tok = xa_new.shape[0]
        h2, logits = _normmod(xa_new, norm2_g[i], mt[3], mt[4], n_tok,
                              router=(moe_w_router[i], moe_b_router[i]))
        xa = _moe(h2, logits, xa_new, mt[5], moe_w1[i].astype(BF16), moe_b1[i], moe_w2[i].astype(BF16),
                  moe_b2[i])
    return xa[:N_LAT].reshape(BATCH, SEQ, D_MODEL)
```

```python
import functools

import jax
import jax.numpy as jnp
from jax import lax
from jax.experimental import pallas as pl
from jax.experimental.pallas import tpu as pltpu

F32 = jnp.float32
BF16 = jnp.bfloat16

D_MODEL = 1024
BATCH = 8
SEQ = 2048
DEPTH = 2
GRID_W = 64
CTX_LEN = 256
N_LAT = BATCH * SEQ
N_CTX = BATCH * CTX_LEN
N_ALL = N_LAT + N_CTX

NA_HEADS = 16
NA_HEAD_DIM = 64
NA_KR = 8
NA_KC = 16
NA_ROWS = SEQ // GRID_W
NA_QROWS = 4
NA_BAND = 12
NA_QBLK = NA_QROWS * GRID_W
NA_KBLK = NA_BAND * GRID_W

SSD_D_INNER = 2048
SSD_HEAD_DIM = 64
SSD_HEADS = 32
SSD_GROUPS = 8
SSD_HPG = 4
SSD_STATE = 128
SSD_CONV_W = 5
SSD_CHUNK = 128
SSD_CONV_DIM = SSD_D_INNER + 2 * SSD_GROUPS * SSD_STATE
SSD_MAIN_W = SSD_D_INNER + SSD_CONV_DIM
SSD_GROUP_W = SSD_HPG * SSD_HEAD_DIM

N_EXPERTS = 32
TOP_K = 4
D_FF = 1024
SWIGLU_ALPHA = 1.702
SWIGLU_LIMIT = 7.0
MOE_BLOCK = 256
NORM_EPS = 1e-6
NEG_BIG = -1e30

VMEM_LIMIT = 56 * 1024 * 1024

_NT = (((1,), (1,)), ((), ()))
_TN = (((0,), (0,)), ((), ()))


def _cparams(sem):
    return pltpu.CompilerParams(dimension_semantics=sem, vmem_limit_bytes=VMEM_LIMIT)


def _mod_row(i, tm):
    return jnp.minimum((i * tm) // SEQ, BATCH)


def _split_bf16(x, pieces):
    out = []
    rem = x
    for _ in range(pieces):
        p = rem.astype(BF16)
        out.append(p)
        rem = rem - p.astype(F32)
    return out


def _dot_exact_rhs(x, sel, pieces, dims=None):
    acc = None
    for p in _split_bf16(x, pieces):
        if dims is None:
            t = jnp.dot(p, sel, preferred_element_type=F32)
        else:
            t = lax.dot_general(p, sel, dims, preferred_element_type=F32)
        acc = t if acc is None else acc + t
    return acc


def _dot_exact_lhs(sel, x, pieces):
    acc = None
    for p in _split_bf16(x, pieces):
        t = jnp.dot(sel, p, preferred_element_type=F32)
        acc = t if acc is None else acc + t
    return acc


def _normmod_kernel(x_ref, g_ref, sh_ref, sc_ref, o_ref):
    x = x_ref[...]
    ms = jnp.mean(x * x, axis=-1, keepdims=True)
    y = x * lax.rsqrt(ms + NORM_EPS) * g_ref[...]
    o_ref[...] = (y * (1.0 + sc_ref[0]) + sh_ref[0]).astype(o_ref.dtype)


ROUTE_EXPERT_LANE = 0
ROUTE_RANK_LANE = 4
ROUTE_GATE_LANE = 8


def _normmod_router_kernel(x_ref, g_ref, sh_ref, sc_ref, wr_ref, br_ref, o_ref, route_ref, count_ref):
    i = pl.program_id(0)
    x = x_ref[...]
    tm = x.shape[0]
    ms = jnp.mean(x * x, axis=-1, keepdims=True)
    y = x * lax.rsqrt(ms + NORM_EPS) * g_ref[...]
    h = y * (1.0 + sc_ref[0]) + sh_ref[0]
    o_ref[...] = h.astype(o_ref.dtype)
    logits = jnp.dot(h, wr_ref[...], preferred_element_type=F32,
                     precision=lax.Precision.HIGHEST) + br_ref[...]

    @pl.when(i == 0)
    def _():
        count_ref[...] = jnp.zeros_like(count_ref)

    lane_e = lax.broadcasted_iota(jnp.int32, logits.shape, 1).astype(F32)
    cur = logits
    vals, ids, hots = [], [], []
    for _ in range(TOP_K):
        m = jnp.max(cur, axis=-1, keepdims=True)
        idx = jnp.min(jnp.where(cur == m, lane_e, float(N_EXPERTS)), axis=-1, keepdims=True)
        hot = lane_e == idx
        vals.append(m)
        ids.append(idx)
        hots.append(hot)
        cur = jnp.where(hot, -jnp.inf, cur)
    exps = [jnp.exp(v - vals[0]) for v in vals]
    den = exps[0] + exps[1] + exps[2] + exps[3]
    picked = jnp.where(hots[0] | hots[1] | hots[2] | hots[3], 1.0, 0.0)
    row = lax.broadcasted_iota(jnp.int32, (tm, tm), 0)
    col = lax.broadcasted_iota(jnp.int32, (tm, tm), 1)
    earlier = jnp.where(col < row, 1.0, 0.0).astype(BF16)
    before = count_ref[...] + jnp.dot(earlier, picked.astype(BF16), preferred_element_type=F32)
    lane = lax.broadcasted_iota(jnp.int32, (tm, 128), 1)
    route = jnp.zeros((tm, 128), F32)
    for k in range(TOP_K):
        rank = jnp.sum(jnp.where(hots[k], before, 0.0), axis=-1, keepdims=True)
        route = jnp.where(lane == ROUTE_EXPERT_LANE + k, ids[k], route)
        route = jnp.where(lane == ROUTE_RANK_LANE + k, rank, route)
        route = jnp.where(lane == ROUTE_GATE_LANE + k, exps[k] / den, route)
    route_ref[...] = route
    count_ref[...] += jnp.sum(picked, axis=0, keepdims=True)


def _normmod(x, g, shift, scale, n_rows, router=None, tm=512):
    grid = (n_rows // tm,)
    x_spec = pl.BlockSpec((tm, D_MODEL), lambda i: (i, 0))
    g_spec = pl.BlockSpec((1, D_MODEL), lambda i: (0, 0))
    m_spec = pl.BlockSpec((1, 1, D_MODEL), lambda i: (_mod_row(i, tm), 0, 0))
    o_spec = pl.BlockSpec((tm, D_MODEL), lambda i: (i, 0))
    g2 = g.reshape(1, D_MODEL)
    if router is None:
        return pl.pallas_call(
            _normmod_kernel, grid=grid,
            in_specs=[x_spec, g_spec, m_spec, m_spec], out_specs=o_spec,
            out_shape=jax.ShapeDtypeStruct((n_rows, D_MODEL), BF16),
            compiler_params=_cparams(("arbitrary",)), name="normmod",
        )(x, g2, shift, scale)
    w_r, b_r = router
    return pl.pallas_call(
        _normmod_router_kernel, grid=grid,
        in_specs=[x_spec, g_spec, m_spec, m_spec,
                  pl.BlockSpec((D_MODEL, N_EXPERTS), lambda i: (0, 0)),
                  pl.BlockSpec((1, N_EXPERTS), lambda i: (0, 0))],
        out_specs=[o_spec, pl.BlockSpec((tm, 128), lambda i: (i, 0)),
                   pl.BlockSpec((1, N_EXPERTS), lambda i: (0, 0))],
        out_shape=[jax.ShapeDtypeStruct((n_rows, D_MODEL), F32),
                   jax.ShapeDtypeStruct((n_rows, 128), F32),
                   jax.ShapeDtypeStruct((1, N_EXPERTS), F32)],
        compiler_params=_cparams(("arbitrary",)), name="normmod_router",
    )(x, g2, shift, scale, w_r, b_r.reshape(1, N_EXPERTS))


def _mm_kernel(a_ref, w_ref, o_ref, *, precision):
    o_ref[...] = jnp.dot(a_ref[...], w_ref[...], preferred_element_type=F32,
                         precision=precision).astype(o_ref.dtype)


def _mm_bias_kernel(a_ref, w_ref, b_ref, o_ref, *, precision):
    o_ref[...] = (jnp.dot(a_ref[...], w_ref[...], preferred_element_type=F32,
                          precision=precision) + b_ref[...]).astype(o_ref.dtype)


def _mm_res_kernel(a_ref, w_ref, res_ref, gate_ref, o_ref):
    acc = jnp.dot(a_ref[...], w_ref[...], preferred_element_type=F32)
    o_ref[...] = res_ref[...] + gate_ref[0] * acc


def _matmul(a, w, *, tm, tn, out_dtype=F32, bias=None, precision=None, name="mm"):
    m, k = a.shape
    n = w.shape[1]
    grid = (n // tn, m // tm)
    in_specs = [pl.BlockSpec((tm, k), lambda j, i: (i, 0)),
                pl.BlockSpec((k, tn), lambda j, i: (0, j))]
    args = [a, w]
    if bias is None:
        body = functools.partial(_mm_kernel, precision=precision)
    else:
        body = functools.partial(_mm_bias_kernel, precision=precision)
        in_specs.append(pl.BlockSpec((1, tn), lambda j, i: (0, j)))
        args.append(bias.reshape(1, n))
    return pl.pallas_call(
        body, grid=grid, in_specs=in_specs,
        out_specs=pl.BlockSpec((tm, tn), lambda j, i: (i, j)),
        out_shape=jax.ShapeDtypeStruct((m, n), out_dtype),
        compiler_params=_cparams(("arbitrary", "arbitrary")), name=name,
    )(*args)


def _matmul_residual(a, w, res, gate, *, tm, name):
    m, k = a.shape
    n = w.shape[1]
    return pl.pallas_call(
        _mm_res_kernel, grid=(m // tm,),
        in_specs=[pl.BlockSpec((tm, k), lambda i: (i, 0)),
                  pl.BlockSpec((k, n), lambda i: (0, 0)),
                  pl.BlockSpec((tm, n), lambda i: (i, 0)),
                  pl.BlockSpec((1, 1, n), lambda i: (_mod_row(i, tm), 0, 0))],
        out_specs=pl.BlockSpec((tm, n), lambda i: (i, 0)),
        out_shape=jax.ShapeDtypeStruct((m, n), F32),
        compiler_params=_cparams(("arbitrary",)), name=name,
    )(a, w, res, gate)


def _qkv_kernel(a_ref, w_ref, gain_ref, seg_ref, segt_ref, o_ref):
    j = pl.program_id(0)
    acc = jnp.dot(a_ref[...], w_ref[...], preferred_element_type=F32)

    @pl.when(j < 2)
    def _():
        ss = _dot_exact_rhs(acc * acc, seg_ref[...], 2)
        inv = lax.rsqrt(ss * (1.0 / NA_HEAD_DIM) + NORM_EPS)
        o_ref[...] = (acc * _dot_exact_rhs(inv, segt_ref[...], 2) * gain_ref[0]).astype(o_ref.dtype)

    @pl.when(j == 2)
    def _():
        o_ref[...] = acc.astype(o_ref.dtype)


def _qkv_proj(h, w_qkv, q_g, k_g, tm=512):
    m = h.shape[0]
    head_of_col = jnp.arange(D_MODEL) // NA_HEAD_DIM
    seg = (head_of_col[:, None] == jnp.arange(128)[None, :]).astype(BF16)
    gain = jnp.stack([jnp.tile(q_g, NA_HEADS) * NA_HEAD_DIM ** -0.5, jnp.tile(k_g, NA_HEADS),
                      jnp.ones((D_MODEL,), F32)]).reshape(3, 1, D_MODEL)
    return pl.pallas_call(
        _qkv_kernel, grid=(3, m // tm),
        in_specs=[pl.BlockSpec((tm, D_MODEL), lambda j, i: (i, 0)),
                  pl.BlockSpec((D_MODEL, D_MODEL), lambda j, i: (0, j)),
                  pl.BlockSpec((1, 1, D_MODEL), lambda j, i: (j, 0, 0)),
                  pl.BlockSpec((D_MODEL, 128), lambda j, i: (0, 0)),
                  pl.BlockSpec((128, D_MODEL), lambda j, i: (0, 0))],
        out_specs=pl.BlockSpec((tm, D_MODEL), lambda j, i: (i, j)),
        out_shape=jax.ShapeDtypeStruct((m, 3 * D_MODEL), BF16),
        compiler_params=_cparams(("arbitrary", "arbitrary")), name="qkv_proj",
    )(h, w_qkv, gain, seg, seg.T)


def _na_bias_table(rpb):
    col = jnp.arange(GRID_W)
    col_start = jnp.clip(col - NA_KC // 2, 0, GRID_W - NA_KC)
    col_ok = (col[None, :] >= col_start[:, None]) & (col[None, :] < col_start[:, None] + NA_KC)
    col_idx = jnp.clip(col[None, :] - col[:, None] + NA_KC - 1, 0, 2 * NA_KC - 2)
    tables = []
    for rb in (0, 1, NA_ROWS // NA_QROWS - 1):
        r0 = rb * NA_QROWS
        ks = min(max(r0 - NA_KR // 2, 0), NA_ROWS - NA_BAND)
        per_row = []
        for i in range(NA_QROWS):
            r = r0 + i
            rs = min(max(r - NA_KR // 2, 0), NA_ROWS - NA_KR)
            per_n = []
            for n in range(NA_BAND):
                kr = ks + n
                if rs <= kr < rs + NA_KR:
                    dr = kr - r + NA_KR - 1
                    t = rpb[:, dr][:, col_idx]
                    t = jnp.where(col_ok[None], t, NEG_BIG)
                else:
                    t = jnp.full((NA_HEADS, GRID_W, GRID_W), NEG_BIG, F32)
                per_n.append(t)
            per_row.append(jnp.concatenate(per_n, axis=-1))
        tables.append(jnp.concatenate(per_row, axis=1))
    return jnp.stack(tables, axis=0).astype(F32)


def _softmax_pv(s_list, v_list):
    m = s_list[0].max(axis=-1, keepdims=True)
    for s in s_list[1:]:
        m = jnp.maximum(m, s.max(axis=-1, keepdims=True))
    acc = None
    den = None
    for s, v in zip(s_list, v_list):
        p = jnp.exp(s - m)
        l = p.sum(axis=-1, keepdims=True)
        o = jnp.dot(p.astype(BF16), v, preferred_element_type=F32)
        acc = o if acc is None else acc + o
        den = l if den is None else den + l
    return acc / den


NA_LAT_STEPS = NA_ROWS // NA_QROWS


def _na_kernel(q_ref, k_ref, v_ref, kc_ref, vc_ref, tbl_ref, o_ref):
    rb = pl.program_id(2)
    q2 = q_ref[...]
    kc = kc_ref[...]
    vc = vc_ref[...]
    lane = lax.broadcasted_iota(jnp.int32, q2.shape, 1)

    def head_queries(sub):
        sel = (lane < NA_HEAD_DIM) if sub == 0 else (lane >= NA_HEAD_DIM)
        return jnp.where(sel, q2, jnp.zeros_like(q2))

    @pl.when(rb < NA_LAT_STEPS)
    def _():
        ks = jnp.clip(rb * NA_QROWS - NA_KR // 2, 0, NA_ROWS - NA_BAND)
        kstart = pl.multiple_of(ks * GRID_W, GRID_W)
        kb = k_ref[pl.ds(kstart, NA_KBLK), :]
        vb = v_ref[pl.ds(kstart, NA_KBLK), :]
        outs = []
        for sub in range(2):
            qm = head_queries(sub)
            s_lat = lax.dot_general(qm, kb, _NT, preferred_element_type=F32) + tbl_ref[0, sub]
            s_ctx = lax.dot_general(qm, kc, _NT, preferred_element_type=F32)
            outs.append(_softmax_pv([s_lat, s_ctx], [vb, vc]))
        o_ref[...] = jnp.where(lane < NA_HEAD_DIM, outs[0], outs[1]).astype(o_ref.dtype)

    @pl.when(rb == NA_LAT_STEPS)
    def _():
        outs = []
        for sub in range(2):
            s_ctx = lax.dot_general(head_queries(sub), kc, _NT, preferred_element_type=F32)
            outs.append(_softmax_pv([s_ctx], [vc]))
        o_ref[...] = jnp.where(lane < NA_HEAD_DIM, outs[0], outs[1]).astype(o_ref.dtype)


def _na_block_type(rb):
    return jnp.where(rb == 0, 0, jnp.where(rb >= NA_LAT_STEPS - 1, 2, 1))


def _neighbourhood_attention(qkv, table):
    hp = NA_HEADS // 2
    ctx_blk0 = N_LAT // CTX_LEN

    def q_block(b, h, r):
        return (jnp.where(r < NA_LAT_STEPS, b * NA_LAT_STEPS + r, ctx_blk0 + b), h)

    return pl.pallas_call(
        _na_kernel, grid=(BATCH, hp, NA_LAT_STEPS + 1),
        in_specs=[
            pl.BlockSpec((NA_QBLK, 128), q_block),
            pl.BlockSpec((SEQ, 128), lambda b, h, r: (b, hp + h)),
            pl.BlockSpec((SEQ, 128), lambda b, h, r: (b, 2 * hp + h)),
            pl.BlockSpec((CTX_LEN, 128), lambda b, h, r: (ctx_blk0 + b, hp + h)),
            pl.BlockSpec((CTX_LEN, 128), lambda b, h, r: (ctx_blk0 + b, 2 * hp + h)),
            pl.BlockSpec((1, 2, NA_QBLK, NA_KBLK), lambda b, h, r: (_na_block_type(r), h, 0, 0)),
        ],
        out_specs=pl.BlockSpec((NA_QBLK, 128), q_block),
        out_shape=jax.ShapeDtypeStruct((N_ALL, D_MODEL), BF16),
        compiler_params=_cparams(("arbitrary", "arbitrary", "arbitrary")), name="na_attention",
    )(qkv, qkv, qkv, qkv, qkv, table)


CONV_ROWS = 256
CONV_COLS = 512
CONV_HALO = 8


def _conv_silu_kernel(cur_ref, prev_ref, next_ref, w_ref, b_ref, o_ref):
    i = pl.program_id(0)
    tiles_per_seq = SEQ // CONV_ROWS
    is_ctx = i >= N_LAT // CONV_ROWS
    first = jnp.logical_or(is_ctx, i % tiles_per_seq == 0)
    last = jnp.logical_or(is_ctx, i % tiles_per_seq == tiles_per_seq - 1)
    prev = jnp.where(first, 0.0, prev_ref[...])
    nxt = jnp.where(last, 0.0, next_ref[...])
    ext = jnp.concatenate([prev, cur_ref[...], nxt], axis=0)
    acc = jnp.zeros(o_ref.shape, F32) + b_ref[...]
    pad = SSD_CONV_W // 2
    for t in range(SSD_CONV_W):
        lo = CONV_HALO - pad + t
        acc = acc + ext[lo:lo + CONV_ROWS, :] * w_ref[t:t + 1, :]
    o_ref[...] = acc * jax.nn.sigmoid(acc)


def _conv_silu(zxbc, conv_w, conv_b):
    col0 = SSD_D_INNER // CONV_COLS
    n_halo_blocks = N_ALL // CONV_HALO
    per_tile = CONV_ROWS // CONV_HALO
    return pl.pallas_call(
        _conv_silu_kernel, grid=(N_ALL // CONV_ROWS, SSD_CONV_DIM // CONV_COLS),
        in_specs=[
            pl.BlockSpec((CONV_ROWS, CONV_COLS), lambda i, c: (i, col0 + c)),
            pl.BlockSpec((CONV_HALO, CONV_COLS), lambda i, c: (jnp.maximum(i * per_tile - 1, 0), col0 + c)),
            pl.BlockSpec((CONV_HALO, CONV_COLS),
                         lambda i, c: (jnp.minimum((i + 1) * per_tile, n_halo_blocks - 1), col0 + c)),
            pl.BlockSpec((SSD_CONV_W, CONV_COLS), lambda i, c: (0, c)),
            pl.BlockSpec((1, CONV_COLS), lambda i, c: (0, c)),
        ],
        out_specs=pl.BlockSpec((CONV_ROWS, CONV_COLS), lambda i, c: (i, c)),
        out_shape=jax.ShapeDtypeStruct((N_ALL, SSD_CONV_DIM), F32),
        compiler_params=_cparams(("arbitrary", "arbitrary")), name="ssd_conv_silu",
    )(zxbc, zxbc, zxbc, conv_w, conv_b.reshape(1, SSD_CONV_DIM))


SSD_STEPS = (CTX_LEN + SEQ) // SSD_CHUNK
SSD_CTX_CHUNKS = CTX_LEN // SSD_CHUNK
SSD_LAT_CHUNKS = SEQ // SSD_CHUNK


def _ssd_chunk(b, d, j):
    jj = jnp.where(d == 0, j, jnp.where(j < SSD_CTX_CHUNKS, SSD_CTX_CHUNKS - 1 - j,
                                        SSD_STEPS + SSD_CTX_CHUNKS - 1 - j))
    return jnp.where(jj < SSD_CTX_CHUNKS,
                     N_LAT // SSD_CHUNK + SSD_CTX_CHUNKS * b + jj,
                     SSD_LAT_CHUNKS * b + jj - SSD_CTX_CHUNKS)


def _ssd_kernel(xs_ref, b_ref, c_ref, dtr_ref, bias_ref, alog_ref, expand_ref, y_ref, state_ref):
    d = pl.program_id(1)
    j = pl.program_id(2)

    @pl.when(j == 0)
    def _():
        state_ref[...] = jnp.zeros_like(state_ref)

    fwd = d == 0
    dtr = dtr_ref[...]
    dt_pre = jnp.where(fwd, dtr[:, :SSD_HEADS], dtr[:, SSD_HEADS:]) + bias_ref[...]
    dt = jnp.maximum(dt_pre, 0.0) + jnp.log1p(jnp.exp(-jnp.abs(dt_pre)))
    da = dt * (-jnp.exp(alog_ref[...]))
    li = lax.broadcasted_iota(jnp.int32, (SSD_CHUNK, SSD_CHUNK), 0)
    si = lax.broadcasted_iota(jnp.int32, (SSD_CHUNK, SSD_CHUNK), 1)
    before = jnp.where(fwd, li - si, si - li) >= 0
    before_t = jnp.where(fwd, si - li, li - si) >= 0
    cum = _dot_exact_lhs(jnp.where(before, 1.0, 0.0).astype(BF16), da, 3)
    cum_t = _dot_exact_rhs(da, jnp.where(before_t, 1.0, 0.0).astype(BF16), 3, _TN)
    dt_t = _dot_exact_rhs(dt, jnp.where(li == si, 1.0, 0.0).astype(BF16), 2, _TN)
    tot = jnp.where(fwd, cum[SSD_CHUNK - 1:SSD_CHUNK, :], cum[0:1, :])
    expand = expand_ref[...]
    to_end = _dot_exact_rhs(dt * jnp.exp(tot - cum), expand, 2)
    from_start = _dot_exact_rhs(jnp.exp(cum), expand, 2)
    carry = jnp.where(fwd, from_start[SSD_CHUNK - 1:SSD_CHUNK, :], from_start[0:1, :])
    xs = xs_ref[...]
    x16 = xs.astype(BF16)
    xdec = (xs * to_end).astype(BF16)
    lane = lax.broadcasted_iota(jnp.int32, (SSD_CHUNK, 2 * SSD_HEAD_DIM), 1)
    for g in range(SSD_GROUPS):
        gs = slice(g * SSD_GROUP_W, (g + 1) * SSD_GROUP_W)
        bg = b_ref[:, g * SSD_STATE:(g + 1) * SSD_STATE].astype(BF16)
        cg = c_ref[:, g * SSD_STATE:(g + 1) * SSD_STATE].astype(BF16)
        cb = lax.dot_general(cg, bg, _NT, preferred_element_type=F32)
        st = state_ref[g]
        y_off = jnp.dot(cg, st.astype(BF16), preferred_element_type=F32) * from_start[:, gs]
        y_pairs = []
        for pr in range(SSD_HPG // 2):
            ms = []
            for r2 in range(2):
                h = g * SSD_HPG + pr * 2 + r2
                decay = jnp.exp(jnp.where(before, cum[:, h:h + 1] - cum_t[h:h + 1, :], NEG_BIG))
                ms.append((cb * decay * dt_t[h:h + 1, :]).astype(BF16))
            m2 = jnp.concatenate(ms, axis=1)
            c0 = g * SSD_GROUP_W + pr * 2 * SSD_HEAD_DIM
            slab = x16[:, c0:c0 + 2 * SSD_HEAD_DIM]
            zero = jnp.zeros_like(slab)
            xdiag = jnp.concatenate([jnp.where(lane < SSD_HEAD_DIM, slab, zero),
                                     jnp.where(lane >= SSD_HEAD_DIM, slab, zero)], axis=0)
            y_pairs.append(jnp.dot(m2, xdiag, preferred_element_type=F32))
        y_ref[:, gs] = jnp.concatenate(y_pairs, axis=1) + y_off
        s_new = lax.dot_general(bg, xdec[:, gs], _TN, preferred_element_type=F32)
        state_ref[g] = st * carry[:, gs] + s_new


def _ssd_scan(xbc, dt_raw, dt_bias, a_log):
    expand = (jnp.arange(SSD_HEADS)[:, None] == (jnp.arange(SSD_D_INNER) // SSD_HEAD_DIM)[None, :]).astype(BF16)
    bc_w = SSD_GROUPS * SSD_STATE
    b_col = SSD_D_INNER // bc_w
    return pl.pallas_call(
        _ssd_kernel, grid=(BATCH, 2, SSD_STEPS),
        in_specs=[
            pl.BlockSpec((SSD_CHUNK, SSD_D_INNER), lambda b, d, j: (_ssd_chunk(b, d, j), 0)),
            pl.BlockSpec((SSD_CHUNK, bc_w), lambda b, d, j: (_ssd_chunk(b, d, j), b_col)),
            pl.BlockSpec((SSD_CHUNK, bc_w), lambda b, d, j: (_ssd_chunk(b, d, j), b_col + 1)),
            pl.BlockSpec((SSD_CHUNK, 2 * SSD_HEADS), lambda b, d, j: (_ssd_chunk(b, d, j), 0)),
            pl.BlockSpec((None, 1, SSD_HEADS), lambda b, d, j: (d, 0, 0)),
            pl.BlockSpec((None, 1, SSD_HEADS), lambda b, d, j: (d, 0, 0)),
            pl.BlockSpec((SSD_HEADS, SSD_D_INNER), lambda b, d, j: (0, 0)),
        ],
        out_specs=pl.BlockSpec((None, SSD_CHUNK, SSD_D_INNER), lambda b, d, j: (d, _ssd_chunk(b, d, j), 0)),
        out_shape=jax.ShapeDtypeStruct((2, N_ALL, SSD_D_INNER), F32),
        scratch_shapes=[pltpu.VMEM((SSD_GROUPS, SSD_STATE, SSD_GROUP_W), F32)],
        compiler_params=_cparams(("arbitrary", "arbitrary", "arbitrary")), name="ssd_scan",
    )(xbc, xbc, xbc, dt_raw, dt_bias.reshape(2, 1, SSD_HEADS), a_log.reshape(2, 1, SSD_HEADS), expand)


def _ssd_gate_norm_kernel(yf_ref, yb_ref, xs_ref, z_ref, dskip_ref, g_ref, o_ref):
    z = z_ref[...]
    y = (yf_ref[...] + yb_ref[...] + dskip_ref[...] * xs_ref[...]) * (z * jax.nn.sigmoid(z))
    for grp in range(SSD_GROUPS):
        gs = slice(grp * SSD_GROUP_W, (grp + 1) * SSD_GROUP_W)
        yg = y[:, gs]
        ms = jnp.mean(yg * yg, axis=-1, keepdims=True)
        o_ref[:, gs] = (yg * lax.rsqrt(ms + NORM_EPS) * g_ref[:, gs]).astype(o_ref.dtype)


def _ssd_gate_norm(y2, xbc, zxbc, d_skip, norm_g, n_out, tm=256):
    d_cols = jnp.repeat(d_skip, SSD_HEAD_DIM).reshape(1, SSD_D_INNER)
    return pl.pallas_call(
        _ssd_gate_norm_kernel, grid=(n_out // tm,),
        in_specs=[
            pl.BlockSpec((None, tm, SSD_D_INNER), lambda i: (0, i, 0)),
            pl.BlockSpec((None, tm, SSD_D_INNER), lambda i: (1, i, 0)),
            pl.BlockSpec((tm, SSD_D_INNER), lambda i: (i, 0)),
            pl.BlockSpec((tm, SSD_D_INNER), lambda i: (i, 0)),
            pl.BlockSpec((1, SSD_D_INNER), lambda i: (0, 0)),
            pl.BlockSpec((1, SSD_D_INNER), lambda i: (0, 0)),
        ],
        out_specs=pl.BlockSpec((tm, SSD_D_INNER), lambda i: (i, 0)),
        out_shape=jax.ShapeDtypeStruct((n_out, SSD_D_INNER), BF16),
        compiler_params=_cparams(("arbitrary",)), name="ssd_gate_norm",
    )(y2, y2, xbc, zxbc, d_cols, norm_g.reshape(1, SSD_D_INNER))


DISPATCH_TOKENS = 256
COMBINE_TOKENS = 128


def _dispatch_kernel(dest_ref, h_ref, xb_init_ref, xb_ref, sem):
    del xb_init_ref
    i = pl.program_id(0)
    base = i * (DISPATCH_TOKENS * TOP_K)

    def body(t, carry):
        for k in range(TOP_K):
            pltpu.make_async_copy(h_ref.at[pl.ds(t, 1)], xb_ref.at[pl.ds(dest_ref[base + t * TOP_K + k], 1)],
                                  sem.at[0]).start()
        return carry

    lax.fori_loop(0, DISPATCH_TOKENS, body, 0, unroll=4)
    for k in range(TOP_K):
        pltpu.make_async_copy(h_ref, xb_ref.at[pl.ds(0, DISPATCH_TOKENS)], sem.at[0]).wait()


def _moe_dispatch(h, dest, n_slots):
    n_tok = h.shape[0]
    tt = DISPATCH_TOKENS
    grid_spec = pltpu.PrefetchScalarGridSpec(
        num_scalar_prefetch=1, grid=(n_tok // tt,),
        in_specs=[pl.BlockSpec((tt, D_MODEL), lambda i, d: (i, 0)),
                  pl.BlockSpec(memory_space=pl.ANY)],
        out_specs=pl.BlockSpec(memory_space=pl.ANY),
        scratch_shapes=[pltpu.SemaphoreType.DMA((1,))],
    )
    return pl.pallas_call(
        _dispatch_kernel, grid_spec=grid_spec,
        out_shape=jax.ShapeDtypeStruct((n_slots, D_MODEL), F32),
        input_output_aliases={2: 0},
        compiler_params=_cparams(("arbitrary",)), name="moe_dispatch",
    )(dest, h, jnp.zeros((n_slots, D_MODEL), F32))


def _moe_kernel(be_ref, nreal_ref, x_ref, w1_ref, b1_ref, w2_ref, b2_ref, o_ref):
    i = pl.program_id(0)

    @pl.when(i < nreal_ref[0])
    def _():
        h = jnp.dot(x_ref[...].astype(BF16), w1_ref[...], preferred_element_type=F32) + b1_ref[0]
        glu = jnp.minimum(h[:, :D_FF], SWIGLU_LIMIT)
        lin = jnp.clip(h[:, D_FF:], -SWIGLU_LIMIT, SWIGLU_LIMIT)
        act = glu * jax.nn.sigmoid(SWIGLU_ALPHA * glu) * (lin + 1.0)
        o_ref[...] = jnp.dot(act.astype(BF16), w2_ref[...], preferred_element_type=F32) + b2_ref[0]

    @pl.when(i >= nreal_ref[0])
    def _():
        o_ref[...] = jnp.zeros_like(o_ref)


def _moe_experts(xb, block_expert, n_real, w1, b1, w2, b2):
    n_slots = xb.shape[0]
    n_blocks = n_slots // MOE_BLOCK
    grid_spec = pltpu.PrefetchScalarGridSpec(
        num_scalar_prefetch=2, grid=(n_blocks,),
        in_specs=[
            pl.BlockSpec((MOE_BLOCK, D_MODEL), lambda i, be, nr: (i, 0)),
            pl.BlockSpec((None, D_MODEL, 2 * D_FF), lambda i, be, nr: (be[i], 0, 0)),
            pl.BlockSpec((1, 1, 2 * D_FF), lambda i, be, nr: (be[i], 0, 0)),
            pl.BlockSpec((None, D_FF, D_MODEL), lambda i, be, nr: (be[i], 0, 0)),
            pl.BlockSpec((1, 1, D_MODEL), lambda i, be, nr: (be[i], 0, 0)),
        ],
        out_specs=pl.BlockSpec((MOE_BLOCK, D_MODEL), lambda i, be, nr: (i, 0)),
    )
    return pl.pallas_call(
        _moe_kernel, grid_spec=grid_spec,
        out_shape=jax.ShapeDtypeStruct((n_slots, D_MODEL), F32),
        compiler_params=_cparams(("arbitrary",)), name="moe_experts",
    )(block_expert, n_real, xb, w1, b1.reshape(N_EXPERTS, 1, 2 * D_FF), w2, b2.reshape(N_EXPERTS, 1, D_MODEL))


def _combine_kernel(dest_ref, y_hbm, route_ref, res_ref, gate_ref, o_ref, buf, sem):
    i = pl.program_id(0)
    n = pl.num_programs(0)

    def issue(tile, slot):
        base = tile * (COMBINE_TOKENS * TOP_K)

        def body(t, carry):
            for k in range(TOP_K):
                pltpu.make_async_copy(y_hbm.at[pl.ds(dest_ref[base + t * TOP_K + k], 1)],
                                      buf.at[slot, k, pl.ds(t, 1)], sem.at[slot]).start()
            return carry

        lax.fori_loop(0, COMBINE_TOKENS, body, 0, unroll=4)

    @pl.when(i == 0)
    def _():
        issue(0, 0)

    @pl.when(i + 1 < n)
    def _():
        issue(i + 1, (i + 1) % 2)

    slot = i % 2
    for k in range(TOP_K):
        pltpu.make_async_copy(y_hbm.at[pl.ds(0, COMBINE_TOKENS)], buf.at[slot, k], sem.at[slot]).wait()
    route = route_ref[...]
    y = None
    for k in range(TOP_K):
        t = route[:, ROUTE_GATE_LANE + k:ROUTE_GATE_LANE + k + 1] * buf[slot, k]
        y = t if y is None else y + t
    o_ref[...] = res_ref[...] + gate_ref[0] * y


def _moe_combine(yb, dest, route, res, gate):
    n_tok = route.shape[0]
    tt = COMBINE_TOKENS
    grid_spec = pltpu.PrefetchScalarGridSpec(
        num_scalar_prefetch=1, grid=(n_tok // tt,),
        in_specs=[
            pl.BlockSpec(memory_space=pl.ANY),
            pl.BlockSpec((tt, 128), lambda i, d: (i, 0)),
            pl.BlockSpec((tt, D_MODEL), lambda i, d: (i, 0)),
            pl.BlockSpec((1, 1, D_MODEL), lambda i, d: (_mod_row(i, tt), 0, 0)),
        ],
        out_specs=pl.BlockSpec((tt, D_MODEL), lambda i, d: (i, 0)),
        scratch_shapes=[pltpu.VMEM((2, TOP_K, tt, D_MODEL), F32), pltpu.SemaphoreType.DMA((2,))],
    )
    return pl.pallas_call(
        _combine_kernel, grid_spec=grid_spec,
        out_shape=jax.ShapeDtypeStruct((n_tok, D_MODEL), F32),
        compiler_params=_cparams(("arbitrary",)), name="moe_combine",
    )(dest, yb, route, res, gate)


def _moe(h, route, counts, res, gate, w1, b1, w2, b2):
    n_tok = h.shape[0]
    nk = n_tok * TOP_K
    n_blocks = -(-(nk + N_EXPERTS * (MOE_BLOCK - 1)) // MOE_BLOCK)
    n_slots = n_blocks * MOE_BLOCK
    expert = route[:, ROUTE_EXPERT_LANE:ROUTE_EXPERT_LANE + TOP_K].astype(jnp.int32)
    rank = route[:, ROUTE_RANK_LANE:ROUTE_RANK_LANE + TOP_K].astype(jnp.int32)
    counts = counts.reshape(N_EXPERTS).astype(jnp.int32)
    padded = (counts + MOE_BLOCK - 1) // MOE_BLOCK * MOE_BLOCK
    pend = jnp.cumsum(padded)
    pstart = pend - padded
    ids = jnp.arange(N_EXPERTS, dtype=jnp.int32)
    dest = rank + jnp.sum(jnp.where(expert[..., None] == ids, pstart, 0), axis=-1)
    dest = dest.reshape(nk).astype(jnp.int32)
    block_start = jnp.arange(n_blocks, dtype=jnp.int32) * MOE_BLOCK
    block_expert = jnp.minimum(jnp.sum((pend[None, :] <= block_start[:, None]).astype(jnp.int32), axis=1),
                               N_EXPERTS - 1).astype(jnp.int32)
    n_real = (pend[-1] // MOE_BLOCK).astype(jnp.int32).reshape(1)
    xb = _moe_dispatch(h, dest, n_slots)
    yb = _moe_experts(xb, block_expert, n_real, w1, b1, w2, b2)
    return _moe_combine(yb, dest, route, res, gate)


def kernel(x, c, ctx, c_ctx, ada_w, ada_b, norm1_g, norm2_g, na_w_qkv, na_q_g, na_k_g, na_rpb, na_w_o,
           ssd_w_in, ssd_conv_w, ssd_conv_b, ssd_dt_bias, ssd_a_log, ssd_d, ssd_norm_g, ssd_w_out,
           moe_w_router, moe_b_router, moe_w1, moe_b1, moe_w2, moe_b2):
    xa = jnp.concatenate([x.reshape(N_LAT, D_MODEL), ctx.reshape(N_CTX, D_MODEL)], axis=0)
    cond = jnp.concatenate([jax.nn.silu(c), jax.nn.silu(c_ctx)[None, :],
                            jnp.zeros((16 - BATCH - 1, D_MODEL), F32)], axis=0)
    for i in range(DEPTH):
        last = i == DEPTH - 1
        mod = _matmul(cond, ada_w[i], tm=16, tn=D_MODEL, bias=ada_b[i],
                      precision=lax.Precision.HIGHEST, name="ada_mod")
        mt = [mod[:, t * D_MODEL:(t + 1) * D_MODEL].reshape(16, 1, D_MODEL) for t in range(6)]
        h1 = _normmod(xa, norm1_g[i], mt[0], mt[1], N_ALL)
        n_out = N_LAT if last else N_ALL
        j = i // 2
        if i % 2 == 0:
            qkv = _qkv_proj(h1, na_w_qkv[j].astype(BF16), na_q_g[j], na_k_g[j])
            o_all = _neighbourhood_attention(qkv, _na_bias_table(na_rpb[j]))
            xa_new = _matmul_residual(o_all[:n_out], na_w_o[j].astype(BF16), xa, mt[2], tm=512,
                                      name="na_out_proj")
        else:
            w_in = ssd_w_in[j]
            zxbc = _matmul(h1, w_in[:, :SSD_MAIN_W].astype(BF16), tm=512, tn=1024, name="ssd_in_proj")
            dt_raw = _matmul(h1, w_in[:, SSD_MAIN_W:].astype(BF16), tm=512, tn=2 * SSD_HEADS,
                             name="ssd_dt_proj")
            xbc = _conv_silu(zxbc, ssd_conv_w[j], ssd_conv_b[j])
            y2 = _ssd_scan(xbc, dt_raw, ssd_dt_bias[j], ssd_a_log[j])
            gated = _ssd_gate_norm(y2, xbc, zxbc, ssd_d[j], ssd_norm_g[j], n_out)
            xa_new = _matmul_residual(gated, ssd_w_out[j].astype(BF16), xa, mt[2], tm=512, name="ssd_out_proj")
        n_tok = xa_new.shape[0]
        h2, route, counts = _normmod(xa_new, norm2_g[i], mt[3], mt[4], n_tok,
                                     router=(moe_w_router[i], moe_b_router[i]))
        xa = _moe(h2, route, counts, xa_new, mt[5], moe_w1[i].astype(BF16), moe_b1[i],
                  moe_w2[i].astype(BF16), moe_b2[i])
    return xa[:N_LAT].reshape(BATCH, SEQ, D_MODEL)
```

```python
import jax
import jax.numpy as jnp
from jax import lax
from jax.experimental import pallas as pl
from jax.experimental.pallas import tpu as pltpu

F32 = jnp.float32
BF16 = jnp.bfloat16

D_MODEL = 1024
BATCH = 8
SEQ = 2048
DEPTH = 2
GRID_W = 64
CTX_LEN = 256
N_LAT = BATCH * SEQ
N_CTX = BATCH * CTX_LEN
N_ALL = N_LAT + N_CTX

NA_HEADS = 16
NA_HEAD_DIM = 64
NA_KR = 8
NA_KC = 16
NA_ROWS = SEQ // GRID_W
NA_QROWS = 4
NA_BAND = 12
NA_QBLK = NA_QROWS * GRID_W
NA_KBLK = NA_BAND * GRID_W

SSD_D_INNER = 2048
SSD_HEAD_DIM = 64
SSD_HEADS = 32
SSD_GROUPS = 8
SSD_HPG = 4
SSD_STATE = 128
SSD_CONV_W = 5
SSD_CHUNK = 128
SSD_CONV_DIM = SSD_D_INNER + 2 * SSD_GROUPS * SSD_STATE
SSD_MAIN_W = SSD_D_INNER + SSD_CONV_DIM
SSD_GROUP_W = SSD_HPG * SSD_HEAD_DIM

N_EXPERTS = 32
TOP_K = 4
D_FF = 1024
SWIGLU_ALPHA = 1.702
SWIGLU_LIMIT = 7.0
MOE_BLOCK = 512
NORM_EPS = 1e-6
NEG_BIG = -1e30

VMEM_LIMIT = 56 * 1024 * 1024

_NT = (((1,), (1,)), ((), ()))
_TN = (((0,), (0,)), ((), ()))


def _cparams(sem):
    return pltpu.CompilerParams(dimension_semantics=sem, vmem_limit_bytes=VMEM_LIMIT)


def _mod_row(i, tm):
    return jnp.minimum((i * tm) // SEQ, BATCH)


def _split_bf16(x, pieces):
    out = []
    rem = x
    for _ in range(pieces):
        p = rem.astype(BF16)
        out.append(p)
        rem = rem - p.astype(F32)
    return out


def _dot_exact_rhs(x, sel, pieces, dims=None):
    acc = None
    for p in _split_bf16(x, pieces):
        if dims is None:
            t = jnp.dot(p, sel, preferred_element_type=F32)
        else:
            t = lax.dot_general(p, sel, dims, preferred_element_type=F32)
        acc = t if acc is None else acc + t
    return acc


def _dot_exact_lhs(sel, x, pieces):
    acc = None
    for p in _split_bf16(x, pieces):
        t = jnp.dot(sel, p, preferred_element_type=F32)
        acc = t if acc is None else acc + t
    return acc


def _normmod_kernel(x_ref, g_ref, sh_ref, sc_ref, o_ref):
    x = x_ref[...]
    ms = jnp.mean(x * x, axis=-1, keepdims=True)
    y = x * lax.rsqrt(ms + NORM_EPS) * g_ref[...]
    o_ref[...] = (y * (1.0 + sc_ref[0]) + sh_ref[0]).astype(o_ref.dtype)


ROUTE_EXPERT_LANE = 0
ROUTE_RANK_LANE = 4
ROUTE_GATE_LANE = 8


def _normmod_router_kernel(x_ref, g_ref, sh_ref, sc_ref, wr_ref, br_ref, o_ref, route_ref, count_ref):
    i = pl.program_id(0)
    x = x_ref[...]
    tm = x.shape[0]
    ms = jnp.mean(x * x, axis=-1, keepdims=True)
    y = x * lax.rsqrt(ms + NORM_EPS) * g_ref[...]
    h = y * (1.0 + sc_ref[0]) + sh_ref[0]
    o_ref[...] = h.astype(o_ref.dtype)
    logits = jnp.dot(h, wr_ref[...], preferred_element_type=F32,
                     precision=lax.Precision.HIGHEST) + br_ref[...]

    @pl.when(i == 0)
    def _():
        count_ref[...] = jnp.zeros_like(count_ref)

    lane_e = lax.broadcasted_iota(jnp.int32, logits.shape, 1).astype(F32)
    cur = logits
    vals, ids, hots = [], [], []
    for _ in range(TOP_K):
        m = jnp.max(cur, axis=-1, keepdims=True)
        idx = jnp.min(jnp.where(cur == m, lane_e, float(N_EXPERTS)), axis=-1, keepdims=True)
        hot = lane_e == idx
        vals.append(m)
        ids.append(idx)
        hots.append(hot)
        cur = jnp.where(hot, -jnp.inf, cur)
    exps = [jnp.exp(v - vals[0]) for v in vals]
    den = exps[0] + exps[1] + exps[2] + exps[3]
    picked = jnp.where(hots[0] | hots[1] | hots[2] | hots[3], 1.0, 0.0)
    row = lax.broadcasted_iota(jnp.int32, (tm, tm), 0)
    col = lax.broadcasted_iota(jnp.int32, (tm, tm), 1)
    earlier = jnp.where(col < row, 1.0, 0.0).astype(BF16)
    before = count_ref[...] + jnp.dot(earlier, picked.astype(BF16), preferred_element_type=F32)
    lane = lax.broadcasted_iota(jnp.int32, (tm, 128), 1)
    route = jnp.zeros((tm, 128), F32)
    for k in range(TOP_K):
        rank = jnp.sum(jnp.where(hots[k], before, 0.0), axis=-1, keepdims=True)
        route = jnp.where(lane == ROUTE_EXPERT_LANE + k, ids[k], route)
        route = jnp.where(lane == ROUTE_RANK_LANE + k, rank, route)
        route = jnp.where(lane == ROUTE_GATE_LANE + k, exps[k] / den, route)
    route_ref[...] = route
    count_ref[...] += jnp.sum(picked, axis=0, keepdims=True)


def _normmod(x, g, shift, scale, n_rows, router=None, tm=512):
    grid = (n_rows // tm,)
    x_spec = pl.BlockSpec((tm, D_MODEL), lambda i: (i, 0))
    g_spec = pl.BlockSpec((1, D_MODEL), lambda i: (0, 0))
    m_spec = pl.BlockSpec((1, 1, D_MODEL), lambda i: (_mod_row(i, tm), 0, 0))
    o_spec = pl.BlockSpec((tm, D_MODEL), lambda i: (i, 0))
    g2 = g.reshape(1, D_MODEL)
    if router is None:
        return pl.pallas_call(
            _normmod_kernel, grid=grid,
            in_specs=[x_spec, g_spec, m_spec, m_spec], out_specs=o_spec,
            out_shape=jax.ShapeDtypeStruct((n_rows, D_MODEL), BF16),
            compiler_params=_cparams(("arbitrary",)), name="normmod",
        )(x, g2, shift, scale)
    w_r, b_r = router
    return pl.pallas_call(
        _normmod_router_kernel, grid=grid,
        in_specs=[x_spec, g_spec, m_spec, m_spec,
                  pl.BlockSpec((D_MODEL, N_EXPERTS), lambda i: (0, 0)),
                  pl.BlockSpec((1, N_EXPERTS), lambda i: (0, 0))],
        out_specs=[o_spec, pl.BlockSpec((tm, 128), lambda i: (i, 0)),
                   pl.BlockSpec((1, N_EXPERTS), lambda i: (0, 0))],
        out_shape=[jax.ShapeDtypeStruct((n_rows, D_MODEL), F32),
                   jax.ShapeDtypeStruct((n_rows, 128), F32),
                   jax.ShapeDtypeStruct((1, N_EXPERTS), F32)],
        compiler_params=_cparams(("arbitrary",)), name="normmod_router",
    )(x, g2, shift, scale, w_r, b_r.reshape(1, N_EXPERTS))


def _mm_bias_f32_kernel(a_ref, w_ref, b_ref, o_ref):
    o_ref[...] = jnp.dot(a_ref[...], w_ref[...], preferred_element_type=F32,
                         precision=lax.Precision.HIGHEST) + b_ref[...]


def _matmul_bias_f32(a, w, bias, *, tn, name):
    m, k = a.shape
    n = w.shape[1]
    return pl.pallas_call(
        _mm_bias_f32_kernel, grid=(n // tn,),
        in_specs=[pl.BlockSpec((m, k), lambda j: (0, 0)),
                  pl.BlockSpec((k, tn), lambda j: (0, j)),
                  pl.BlockSpec((1, tn), lambda j: (0, j))],
        out_specs=pl.BlockSpec((m, tn), lambda j: (0, j)),
        out_shape=jax.ShapeDtypeStruct((m, n), F32),
        compiler_params=_cparams(("arbitrary",)), name=name,
    )(a, w, bias.reshape(1, n))


def _cast_weight_once(first, w_ref, wb_ref):
    @pl.when(first)
    def _():
        wb_ref[...] = w_ref[...].astype(BF16)


def _mm_kernel(a_ref, w_ref, o_ref, wb_ref):
    _cast_weight_once(pl.program_id(1) == 0, w_ref, wb_ref)
    o_ref[...] = jnp.dot(a_ref[...], wb_ref[...], preferred_element_type=F32).astype(o_ref.dtype)


def _mm_res_kernel(a_ref, w_ref, res_ref, gate_ref, o_ref, wb_ref):
    _cast_weight_once(pl.program_id(0) == 0, w_ref, wb_ref)
    acc = jnp.dot(a_ref[...], wb_ref[...], preferred_element_type=F32)
    o_ref[...] = res_ref[...] + gate_ref[0] * acc


def _matmul(a, w, *, tm, tn, n_cols, col0=0, out_dtype=F32, name="mm"):
    m, k = a.shape
    return pl.pallas_call(
        _mm_kernel, grid=(n_cols // tn, m // tm),
        in_specs=[pl.BlockSpec((tm, k), lambda j, i: (i, 0)),
                  pl.BlockSpec((k, tn), lambda j, i: (0, col0 + j))],
        out_specs=pl.BlockSpec((tm, tn), lambda j, i: (i, j)),
        out_shape=jax.ShapeDtypeStruct((m, n_cols), out_dtype),
        scratch_shapes=[pltpu.VMEM((k, tn), BF16)],
        compiler_params=_cparams(("arbitrary", "arbitrary")), name=name,
    )(a, w)


def _matmul_residual(a, w, res, gate, *, tm, name):
    m, k = a.shape
    n = w.shape[1]
    return pl.pallas_call(
        _mm_res_kernel, grid=(m // tm,),
        in_specs=[pl.BlockSpec((tm, k), lambda i: (i, 0)),
                  pl.BlockSpec((k, n), lambda i: (0, 0)),
                  pl.BlockSpec((tm, n), lambda i: (i, 0)),
                  pl.BlockSpec((1, 1, n), lambda i: (_mod_row(i, tm), 0, 0))],
        out_specs=pl.BlockSpec((tm, n), lambda i: (i, 0)),
        out_shape=jax.ShapeDtypeStruct((m, n), F32),
        scratch_shapes=[pltpu.VMEM((k, n), BF16)],
        compiler_params=_cparams(("arbitrary",)), name=name,
    )(a, w, res, gate)


def _qkv_kernel(a_ref, w_ref, gain_ref, seg_ref, segt_ref, o_ref, wb_ref):
    j = pl.program_id(0)
    _cast_weight_once(pl.program_id(1) == 0, w_ref, wb_ref)
    acc = jnp.dot(a_ref[...], wb_ref[...], preferred_element_type=F32)

    @pl.when(j < 2)
    def _():
        ss = _dot_exact_rhs(acc * acc, seg_ref[...], 2)
        inv = lax.rsqrt(ss * (1.0 / NA_HEAD_DIM) + NORM_EPS)
        o_ref[...] = (acc * _dot_exact_rhs(inv, segt_ref[...], 2) * gain_ref[0]).astype(o_ref.dtype)

    @pl.when(j == 2)
    def _():
        o_ref[...] = acc.astype(o_ref.dtype)


def _qkv_proj(h, w_qkv, q_g, k_g, tm=512):
    m = h.shape[0]
    head_of_col = jnp.arange(D_MODEL) // NA_HEAD_DIM
    seg = (head_of_col[:, None] == jnp.arange(128)[None, :]).astype(BF16)
    gain = jnp.stack([jnp.tile(q_g, NA_HEADS) * NA_HEAD_DIM ** -0.5, jnp.tile(k_g, NA_HEADS),
                      jnp.ones((D_MODEL,), F32)]).reshape(3, 1, D_MODEL)
    return pl.pallas_call(
        _qkv_kernel, grid=(3, m // tm),
        in_specs=[pl.BlockSpec((tm, D_MODEL), lambda j, i: (i, 0)),
                  pl.BlockSpec((D_MODEL, D_MODEL), lambda j, i: (0, j)),
                  pl.BlockSpec((1, 1, D_MODEL), lambda j, i: (j, 0, 0)),
                  pl.BlockSpec((D_MODEL, 128), lambda j, i: (0, 0)),
                  pl.BlockSpec((128, D_MODEL), lambda j, i: (0, 0))],
        out_specs=pl.BlockSpec((tm, D_MODEL), lambda j, i: (i, j)),
        out_shape=jax.ShapeDtypeStruct((m, 3 * D_MODEL), BF16),
        scratch_shapes=[pltpu.VMEM((D_MODEL, D_MODEL), BF16)],
        compiler_params=_cparams(("arbitrary", "arbitrary")), name="qkv_proj",
    )(h, w_qkv, gain, seg, seg.T)


def _na_bias_table(rpb):
    col = jnp.arange(GRID_W)
    col_start = jnp.clip(col - NA_KC // 2, 0, GRID_W - NA_KC)
    col_ok = (col[None, :] >= col_start[:, None]) & (col[None, :] < col_start[:, None] + NA_KC)
    col_idx = jnp.clip(col[None, :] - col[:, None] + NA_KC - 1, 0, 2 * NA_KC - 2)
    by_offset = jnp.take(rpb, col_idx.reshape(-1), axis=2).reshape(NA_HEADS, 2 * NA_KR - 1, GRID_W, GRID_W)
    by_offset = jnp.where(col_ok[None, None], by_offset, NEG_BIG)
    masked = jnp.full((NA_HEADS, GRID_W, GRID_W), NEG_BIG, F32)
    tables = []
    for rb in (0, 1, NA_ROWS // NA_QROWS - 1):
        r0 = rb * NA_QROWS
        ks = min(max(r0 - NA_KR // 2, 0), NA_ROWS - NA_BAND)
        per_row = []
        for i in range(NA_QROWS):
            r = r0 + i
            rs = min(max(r - NA_KR // 2, 0), NA_ROWS - NA_KR)
            per_n = []
            for n in range(NA_BAND):
                kr = ks + n
                in_window = rs <= kr < rs + NA_KR
                per_n.append(by_offset[:, kr - r + NA_KR - 1] if in_window else masked)
            per_row.append(jnp.concatenate(per_n, axis=-1))
        tables.append(jnp.concatenate(per_row, axis=1))
    return jnp.stack(tables, axis=0).astype(F32)


def _softmax_pv(s_list, v_list):
    m = s_list[0].max(axis=-1, keepdims=True)
    for s in s_list[1:]:
        m = jnp.maximum(m, s.max(axis=-1, keepdims=True))
    acc = None
    den = None
    for s, v in zip(s_list, v_list):
        p = jnp.exp(s - m)
        l = p.sum(axis=-1, keepdims=True)
        o = jnp.dot(p.astype(BF16), v, preferred_element_type=F32)
        acc = o if acc is None else acc + o
        den = l if den is None else den + l
    return acc / den


NA_LAT_STEPS = NA_ROWS // NA_QROWS


def _na_kernel(q_ref, k_ref, v_ref, kc_ref, vc_ref, tbl_ref, o_ref):
    rb = pl.program_id(2)
    q2 = q_ref[...]
    kc = kc_ref[...]
    vc = vc_ref[...]
    lane = lax.broadcasted_iota(jnp.int32, q2.shape, 1)

    def head_queries(sub):
        sel = (lane < NA_HEAD_DIM) if sub == 0 else (lane >= NA_HEAD_DIM)
        return jnp.where(sel, q2, jnp.zeros_like(q2))

    @pl.when(rb < NA_LAT_STEPS)
    def _():
        ks = jnp.clip(rb * NA_QROWS - NA_KR // 2, 0, NA_ROWS - NA_BAND)
        kstart = pl.multiple_of(ks * GRID_W, GRID_W)
        kb = k_ref[pl.ds(kstart, NA_KBLK), :]
        vb = v_ref[pl.ds(kstart, NA_KBLK), :]
        outs = []
        for sub in range(2):
            qm = head_queries(sub)
            s_lat = lax.dot_general(qm, kb, _NT, preferred_element_type=F32) + tbl_ref[0, sub]
            s_ctx = lax.dot_general(qm, kc, _NT, preferred_element_type=F32)
            outs.append(_softmax_pv([s_lat, s_ctx], [vb, vc]))
        o_ref[...] = jnp.where(lane < NA_HEAD_DIM, outs[0], outs[1]).astype(o_ref.dtype)

    @pl.when(rb == NA_LAT_STEPS)
    def _():
        outs = []
        for sub in range(2):
            s_ctx = lax.dot_general(head_queries(sub), kc, _NT, preferred_element_type=F32)
            outs.append(_softmax_pv([s_ctx], [vc]))
        o_ref[...] = jnp.where(lane < NA_HEAD_DIM, outs[0], outs[1]).astype(o_ref.dtype)


def _na_block_type(rb):
    return jnp.where(rb == 0, 0, jnp.where(rb >= NA_LAT_STEPS - 1, 2, 1))


def _neighbourhood_attention(qkv, table):
    hp = NA_HEADS // 2
    ctx_blk0 = N_LAT // CTX_LEN

    def q_block(b, h, r):
        return (jnp.where(r < NA_LAT_STEPS, b * NA_LAT_STEPS + r, ctx_blk0 + b), h)

    return pl.pallas_call(
        _na_kernel, grid=(BATCH, hp, NA_LAT_STEPS + 1),
        in_specs=[
            pl.BlockSpec((NA_QBLK, 128), q_block),
            pl.BlockSpec((SEQ, 128), lambda b, h, r: (b, hp + h)),
            pl.BlockSpec((SEQ, 128), lambda b, h, r: (b, 2 * hp + h)),
            pl.BlockSpec((CTX_LEN, 128), lambda b, h, r: (ctx_blk0 + b, hp + h)),
            pl.BlockSpec((CTX_LEN, 128), lambda b, h, r: (ctx_blk0 + b, 2 * hp + h)),
            pl.BlockSpec((1, 2, NA_QBLK, NA_KBLK), lambda b, h, r: (_na_block_type(r), h, 0, 0)),
        ],
        out_specs=pl.BlockSpec((NA_QBLK, 128), q_block),
        out_shape=jax.ShapeDtypeStruct((N_ALL, D_MODEL), BF16),
        compiler_params=_cparams(("arbitrary", "arbitrary", "arbitrary")), name="na_attention",
    )(qkv, qkv, qkv, qkv, qkv, table)


CONV_ROWS = 256
CONV_COLS = 2048
CONV_HALO = 8


def _conv_silu_kernel(cur_ref, prev_ref, next_ref, w_ref, b_ref, o_ref):
    i = pl.program_id(0)
    tiles_per_seq = SEQ // CONV_ROWS
    is_ctx = i >= N_LAT // CONV_ROWS
    first = jnp.logical_or(is_ctx, i % tiles_per_seq == 0)
    last = jnp.logical_or(is_ctx, i % tiles_per_seq == tiles_per_seq - 1)
    prev = jnp.where(first, 0.0, prev_ref[...])
    nxt = jnp.where(last, 0.0, next_ref[...])
    ext = jnp.concatenate([prev, cur_ref[...], nxt], axis=0)
    acc = jnp.zeros(o_ref.shape, F32) + b_ref[...]
    pad = SSD_CONV_W // 2
    for t in range(SSD_CONV_W):
        lo = CONV_HALO - pad + t
        acc = acc + ext[lo:lo + CONV_ROWS, :] * w_ref[t:t + 1, :]
    o_ref[...] = acc * jax.nn.sigmoid(acc)


def _conv_silu(zxbc, conv_w, conv_b):
    col0 = SSD_D_INNER // CONV_COLS
    n_halo_blocks = N_ALL // CONV_HALO
    per_tile = CONV_ROWS // CONV_HALO
    return pl.pallas_call(
        _conv_silu_kernel, grid=(N_ALL // CONV_ROWS, SSD_CONV_DIM // CONV_COLS),
        in_specs=[
            pl.BlockSpec((CONV_ROWS, CONV_COLS), lambda i, c: (i, col0 + c)),
            pl.BlockSpec((CONV_HALO, CONV_COLS), lambda i, c: (jnp.maximum(i * per_tile - 1, 0), col0 + c)),
            pl.BlockSpec((CONV_HALO, CONV_COLS),
                         lambda i, c: (jnp.minimum((i + 1) * per_tile, n_halo_blocks - 1), col0 + c)),
            pl.BlockSpec((SSD_CONV_W, CONV_COLS), lambda i, c: (0, c)),
            pl.BlockSpec((1, CONV_COLS), lambda i, c: (0, c)),
        ],
        out_specs=pl.BlockSpec((CONV_ROWS, CONV_COLS), lambda i, c: (i, c)),
        out_shape=jax.ShapeDtypeStruct((N_ALL, SSD_CONV_DIM), F32),
        compiler_params=_cparams(("arbitrary", "arbitrary")), name="ssd_conv_silu",
    )(zxbc, zxbc, zxbc, conv_w, conv_b.reshape(1, SSD_CONV_DIM))


SSD_STEPS = (CTX_LEN + SEQ) // SSD_CHUNK
SSD_CTX_CHUNKS = CTX_LEN // SSD_CHUNK
SSD_LAT_CHUNKS = SEQ // SSD_CHUNK


def _ssd_chunk(b, d, j):
    jj = jnp.where(d == 0, j, jnp.where(j < SSD_CTX_CHUNKS, SSD_CTX_CHUNKS - 1 - j,
                                        SSD_STEPS + SSD_CTX_CHUNKS - 1 - j))
    return jnp.where(jj < SSD_CTX_CHUNKS,
                     N_LAT // SSD_CHUNK + SSD_CTX_CHUNKS * b + jj,
                     SSD_LAT_CHUNKS * b + jj - SSD_CTX_CHUNKS)


def _ssd_kernel(xs_ref, b_ref, c_ref, dtr_ref, bias_ref, alog_ref, expand_ref, y_ref, state_ref):
    d = pl.program_id(1)
    j = pl.program_id(2)

    @pl.when(j == 0)
    def _():
        state_ref[...] = jnp.zeros_like(state_ref)

    fwd = d == 0
    dtr = dtr_ref[...]
    dt_pre = jnp.where(fwd, dtr[:, :SSD_HEADS], dtr[:, SSD_HEADS:]) + bias_ref[...]
    dt = jnp.maximum(dt_pre, 0.0) + jnp.log1p(jnp.exp(-jnp.abs(dt_pre)))
    da = dt * (-jnp.exp(alog_ref[...]))
    li = lax.broadcasted_iota(jnp.int32, (SSD_CHUNK, SSD_CHUNK), 0)
    si = lax.broadcasted_iota(jnp.int32, (SSD_CHUNK, SSD_CHUNK), 1)
    before = jnp.where(fwd, li - si, si - li) >= 0
    before_t = jnp.where(fwd, si - li, li - si) >= 0
    cum = _dot_exact_lhs(jnp.where(before, 1.0, 0.0).astype(BF16), da, 3)
    cum_t = _dot_exact_rhs(da, jnp.where(before_t, 1.0, 0.0).astype(BF16), 3, _TN)
    dt_t = _dot_exact_rhs(dt, jnp.where(li == si, 1.0, 0.0).astype(BF16), 2, _TN)
    tot = jnp.where(fwd, cum[SSD_CHUNK - 1:SSD_CHUNK, :], cum[0:1, :])
    expand = expand_ref[...]
    to_end = _dot_exact_rhs(dt * jnp.exp(tot - cum), expand, 2)
    from_start = _dot_exact_rhs(jnp.exp(cum), expand, 2)
    carry = jnp.where(fwd, from_start[SSD_CHUNK - 1:SSD_CHUNK, :], from_start[0:1, :])
    xs = xs_ref[...]
    x16 = xs.astype(BF16)
    xdec = (xs * to_end).astype(BF16)
    lane = lax.broadcasted_iota(jnp.int32, (SSD_CHUNK, 2 * SSD_HEAD_DIM), 1)
    for g in range(SSD_GROUPS):
        gs = slice(g * SSD_GROUP_W, (g + 1) * SSD_GROUP_W)
        bg = b_ref[:, g * SSD_STATE:(g + 1) * SSD_STATE].astype(BF16)
        cg = c_ref[:, g * SSD_STATE:(g + 1) * SSD_STATE].astype(BF16)
        cb = lax.dot_general(cg, bg, _NT, preferred_element_type=F32)
        st = state_ref[g]
        y_off = jnp.dot(cg, st.astype(BF16), preferred_element_type=F32) * from_start[:, gs]
        y_pairs = []
        for pr in range(SSD_HPG // 2):
            ms = []
            for r2 in range(2):
                h = g * SSD_HPG + pr * 2 + r2
                decay = jnp.exp(jnp.where(before, cum[:, h:h + 1] - cum_t[h:h + 1, :], NEG_BIG))
                ms.append((cb * decay * dt_t[h:h + 1, :]).astype(BF16))
            m2 = jnp.concatenate(ms, axis=1)
            c0 = g * SSD_GROUP_W + pr * 2 * SSD_HEAD_DIM
            slab = x16[:, c0:c0 + 2 * SSD_HEAD_DIM]
            zero = jnp.zeros_like(slab)
            xdiag = jnp.concatenate([jnp.where(lane < SSD_HEAD_DIM, slab, zero),
                                     jnp.where(lane >= SSD_HEAD_DIM, slab, zero)], axis=0)
            y_pairs.append(jnp.dot(m2, xdiag, preferred_element_type=F32))
        y_ref[:, gs] = jnp.concatenate(y_pairs, axis=1) + y_off
        s_new = lax.dot_general(bg, xdec[:, gs], _TN, preferred_element_type=F32)
        state_ref[g] = st * carry[:, gs] + s_new


def _ssd_scan(xbc, dt_raw, dt_bias, a_log):
    expand = (jnp.arange(SSD_HEADS)[:, None] == (jnp.arange(SSD_D_INNER) // SSD_HEAD_DIM)[None, :]).astype(BF16)
    bc_w = SSD_GROUPS * SSD_STATE
    b_col = SSD_D_INNER // bc_w
    return pl.pallas_call(
        _ssd_kernel, grid=(BATCH, 2, SSD_STEPS),
        in_specs=[
            pl.BlockSpec((SSD_CHUNK, SSD_D_INNER), lambda b, d, j: (_ssd_chunk(b, d, j), 0)),
            pl.BlockSpec((SSD_CHUNK, bc_w), lambda b, d, j: (_ssd_chunk(b, d, j), b_col)),
            pl.BlockSpec((SSD_CHUNK, bc_w), lambda b, d, j: (_ssd_chunk(b, d, j), b_col + 1)),
            pl.BlockSpec((SSD_CHUNK, 2 * SSD_HEADS), lambda b, d, j: (_ssd_chunk(b, d, j), 0)),
            pl.BlockSpec((None, 1, SSD_HEADS), lambda b, d, j: (d, 0, 0)),
            pl.BlockSpec((None, 1, SSD_HEADS), lambda b, d, j: (d, 0, 0)),
            pl.BlockSpec((SSD_HEADS, SSD_D_INNER), lambda b, d, j: (0, 0)),
        ],
        out_specs=pl.BlockSpec((None, SSD_CHUNK, SSD_D_INNER), lambda b, d, j: (d, _ssd_chunk(b, d, j), 0)),
        out_shape=jax.ShapeDtypeStruct((2, N_ALL, SSD_D_INNER), F32),
        scratch_shapes=[pltpu.VMEM((SSD_GROUPS, SSD_STATE, SSD_GROUP_W), F32)],
        compiler_params=_cparams(("arbitrary", "arbitrary", "arbitrary")), name="ssd_scan",
    )(xbc, xbc, xbc, dt_raw, dt_bias.reshape(2, 1, SSD_HEADS), a_log.reshape(2, 1, SSD_HEADS), expand)


def _ssd_gate_norm_kernel(yf_ref, yb_ref, xs_ref, z_ref, dskip_ref, g_ref, o_ref):
    z = z_ref[...]
    y = (yf_ref[...] + yb_ref[...] + dskip_ref[...] * xs_ref[...]) * (z * jax.nn.sigmoid(z))
    for grp in range(SSD_GROUPS):
        gs = slice(grp * SSD_GROUP_W, (grp + 1) * SSD_GROUP_W)
        yg = y[:, gs]
        ms = jnp.mean(yg * yg, axis=-1, keepdims=True)
        o_ref[:, gs] = (yg * lax.rsqrt(ms + NORM_EPS) * g_ref[:, gs]).astype(o_ref.dtype)


def _ssd_gate_norm(y2, xbc, zxbc, d_skip, norm_g, n_out, tm=256):
    d_cols = jnp.repeat(d_skip, SSD_HEAD_DIM).reshape(1, SSD_D_INNER)
    return pl.pallas_call(
        _ssd_gate_norm_kernel, grid=(n_out // tm,),
        in_specs=[
            pl.BlockSpec((None, tm, SSD_D_INNER), lambda i: (0, i, 0)),
            pl.BlockSpec((None, tm, SSD_D_INNER), lambda i: (1, i, 0)),
            pl.BlockSpec((tm, SSD_D_INNER), lambda i: (i, 0)),
            pl.BlockSpec((tm, SSD_D_INNER), lambda i: (i, 0)),
            pl.BlockSpec((1, SSD_D_INNER), lambda i: (0, 0)),
            pl.BlockSpec((1, SSD_D_INNER), lambda i: (0, 0)),
        ],
        out_specs=pl.BlockSpec((tm, SSD_D_INNER), lambda i: (i, 0)),
        out_shape=jax.ShapeDtypeStruct((n_out, SSD_D_INNER), BF16),
        compiler_params=_cparams(("arbitrary",)), name="ssd_gate_norm",
    )(y2, y2, xbc, zxbc, d_cols, norm_g.reshape(1, SSD_D_INNER))


DISPATCH_TOKENS = 256
COMBINE_TOKENS = 128
ZERO_FILL_ROWS = MOE_BLOCK + 8


def _dispatch_kernel(dest_ref, padstart_ref, nreal_ref, h_ref, xb_ref, zeros_ref, sem):
    i = pl.program_id(0)
    base = i * (DISPATCH_TOKENS * TOP_K)

    @pl.when(i == 0)
    def _():
        zeros_ref[...] = jnp.zeros_like(zeros_ref)
        n_slots = xb_ref.shape[0]

        def fill(e):
            start = jnp.minimum(padstart_ref[e] // 8 * 8, n_slots - ZERO_FILL_ROWS)
            return pltpu.make_async_copy(zeros_ref, xb_ref.at[pl.ds(pl.multiple_of(start, 8), ZERO_FILL_ROWS)],
                                         sem.at[0])

        for e in range(N_EXPERTS):
            fill(e).start()
        for e in range(N_EXPERTS):
            fill(e).wait()

        def fill_block(blk):
            return pltpu.make_async_copy(zeros_ref.at[pl.ds(0, MOE_BLOCK)],
                                         xb_ref.at[pl.ds(pl.multiple_of(blk * MOE_BLOCK, MOE_BLOCK), MOE_BLOCK)],
                                         sem.at[0])

        def start_block(blk, carry):
            fill_block(blk).start()
            return carry

        def wait_block(blk, carry):
            fill_block(blk).wait()
            return carry

        lax.fori_loop(nreal_ref[0], n_slots // MOE_BLOCK, start_block, 0)
        lax.fori_loop(nreal_ref[0], n_slots // MOE_BLOCK, wait_block, 0)

    def body(t, carry):
        for k in range(TOP_K):
            pltpu.make_async_copy(h_ref.at[pl.ds(t, 1)], xb_ref.at[pl.ds(dest_ref[base + t * TOP_K + k], 1)],
                                  sem.at[0]).start()
        return carry

    lax.fori_loop(0, DISPATCH_TOKENS, body, 0, unroll=4)
    for k in range(TOP_K):
        pltpu.make_async_copy(h_ref, xb_ref.at[pl.ds(0, DISPATCH_TOKENS)], sem.at[0]).wait()


def _moe_dispatch(h, dest, pad_start, n_real, n_slots):
    n_tok = h.shape[0]
    tt = DISPATCH_TOKENS
    grid_spec = pltpu.PrefetchScalarGridSpec(
        num_scalar_prefetch=3, grid=(n_tok // tt,),
        in_specs=[pl.BlockSpec((tt, D_MODEL), lambda i, d, p, nr: (i, 0))],
        out_specs=pl.BlockSpec(memory_space=pl.ANY),
        scratch_shapes=[pltpu.VMEM((ZERO_FILL_ROWS, D_MODEL), F32), pltpu.SemaphoreType.DMA((1,))],
    )
    return pl.pallas_call(
        _dispatch_kernel, grid_spec=grid_spec,
        out_shape=jax.ShapeDtypeStruct((n_slots, D_MODEL), F32),
        compiler_params=_cparams(("arbitrary",)), name="moe_dispatch",
    )(dest, pad_start, n_real, h)


def _moe_kernel(be_ref, nreal_ref, x_ref, w1_ref, b1_ref, w2_ref, b2_ref, o_ref, w1b_ref, w2b_ref):
    i = pl.program_id(0)

    @pl.when(i < nreal_ref[0])
    def _():
        new_expert = jnp.logical_or(i == 0, be_ref[i] != be_ref[jnp.maximum(i - 1, 0)])
        _cast_weight_once(new_expert, w1_ref, w1b_ref)
        _cast_weight_once(new_expert, w2_ref, w2b_ref)
        h = jnp.dot(x_ref[...].astype(BF16), w1b_ref[...], preferred_element_type=F32) + b1_ref[0]
        glu = jnp.minimum(h[:, :D_FF], SWIGLU_LIMIT)
        lin = jnp.clip(h[:, D_FF:], -SWIGLU_LIMIT, SWIGLU_LIMIT)
        act = glu * jax.nn.sigmoid(SWIGLU_ALPHA * glu) * (lin + 1.0)
        o_ref[...] = jnp.dot(act.astype(BF16), w2b_ref[...], preferred_element_type=F32) + b2_ref[0]

    @pl.when(i >= nreal_ref[0])
    def _():
        o_ref[...] = jnp.zeros_like(o_ref)


def _moe_experts(xb, block_expert, n_real, layer, w1, b1, w2, b2):
    n_slots = xb.shape[0]
    n_blocks = n_slots // MOE_BLOCK
    grid_spec = pltpu.PrefetchScalarGridSpec(
        num_scalar_prefetch=2, grid=(n_blocks,),
        in_specs=[
            pl.BlockSpec((MOE_BLOCK, D_MODEL), lambda i, be, nr: (jnp.minimum(i, nr[0] - 1), 0)),
            pl.BlockSpec((None, None, D_MODEL, 2 * D_FF), lambda i, be, nr: (layer, be[i], 0, 0)),
            pl.BlockSpec((1, 1, 2 * D_FF), lambda i, be, nr: (be[i], 0, 0)),
            pl.BlockSpec((None, None, D_FF, D_MODEL), lambda i, be, nr: (layer, be[i], 0, 0)),
            pl.BlockSpec((1, 1, D_MODEL), lambda i, be, nr: (be[i], 0, 0)),
        ],
        out_specs=pl.BlockSpec((MOE_BLOCK, D_MODEL), lambda i, be, nr: (i, 0)),
        scratch_shapes=[pltpu.VMEM((D_MODEL, 2 * D_FF), BF16), pltpu.VMEM((D_FF, D_MODEL), BF16)],
    )
    return pl.pallas_call(
        _moe_kernel, grid_spec=grid_spec,
        out_shape=jax.ShapeDtypeStruct((n_slots, D_MODEL), F32),
        compiler_params=_cparams(("arbitrary",)), name="moe_experts",
    )(block_expert, n_real, xb, w1, b1.reshape(N_EXPERTS, 1, 2 * D_FF), w2, b2.reshape(N_EXPERTS, 1, D_MODEL))


def _combine_kernel(dest_ref, y_hbm, route_ref, res_ref, gate_ref, o_ref, buf, sem):
    i = pl.program_id(0)
    n = pl.num_programs(0)

    def issue(tile, slot):
        base = tile * (COMBINE_TOKENS * TOP_K)

        def body(t, carry):
            for k in range(TOP_K):
                pltpu.make_async_copy(y_hbm.at[pl.ds(dest_ref[base + t * TOP_K + k], 1)],
                                      buf.at[slot, k, pl.ds(t, 1)], sem.at[slot]).start()
            return carry

        lax.fori_loop(0, COMBINE_TOKENS, body, 0, unroll=4)

    @pl.when(i == 0)
    def _():
        issue(0, 0)

    @pl.when(i + 1 < n)
    def _():
        issue(i + 1, (i + 1) % 2)

    slot = i % 2
    for k in range(TOP_K):
        pltpu.make_async_copy(y_hbm.at[pl.ds(0, COMBINE_TOKENS)], buf.at[slot, k], sem.at[slot]).wait()
    route = route_ref[...]
    y = None
    for k in range(TOP_K):
        t = route[:, ROUTE_GATE_LANE + k:ROUTE_GATE_LANE + k + 1] * buf[slot, k]
        y = t if y is None else y + t
    o_ref[...] = res_ref[...] + gate_ref[0] * y


def _moe_combine(yb, dest, route, res, gate):
    n_tok = route.shape[0]
    tt = COMBINE_TOKENS
    grid_spec = pltpu.PrefetchScalarGridSpec(
        num_scalar_prefetch=1, grid=(n_tok // tt,),
        in_specs=[
            pl.BlockSpec(memory_space=pl.ANY),
            pl.BlockSpec((tt, 128), lambda i, d: (i, 0)),
            pl.BlockSpec((tt, D_MODEL), lambda i, d: (i, 0)),
            pl.BlockSpec((1, 1, D_MODEL), lambda i, d: (_mod_row(i, tt), 0, 0)),
        ],
        out_specs=pl.BlockSpec((tt, D_MODEL), lambda i, d: (i, 0)),
        scratch_shapes=[pltpu.VMEM((2, TOP_K, tt, D_MODEL), F32), pltpu.SemaphoreType.DMA((2,))],
    )
    return pl.pallas_call(
        _combine_kernel, grid_spec=grid_spec,
        out_shape=jax.ShapeDtypeStruct((n_tok, D_MODEL), F32),
        compiler_params=_cparams(("arbitrary",)), name="moe_combine",
    )(dest, yb, route, res, gate)


def _moe(h, route, counts, res, gate, layer, w1, b1, w2, b2):
    n_tok = h.shape[0]
    nk = n_tok * TOP_K
    n_blocks = -(-(nk + N_EXPERTS * (MOE_BLOCK - 1)) // MOE_BLOCK)
    n_slots = n_blocks * MOE_BLOCK
    expert = route[:, ROUTE_EXPERT_LANE:ROUTE_EXPERT_LANE + TOP_K].astype(jnp.int32)
    rank = route[:, ROUTE_RANK_LANE:ROUTE_RANK_LANE + TOP_K].astype(jnp.int32)
    counts = counts.reshape(N_EXPERTS).astype(jnp.int32)
    padded = (counts + MOE_BLOCK - 1) // MOE_BLOCK * MOE_BLOCK
    pend = jnp.cumsum(padded)
    pstart = pend - padded
    ids = jnp.arange(N_EXPERTS, dtype=jnp.int32)
    dest = rank + jnp.sum(jnp.where(expert[..., None] == ids, pstart, 0), axis=-1)
    dest = dest.reshape(nk).astype(jnp.int32)
    block_start = jnp.arange(n_blocks, dtype=jnp.int32) * MOE_BLOCK
    block_expert = jnp.minimum(jnp.sum((pend[None, :] <= block_start[:, None]).astype(jnp.int32), axis=1),
                               N_EXPERTS - 1).astype(jnp.int32)
    n_real = (pend[-1] // MOE_BLOCK).astype(jnp.int32).reshape(1)
    xb = _moe_dispatch(h, dest, (pstart + counts).astype(jnp.int32), n_real, n_slots)
    yb = _moe_experts(xb, block_expert, n_real, layer, w1, b1, w2, b2)
    return _moe_combine(yb, dest, route, res, gate)


def kernel(x, c, ctx, c_ctx, ada_w, ada_b, norm1_g, norm2_g, na_w_qkv, na_q_g, na_k_g, na_rpb, na_w_o,
           ssd_w_in, ssd_conv_w, ssd_conv_b, ssd_dt_bias, ssd_a_log, ssd_d, ssd_norm_g, ssd_w_out,
           moe_w_router, moe_b_router, moe_w1, moe_b1, moe_w2, moe_b2):
    xa = jnp.concatenate([x.reshape(N_LAT, D_MODEL), ctx.reshape(N_CTX, D_MODEL)], axis=0)
    cond = jnp.concatenate([jax.nn.silu(c), jax.nn.silu(c_ctx)[None, :],
                            jnp.zeros((16 - BATCH - 1, D_MODEL), F32)], axis=0)
    for i in range(DEPTH):
        last = i == DEPTH - 1
        mod = _matmul_bias_f32(cond, ada_w[i], ada_b[i], tn=D_MODEL, name="ada_mod")
        mt = [mod[:, t * D_MODEL:(t + 1) * D_MODEL].reshape(16, 1, D_MODEL) for t in range(6)]
        h1 = _normmod(xa, norm1_g[i], mt[0], mt[1], N_ALL)
        n_out = N_LAT if last else N_ALL
        j = i // 2
        if i % 2 == 0:
            qkv = _qkv_proj(h1, na_w_qkv[j], na_q_g[j], na_k_g[j])
            o_all = _neighbourhood_attention(qkv, _na_bias_table(na_rpb[j]))
            xa_new = _matmul_residual(o_all[:n_out], na_w_o[j], xa, mt[2], tm=512, name="na_out_proj")
        else:
            w_in = ssd_w_in[j]
            zxbc = _matmul(h1, w_in, tm=512, tn=1024, n_cols=SSD_MAIN_W, name="ssd_in_proj")
            dt_raw = _matmul(h1, w_in[:, SSD_MAIN_W:], tm=512, tn=2 * SSD_HEADS, n_cols=2 * SSD_HEADS,
                             name="ssd_dt_proj")
            xbc = _conv_silu(zxbc, ssd_conv_w[j], ssd_conv_b[j])
            y2 = _ssd_scan(xbc, dt_raw, ssd_dt_bias[j], ssd_a_log[j])
            gated = _ssd_gate_norm(y2, xbc, zxbc, ssd_d[j], ssd_norm_g[j], n_out)
            xa_new = _matmul_residual(gated, ssd_w_out[j], xa, mt[2], tm=512, name="ssd_out_proj")
        n_tok = xa_new.shape[0]
        h2, route, counts = _normmod(xa_new, norm2_g[i], mt[3], mt[4], n_tok,
                                     router=(moe_w_router[i], moe_b_router[i]))
        xa = _moe(h2, route, counts, xa_new, mt[5], i, moe_w1, moe_b1[i], moe_w2, moe_b2[i])
    return xa[:N_LAT].reshape(BATCH, SEQ, D_MODEL)
```

```python
import jax
import jax.numpy as jnp
from jax import lax
from jax.experimental import pallas as pl
from jax.experimental.pallas import tpu as pltpu

F32 = jnp.float32
BF16 = jnp.bfloat16

D_MODEL = 1024
BATCH = 8
SEQ = 2048
DEPTH = 2
GRID_W = 64
CTX_LEN = 256
N_LAT = BATCH * SEQ
N_CTX = BATCH * CTX_LEN
N_ALL = N_LAT + N_CTX

NA_HEADS = 16
NA_HEAD_DIM = 64
NA_KR = 8
NA_KC = 16
NA_ROWS = SEQ // GRID_W
NA_QROWS = 4
NA_BAND = 12
NA_QBLK = NA_QROWS * GRID_W
NA_KBLK = NA_BAND * GRID_W

SSD_D_INNER = 2048
SSD_HEAD_DIM = 64
SSD_HEADS = 32
SSD_GROUPS = 8
SSD_HPG = 4
SSD_STATE = 128
SSD_CONV_W = 5
SSD_CHUNK = 128
SSD_CONV_DIM = SSD_D_INNER + 2 * SSD_GROUPS * SSD_STATE
SSD_MAIN_W = SSD_D_INNER + SSD_CONV_DIM
SSD_GROUP_W = SSD_HPG * SSD_HEAD_DIM

N_EXPERTS = 32
TOP_K = 4
D_FF = 1024
SWIGLU_ALPHA = 1.702
SWIGLU_LIMIT = 7.0
MOE_BLOCK = 512
NORM_EPS = 1e-6
NEG_BIG = -1e30

VMEM_LIMIT = 56 * 1024 * 1024

_NT = (((1,), (1,)), ((), ()))
_TN = (((0,), (0,)), ((), ()))


def _cparams(sem):
    return pltpu.CompilerParams(dimension_semantics=sem, vmem_limit_bytes=VMEM_LIMIT)


def _mod_row(i, tm):
    return jnp.minimum((i * tm) // SEQ, BATCH)


def _split_bf16(x, pieces):
    out = []
    rem = x
    for _ in range(pieces):
        p = rem.astype(BF16)
        out.append(p)
        rem = rem - p.astype(F32)
    return out


def _dot_exact_rhs(x, sel, pieces, dims=None):
    acc = None
    for p in _split_bf16(x, pieces):
        if dims is None:
            t = jnp.dot(p, sel, preferred_element_type=F32)
        else:
            t = lax.dot_general(p, sel, dims, preferred_element_type=F32)
        acc = t if acc is None else acc + t
    return acc


def _dot_exact_lhs(sel, x, pieces):
    acc = None
    for p in _split_bf16(x, pieces):
        t = jnp.dot(sel, p, preferred_element_type=F32)
        acc = t if acc is None else acc + t
    return acc


def _normmod_kernel(x_ref, g_ref, sh_ref, sc_ref, o_ref):
    x = x_ref[...]
    ms = jnp.mean(x * x, axis=-1, keepdims=True)
    y = x * lax.rsqrt(ms + NORM_EPS) * g_ref[...]
    o_ref[...] = (y * (1.0 + sc_ref[0]) + sh_ref[0]).astype(o_ref.dtype)


ROUTE_EXPERT_LANE = 0
ROUTE_RANK_LANE = 4
ROUTE_GATE_LANE = 8


def _normmod_router_kernel(x_ref, g_ref, sh_ref, sc_ref, wr_ref, br_ref, o_ref, route_ref, count_ref):
    i = pl.program_id(0)
    x = x_ref[...]
    tm = x.shape[0]
    ms = jnp.mean(x * x, axis=-1, keepdims=True)
    y = x * lax.rsqrt(ms + NORM_EPS) * g_ref[...]
    h = y * (1.0 + sc_ref[0]) + sh_ref[0]
    o_ref[...] = h.astype(o_ref.dtype)
    logits = jnp.dot(h, wr_ref[...], preferred_element_type=F32,
                     precision=lax.Precision.HIGHEST) + br_ref[...]

    @pl.when(i == 0)
    def _():
        count_ref[...] = jnp.zeros_like(count_ref)

    lane_e = lax.broadcasted_iota(jnp.int32, logits.shape, 1).astype(F32)
    cur = logits
    vals, ids, hots = [], [], []
    for _ in range(TOP_K):
        m = jnp.max(cur, axis=-1, keepdims=True)
        idx = jnp.min(jnp.where(cur == m, lane_e, float(N_EXPERTS)), axis=-1, keepdims=True)
        hot = lane_e == idx
        vals.append(m)
        ids.append(idx)
        hots.append(hot)
        cur = jnp.where(hot, -jnp.inf, cur)
    exps = [jnp.exp(v - vals[0]) for v in vals]
    den = exps[0] + exps[1] + exps[2] + exps[3]
    picked = jnp.where(hots[0] | hots[1] | hots[2] | hots[3], 1.0, 0.0)
    row = lax.broadcasted_iota(jnp.int32, (tm, tm), 0)
    col = lax.broadcasted_iota(jnp.int32, (tm, tm), 1)
    earlier = jnp.where(col < row, 1.0, 0.0).astype(BF16)
    before = count_ref[...] + jnp.dot(earlier, picked.astype(BF16), preferred_element_type=F32)
    lane = lax.broadcasted_iota(jnp.int32, (tm, 128), 1)
    route = jnp.zeros((tm, 128), F32)
    for k in range(TOP_K):
        rank = jnp.sum(jnp.where(hots[k], before, 0.0), axis=-1, keepdims=True)
        route = jnp.where(lane == ROUTE_EXPERT_LANE + k, ids[k], route)
        route = jnp.where(lane == ROUTE_RANK_LANE + k, rank, route)
        route = jnp.where(lane == ROUTE_GATE_LANE + k, exps[k] / den, route)
    route_ref[...] = route
    count_ref[...] += jnp.sum(picked, axis=0, keepdims=True)


def _normmod(x, g, shift, scale, n_rows, router=None, tm=512):
    grid = (n_rows // tm,)
    x_spec = pl.BlockSpec((tm, D_MODEL), lambda i: (i, 0))
    g_spec = pl.BlockSpec((1, D_MODEL), lambda i: (0, 0))
    m_spec = pl.BlockSpec((1, 1, D_MODEL), lambda i: (_mod_row(i, tm), 0, 0))
    o_spec = pl.BlockSpec((tm, D_MODEL), lambda i: (i, 0))
    g2 = g.reshape(1, D_MODEL)
    if router is None:
        return pl.pallas_call(
            _normmod_kernel, grid=grid,
            in_specs=[x_spec, g_spec, m_spec, m_spec], out_specs=o_spec,
            out_shape=jax.ShapeDtypeStruct((n_rows, D_MODEL), BF16),
            compiler_params=_cparams(("arbitrary",)), name="normmod",
        )(x, g2, shift, scale)
    w_r, b_r = router
    return pl.pallas_call(
        _normmod_router_kernel, grid=grid,
        in_specs=[x_spec, g_spec, m_spec, m_spec,
                  pl.BlockSpec((D_MODEL, N_EXPERTS), lambda i: (0, 0)),
                  pl.BlockSpec((1, N_EXPERTS), lambda i: (0, 0))],
        out_specs=[o_spec, pl.BlockSpec((tm, 128), lambda i: (i, 0)),
                   pl.BlockSpec((1, N_EXPERTS), lambda i: (0, 0))],
        out_shape=[jax.ShapeDtypeStruct((n_rows, D_MODEL), F32),
                   jax.ShapeDtypeStruct((n_rows, 128), F32),
                   jax.ShapeDtypeStruct((1, N_EXPERTS), F32)],
        compiler_params=_cparams(("arbitrary",)), name="normmod_router",
    )(x, g2, shift, scale, w_r, b_r.reshape(1, N_EXPERTS))


def _mm_bias_f32_kernel(a_ref, w_ref, b_ref, o_ref):
    o_ref[...] = jnp.dot(a_ref[...], w_ref[...], preferred_element_type=F32,
                         precision=lax.Precision.HIGHEST) + b_ref[...]


def _matmul_bias_f32(a, w, bias, *, tn, name):
    m, k = a.shape
    n = w.shape[1]
    return pl.pallas_call(
        _mm_bias_f32_kernel, grid=(n // tn,),
        in_specs=[pl.BlockSpec((m, k), lambda j: (0, 0)),
                  pl.BlockSpec((k, tn), lambda j: (0, j)),
                  pl.BlockSpec((1, tn), lambda j: (0, j))],
        out_specs=pl.BlockSpec((m, tn), lambda j: (0, j)),
        out_shape=jax.ShapeDtypeStruct((m, n), F32),
        compiler_params=_cparams(("arbitrary",)), name=name,
    )(a, w, bias.reshape(1, n))


def _cast_weight_once(first, w_ref, wb_ref):
    @pl.when(first)
    def _():
        wb_ref[...] = w_ref[...].astype(BF16)


def _mm_kernel(a_ref, w_ref, o_ref, wb_ref):
    _cast_weight_once(pl.program_id(1) == 0, w_ref, wb_ref)
    o_ref[...] = jnp.dot(a_ref[...], wb_ref[...], preferred_element_type=F32).astype(o_ref.dtype)


def _mm_res_kernel(a_ref, w_ref, res_ref, gate_ref, o_ref, wb_ref):
    _cast_weight_once(pl.program_id(0) == 0, w_ref, wb_ref)
    acc = jnp.dot(a_ref[...], wb_ref[...], preferred_element_type=F32)
    o_ref[...] = res_ref[...] + gate_ref[0] * acc


def _matmul(a, w, *, tm, tn, n_cols, col0=0, out_dtype=F32, name="mm"):
    m, k = a.shape
    return pl.pallas_call(
        _mm_kernel, grid=(n_cols // tn, m // tm),
        in_specs=[pl.BlockSpec((tm, k), lambda j, i: (i, 0)),
                  pl.BlockSpec((k, tn), lambda j, i: (0, col0 + j))],
        out_specs=pl.BlockSpec((tm, tn), lambda j, i: (i, j)),
        out_shape=jax.ShapeDtypeStruct((m, n_cols), out_dtype),
        scratch_shapes=[pltpu.VMEM((k, tn), BF16)],
        compiler_params=_cparams(("arbitrary", "arbitrary")), name=name,
    )(a, w)


def _matmul_residual(a, w, res, gate, *, tm, name):
    m, k = a.shape
    n = w.shape[1]
    return pl.pallas_call(
        _mm_res_kernel, grid=(m // tm,),
        in_specs=[pl.BlockSpec((tm, k), lambda i: (i, 0)),
                  pl.BlockSpec((k, n), lambda i: (0, 0)),
                  pl.BlockSpec((tm, n), lambda i: (i, 0)),
                  pl.BlockSpec((1, 1, n), lambda i: (_mod_row(i, tm), 0, 0))],
        out_specs=pl.BlockSpec((tm, n), lambda i: (i, 0)),
        out_shape=jax.ShapeDtypeStruct((m, n), F32),
        scratch_shapes=[pltpu.VMEM((k, n), BF16)],
        compiler_params=_cparams(("arbitrary",)), name=name,
    )(a, w, res, gate)


def _qkv_kernel(a_ref, w_ref, gain_ref, seg_ref, segt_ref, o_ref, wb_ref):
    j = pl.program_id(0)
    _cast_weight_once(pl.program_id(1) == 0, w_ref, wb_ref)
    acc = jnp.dot(a_ref[...], wb_ref[...], preferred_element_type=F32)

    @pl.when(j < 2)
    def _():
        ss = _dot_exact_rhs(acc * acc, seg_ref[...], 2)
        inv = lax.rsqrt(ss * (1.0 / NA_HEAD_DIM) + NORM_EPS)
        o_ref[...] = (acc * _dot_exact_rhs(inv, segt_ref[...], 2) * gain_ref[0]).astype(o_ref.dtype)

    @pl.when(j == 2)
    def _():
        o_ref[...] = acc.astype(o_ref.dtype)


def _qkv_proj(h, w_qkv, q_g, k_g, tm=1024):
    m = h.shape[0]
    head_of_col = jnp.arange(D_MODEL) // NA_HEAD_DIM
    seg = (head_of_col[:, None] == jnp.arange(128)[None, :]).astype(BF16)
    gain = jnp.stack([jnp.tile(q_g, NA_HEADS) * NA_HEAD_DIM ** -0.5, jnp.tile(k_g, NA_HEADS),
                      jnp.ones((D_MODEL,), F32)]).reshape(3, 1, D_MODEL)
    return pl.pallas_call(
        _qkv_kernel, grid=(3, m // tm),
        in_specs=[pl.BlockSpec((tm, D_MODEL), lambda j, i: (i, 0)),
                  pl.BlockSpec((D_MODEL, D_MODEL), lambda j, i: (0, j)),
                  pl.BlockSpec((1, 1, D_MODEL), lambda j, i: (j, 0, 0)),
                  pl.BlockSpec((D_MODEL, 128), lambda j, i: (0, 0)),
                  pl.BlockSpec((128, D_MODEL), lambda j, i: (0, 0))],
        out_specs=pl.BlockSpec((tm, D_MODEL), lambda j, i: (i, j)),
        out_shape=jax.ShapeDtypeStruct((m, 3 * D_MODEL), BF16),
        scratch_shapes=[pltpu.VMEM((D_MODEL, D_MODEL), BF16)],
        compiler_params=_cparams(("arbitrary", "arbitrary")), name="qkv_proj",
    )(h, w_qkv, gain, seg, seg.T)


def _na_bias_table(rpb):
    col = jnp.arange(GRID_W)
    col_start = jnp.clip(col - NA_KC // 2, 0, GRID_W - NA_KC)
    col_ok = (col[None, :] >= col_start[:, None]) & (col[None, :] < col_start[:, None] + NA_KC)
    col_idx = jnp.clip(col[None, :] - col[:, None] + NA_KC - 1, 0, 2 * NA_KC - 2)
    by_offset = jnp.take(rpb, col_idx.reshape(-1), axis=2).reshape(NA_HEADS, 2 * NA_KR - 1, GRID_W, GRID_W)
    by_offset = jnp.where(col_ok[None, None], by_offset, NEG_BIG)
    masked = jnp.full((NA_HEADS, GRID_W, GRID_W), NEG_BIG, F32)
    tables = []
    for rb in (0, 1, NA_ROWS // NA_QROWS - 1):
        r0 = rb * NA_QROWS
        ks = min(max(r0 - NA_KR // 2, 0), NA_ROWS - NA_BAND)
        per_row = []
        for i in range(NA_QROWS):
            r = r0 + i
            rs = min(max(r - NA_KR // 2, 0), NA_ROWS - NA_KR)
            per_n = []
            for n in range(NA_BAND):
                kr = ks + n
                in_window = rs <= kr < rs + NA_KR
                per_n.append(by_offset[:, kr - r + NA_KR - 1] if in_window else masked)
            per_row.append(jnp.concatenate(per_n, axis=-1))
        tables.append(jnp.concatenate(per_row, axis=1))
    return jnp.stack(tables, axis=0).astype(F32)


def _softmax_pv(s_list, v_list):
    m = s_list[0].max(axis=-1, keepdims=True)
    for s in s_list[1:]:
        m = jnp.maximum(m, s.max(axis=-1, keepdims=True))
    acc = None
    den = None
    for s, v in zip(s_list, v_list):
        p = jnp.exp(s - m)
        l = p.sum(axis=-1, keepdims=True)
        o = jnp.dot(p.astype(BF16), v, preferred_element_type=F32)
        acc = o if acc is None else acc + o
        den = l if den is None else den + l
    return acc / den


NA_LAT_STEPS = NA_ROWS // NA_QROWS


NA_STEP_HEADS = 4
NA_STEP_LANES = NA_STEP_HEADS * NA_HEAD_DIM


def _na_kernel(q_ref, k_ref, v_ref, kc_ref, vc_ref, tbl_ref, o_ref):
    rb = pl.program_id(2)
    lane = lax.broadcasted_iota(jnp.int32, (NA_QBLK, 2 * NA_HEAD_DIM), 1)

    def pair_lanes(pair):
        return slice(pair * 2 * NA_HEAD_DIM, (pair + 1) * 2 * NA_HEAD_DIM)

    def head_queries(q2, sub):
        sel = (lane < NA_HEAD_DIM) if sub == 0 else (lane >= NA_HEAD_DIM)
        return jnp.where(sel, q2, jnp.zeros_like(q2))

    @pl.when(rb < NA_LAT_STEPS)
    def _():
        ks = jnp.clip(rb * NA_QROWS - NA_KR // 2, 0, NA_ROWS - NA_BAND)
        kstart = pl.multiple_of(ks * GRID_W, GRID_W)
        for pair in range(NA_STEP_HEADS // 2):
            pl_ = pair_lanes(pair)
            q2 = q_ref[:, pl_]
            kb = k_ref[pl.ds(kstart, NA_KBLK), pl_]
            vb = v_ref[pl.ds(kstart, NA_KBLK), pl_]
            kc = kc_ref[:, pl_]
            vc = vc_ref[:, pl_]
            outs = []
            for sub in range(2):
                qm = head_queries(q2, sub)
                s_lat = lax.dot_general(qm, kb, _NT, preferred_element_type=F32) + tbl_ref[0, 2 * pair + sub]
                s_ctx = lax.dot_general(qm, kc, _NT, preferred_element_type=F32)
                outs.append(_softmax_pv([s_lat, s_ctx], [vb, vc]))
            o_ref[:, pl_] = jnp.where(lane < NA_HEAD_DIM, outs[0], outs[1]).astype(o_ref.dtype)

    @pl.when(rb == NA_LAT_STEPS)
    def _():
        for pair in range(NA_STEP_HEADS // 2):
            pl_ = pair_lanes(pair)
            q2 = q_ref[:, pl_]
            kc = kc_ref[:, pl_]
            vc = vc_ref[:, pl_]
            outs = []
            for sub in range(2):
                s_ctx = lax.dot_general(head_queries(q2, sub), kc, _NT, preferred_element_type=F32)
                outs.append(_softmax_pv([s_ctx], [vc]))
            o_ref[:, pl_] = jnp.where(lane < NA_HEAD_DIM, outs[0], outs[1]).astype(o_ref.dtype)


def _na_block_type(rb):
    return jnp.where(rb == 0, 0, jnp.where(rb >= NA_LAT_STEPS - 1, 2, 1))


def _neighbourhood_attention(qkv, table):
    groups = NA_HEADS // NA_STEP_HEADS
    ctx_blk0 = N_LAT // CTX_LEN

    def q_block(b, h, r):
        return (jnp.where(r < NA_LAT_STEPS, b * NA_LAT_STEPS + r, ctx_blk0 + b), h)

    return pl.pallas_call(
        _na_kernel, grid=(BATCH, groups, NA_LAT_STEPS + 1),
        in_specs=[
            pl.BlockSpec((NA_QBLK, NA_STEP_LANES), q_block),
            pl.BlockSpec((SEQ, NA_STEP_LANES), lambda b, h, r: (b, groups + h)),
            pl.BlockSpec((SEQ, NA_STEP_LANES), lambda b, h, r: (b, 2 * groups + h)),
            pl.BlockSpec((CTX_LEN, NA_STEP_LANES), lambda b, h, r: (ctx_blk0 + b, groups + h)),
            pl.BlockSpec((CTX_LEN, NA_STEP_LANES), lambda b, h, r: (ctx_blk0 + b, 2 * groups + h)),
            pl.BlockSpec((1, NA_STEP_HEADS, NA_QBLK, NA_KBLK), lambda b, h, r: (_na_block_type(r), h, 0, 0)),
        ],
        out_specs=pl.BlockSpec((NA_QBLK, NA_STEP_LANES), q_block),
        out_shape=jax.ShapeDtypeStruct((N_ALL, D_MODEL), BF16),
        compiler_params=_cparams(("arbitrary", "arbitrary", "arbitrary")), name="na_attention",
    )(qkv, qkv, qkv, qkv, qkv, table)


CONV_ROWS = 256
CONV_COLS = 2048
CONV_HALO = 8


def _conv_silu_kernel(cur_ref, prev_ref, next_ref, w_ref, b_ref, o_ref):
    i = pl.program_id(0)
    tiles_per_seq = SEQ // CONV_ROWS
    is_ctx = i >= N_LAT // CONV_ROWS
    first = jnp.logical_or(is_ctx, i % tiles_per_seq == 0)
    last = jnp.logical_or(is_ctx, i % tiles_per_seq == tiles_per_seq - 1)
    prev = jnp.where(first, 0.0, prev_ref[...])
    nxt = jnp.where(last, 0.0, next_ref[...])
    ext = jnp.concatenate([prev, cur_ref[...], nxt], axis=0)
    acc = jnp.zeros(o_ref.shape, F32) + b_ref[...]
    pad = SSD_CONV_W // 2
    for t in range(SSD_CONV_W):
        lo = CONV_HALO - pad + t
        acc = acc + ext[lo:lo + CONV_ROWS, :] * w_ref[t:t + 1, :]
    o_ref[...] = acc * jax.nn.sigmoid(acc)


def _conv_silu(zxbc, conv_w, conv_b):
    col0 = SSD_D_INNER // CONV_COLS
    n_halo_blocks = N_ALL // CONV_HALO
    per_tile = CONV_ROWS // CONV_HALO
    return pl.pallas_call(
        _conv_silu_kernel, grid=(N_ALL // CONV_ROWS, SSD_CONV_DIM // CONV_COLS),
        in_specs=[
            pl.BlockSpec((CONV_ROWS, CONV_COLS), lambda i, c: (i, col0 + c)),
            pl.BlockSpec((CONV_HALO, CONV_COLS), lambda i, c: (jnp.maximum(i * per_tile - 1, 0), col0 + c)),
            pl.BlockSpec((CONV_HALO, CONV_COLS),
                         lambda i, c: (jnp.minimum((i + 1) * per_tile, n_halo_blocks - 1), col0 + c)),
            pl.BlockSpec((SSD_CONV_W, CONV_COLS), lambda i, c: (0, c)),
            pl.BlockSpec((1, CONV_COLS), lambda i, c: (0, c)),
        ],
        out_specs=pl.BlockSpec((CONV_ROWS, CONV_COLS), lambda i, c: (i, c)),
        out_shape=jax.ShapeDtypeStruct((N_ALL, SSD_CONV_DIM), F32),
        compiler_params=_cparams(("arbitrary", "arbitrary")), name="ssd_conv_silu",
    )(zxbc, zxbc, zxbc, conv_w, conv_b.reshape(1, SSD_CONV_DIM))


SSD_STEPS = (CTX_LEN + SEQ) // SSD_CHUNK
SSD_CTX_CHUNKS = CTX_LEN // SSD_CHUNK
SSD_LAT_CHUNKS = SEQ // SSD_CHUNK


def _ssd_chunk(b, d, j):
    jj = jnp.where(d == 0, j, jnp.where(j < SSD_CTX_CHUNKS, SSD_CTX_CHUNKS - 1 - j,
                                        SSD_STEPS + SSD_CTX_CHUNKS - 1 - j))
    return jnp.where(jj < SSD_CTX_CHUNKS,
                     N_LAT // SSD_CHUNK + SSD_CTX_CHUNKS * b + jj,
                     SSD_LAT_CHUNKS * b + jj - SSD_CTX_CHUNKS)


def _ssd_kernel(xs_ref, b_ref, c_ref, dtr_ref, bias_ref, alog_ref, expand_ref, y_ref, state_ref):
    d = pl.program_id(1)
    j = pl.program_id(2)

    @pl.when(j == 0)
    def _():
        state_ref[...] = jnp.zeros_like(state_ref)

    fwd = d == 0
    dtr = dtr_ref[...]
    dt_pre = jnp.where(fwd, dtr[:, :SSD_HEADS], dtr[:, SSD_HEADS:]) + bias_ref[...]
    dt = jnp.maximum(dt_pre, 0.0) + jnp.log1p(jnp.exp(-jnp.abs(dt_pre)))
    da = dt * (-jnp.exp(alog_ref[...]))
    li = lax.broadcasted_iota(jnp.int32, (SSD_CHUNK, SSD_CHUNK), 0)
    si = lax.broadcasted_iota(jnp.int32, (SSD_CHUNK, SSD_CHUNK), 1)
    before = jnp.where(fwd, li - si, si - li) >= 0
    before_t = jnp.where(fwd, si - li, li - si) >= 0
    cum = _dot_exact_lhs(jnp.where(before, 1.0, 0.0).astype(BF16), da, 3)
    cum_t = _dot_exact_rhs(da, jnp.where(before_t, 1.0, 0.0).astype(BF16), 3, _TN)
    dt_t = _dot_exact_rhs(dt, jnp.where(li == si, 1.0, 0.0).astype(BF16), 2, _TN)
    tot = jnp.where(fwd, cum[SSD_CHUNK - 1:SSD_CHUNK, :], cum[0:1, :])
    expand = expand_ref[...]
    to_end = _dot_exact_rhs(dt * jnp.exp(tot - cum), expand, 2)
    from_start = _dot_exact_rhs(jnp.exp(cum), expand, 2)
    carry = jnp.where(fwd, from_start[SSD_CHUNK - 1:SSD_CHUNK, :], from_start[0:1, :])
    xs = xs_ref[...]
    x16 = xs.astype(BF16)
    xdec = (xs * to_end).astype(BF16)
    lane = lax.broadcasted_iota(jnp.int32, (SSD_CHUNK, 2 * SSD_HEAD_DIM), 1)
    for g in range(SSD_GROUPS):
        gs = slice(g * SSD_GROUP_W, (g + 1) * SSD_GROUP_W)
        bg = b_ref[:, g * SSD_STATE:(g + 1) * SSD_STATE].astype(BF16)
        cg = c_ref[:, g * SSD_STATE:(g + 1) * SSD_STATE].astype(BF16)
        cb = lax.dot_general(cg, bg, _NT, preferred_element_type=F32)
        st = state_ref[g]
        y_off = jnp.dot(cg, st.astype(BF16), preferred_element_type=F32) * from_start[:, gs]
        y_pairs = []
        for pr in range(SSD_HPG // 2):
            ms = []
            for r2 in range(2):
                h = g * SSD_HPG + pr * 2 + r2
                decay = jnp.exp(jnp.where(before, cum[:, h:h + 1] - cum_t[h:h + 1, :], NEG_BIG))
                ms.append((cb * decay * dt_t[h:h + 1, :]).astype(BF16))
            m2 = jnp.concatenate(ms, axis=1)
            c0 = g * SSD_GROUP_W + pr * 2 * SSD_HEAD_DIM
            slab = x16[:, c0:c0 + 2 * SSD_HEAD_DIM]
            zero = jnp.zeros_like(slab)
            xdiag = jnp.concatenate([jnp.where(lane < SSD_HEAD_DIM, slab, zero),
                                     jnp.where(lane >= SSD_HEAD_DIM, slab, zero)], axis=0)
            y_pairs.append(jnp.dot(m2, xdiag, preferred_element_type=F32))
        y_ref[:, gs] = jnp.concatenate(y_pairs, axis=1) + y_off
        s_new = lax.dot_general(bg, xdec[:, gs], _TN, preferred_element_type=F32)
        state_ref[g] = st * carry[:, gs] + s_new


def _ssd_scan(xbc, dt_raw, dt_bias, a_log):
    expand = (jnp.arange(SSD_HEADS)[:, None] == (jnp.arange(SSD_D_INNER) // SSD_HEAD_DIM)[None, :]).astype(BF16)
    bc_w = SSD_GROUPS * SSD_STATE
    b_col = SSD_D_INNER // bc_w
    return pl.pallas_call(
        _ssd_kernel, grid=(BATCH, 2, SSD_STEPS),
        in_specs=[
            pl.BlockSpec((SSD_CHUNK, SSD_D_INNER), lambda b, d, j: (_ssd_chunk(b, d, j), 0)),
            pl.BlockSpec((SSD_CHUNK, bc_w), lambda b, d, j: (_ssd_chunk(b, d, j), b_col)),
            pl.BlockSpec((SSD_CHUNK, bc_w), lambda b, d, j: (_ssd_chunk(b, d, j), b_col + 1)),
            pl.BlockSpec((SSD_CHUNK, 2 * SSD_HEADS), lambda b, d, j: (_ssd_chunk(b, d, j), 0)),
            pl.BlockSpec((None, 1, SSD_HEADS), lambda b, d, j: (d, 0, 0)),
            pl.BlockSpec((None, 1, SSD_HEADS), lambda b, d, j: (d, 0, 0)),
            pl.BlockSpec((SSD_HEADS, SSD_D_INNER), lambda b, d, j: (0, 0)),
        ],
        out_specs=pl.BlockSpec((None, SSD_CHUNK, SSD_D_INNER), lambda b, d, j: (d, _ssd_chunk(b, d, j), 0)),
        out_shape=jax.ShapeDtypeStruct((2, N_ALL, SSD_D_INNER), F32),
        scratch_shapes=[pltpu.VMEM((SSD_GROUPS, SSD_STATE, SSD_GROUP_W), F32)],
        compiler_params=_cparams(("arbitrary", "arbitrary", "arbitrary")), name="ssd_scan",
    )(xbc, xbc, xbc, dt_raw, dt_bias.reshape(2, 1, SSD_HEADS), a_log.reshape(2, 1, SSD_HEADS), expand)


def _ssd_out_kernel(yf_ref, yb_ref, xs_ref, z_ref, dskip_ref, g_ref, w_ref, res_ref, gate_ref, o_ref,
                    wb_ref, gated_ref):
    _cast_weight_once(pl.program_id(0) == 0, w_ref, wb_ref)
    z = z_ref[...]
    y = (yf_ref[...] + yb_ref[...] + dskip_ref[...] * xs_ref[...]) * (z * jax.nn.sigmoid(z))
    for grp in range(SSD_GROUPS):
        gs = slice(grp * SSD_GROUP_W, (grp + 1) * SSD_GROUP_W)
        yg = y[:, gs]
        ms = jnp.mean(yg * yg, axis=-1, keepdims=True)
        gated_ref[:, gs] = (yg * lax.rsqrt(ms + NORM_EPS) * g_ref[:, gs]).astype(BF16)
    acc = jnp.dot(gated_ref[...], wb_ref[...], preferred_element_type=F32)
    o_ref[...] = res_ref[...] + gate_ref[0] * acc


def _ssd_out_proj(y2, xbc, zxbc, d_skip, norm_g, w_out, res, gate, n_out, tm=256):
    d_cols = jnp.repeat(d_skip, SSD_HEAD_DIM).reshape(1, SSD_D_INNER)
    return pl.pallas_call(
        _ssd_out_kernel, grid=(n_out // tm,),
        in_specs=[
            pl.BlockSpec((None, tm, SSD_D_INNER), lambda i: (0, i, 0)),
            pl.BlockSpec((None, tm, SSD_D_INNER), lambda i: (1, i, 0)),
            pl.BlockSpec((tm, SSD_D_INNER), lambda i: (i, 0)),
            pl.BlockSpec((tm, SSD_D_INNER), lambda i: (i, 0)),
            pl.BlockSpec((1, SSD_D_INNER), lambda i: (0, 0)),
            pl.BlockSpec((1, SSD_D_INNER), lambda i: (0, 0)),
            pl.BlockSpec((SSD_D_INNER, D_MODEL), lambda i: (0, 0)),
            pl.BlockSpec((tm, D_MODEL), lambda i: (i, 0)),
            pl.BlockSpec((1, 1, D_MODEL), lambda i: (_mod_row(i, tm), 0, 0)),
        ],
        out_specs=pl.BlockSpec((tm, D_MODEL), lambda i: (i, 0)),
        out_shape=jax.ShapeDtypeStruct((n_out, D_MODEL), F32),
        scratch_shapes=[pltpu.VMEM((SSD_D_INNER, D_MODEL), BF16), pltpu.VMEM((tm, SSD_D_INNER), BF16)],
        compiler_params=_cparams(("arbitrary",)), name="ssd_out_proj",
    )(y2, y2, xbc, zxbc, d_cols, norm_g.reshape(1, SSD_D_INNER), w_out, res, gate)


DISPATCH_TOKENS = 1024
COMBINE_TOKENS = 256
ZERO_FILL_ROWS = MOE_BLOCK + 8


def _dispatch_kernel(dest_ref, padstart_ref, nreal_ref, h_ref, xb_ref, zeros_ref, sem):
    i = pl.program_id(0)
    base = i * (DISPATCH_TOKENS * TOP_K)

    @pl.when(i == 0)
    def _():
        zeros_ref[...] = jnp.zeros_like(zeros_ref)
        n_slots = xb_ref.shape[0]

        def fill(e):
            start = jnp.minimum(padstart_ref[e] // 8 * 8, n_slots - ZERO_FILL_ROWS)
            return pltpu.make_async_copy(zeros_ref, xb_ref.at[pl.ds(pl.multiple_of(start, 8), ZERO_FILL_ROWS)],
                                         sem.at[0])

        for e in range(N_EXPERTS):
            fill(e).start()
        for e in range(N_EXPERTS):
            fill(e).wait()

        def fill_block(blk):
            return pltpu.make_async_copy(zeros_ref.at[pl.ds(0, MOE_BLOCK)],
                                         xb_ref.at[pl.ds(pl.multiple_of(blk * MOE_BLOCK, MOE_BLOCK), MOE_BLOCK)],
                                         sem.at[0])

        def start_block(blk, carry):
            fill_block(blk).start()
            return carry

        def wait_block(blk, carry):
            fill_block(blk).wait()
            return carry

        lax.fori_loop(nreal_ref[0], n_slots // MOE_BLOCK, start_block, 0)
        lax.fori_loop(nreal_ref[0], n_slots // MOE_BLOCK, wait_block, 0)

    def body(t, carry):
        for k in range(TOP_K):
            pltpu.make_async_copy(h_ref.at[pl.ds(t, 1)], xb_ref.at[pl.ds(dest_ref[base + t * TOP_K + k], 1)],
                                  sem.at[0]).start()
        return carry

    lax.fori_loop(0, DISPATCH_TOKENS, body, 0, unroll=4)
    for k in range(TOP_K):
        pltpu.make_async_copy(h_ref, xb_ref.at[pl.ds(0, DISPATCH_TOKENS)], sem.at[0]).wait()


def _moe_dispatch(h, dest, pad_start, n_real, n_slots):
    n_tok = h.shape[0]
    tt = DISPATCH_TOKENS
    grid_spec = pltpu.PrefetchScalarGridSpec(
        num_scalar_prefetch=3, grid=(n_tok // tt,),
        in_specs=[pl.BlockSpec((tt, D_MODEL), lambda i, d, p, nr: (i, 0))],
        out_specs=pl.BlockSpec(memory_space=pl.ANY),
        scratch_shapes=[pltpu.VMEM((ZERO_FILL_ROWS, D_MODEL), F32), pltpu.SemaphoreType.DMA((1,))],
    )
    return pl.pallas_call(
        _dispatch_kernel, grid_spec=grid_spec,
        out_shape=jax.ShapeDtypeStruct((n_slots, D_MODEL), F32),
        compiler_params=_cparams(("arbitrary",)), name="moe_dispatch",
    )(dest, pad_start, n_real, h)


def _moe_kernel(be_ref, nreal_ref, x_ref, w1_ref, b1_ref, w2_ref, b2_ref, o_ref, w1b_ref, w2b_ref):
    i = pl.program_id(0)

    @pl.when(i < nreal_ref[0])
    def _():
        new_expert = jnp.logical_or(i == 0, be_ref[i] != be_ref[jnp.maximum(i - 1, 0)])
        _cast_weight_once(new_expert, w1_ref, w1b_ref)
        _cast_weight_once(new_expert, w2_ref, w2b_ref)
        h = jnp.dot(x_ref[...].astype(BF16), w1b_ref[...], preferred_element_type=F32) + b1_ref[0]
        glu = jnp.minimum(h[:, :D_FF], SWIGLU_LIMIT)
        lin = jnp.clip(h[:, D_FF:], -SWIGLU_LIMIT, SWIGLU_LIMIT)
        act = glu * jax.nn.sigmoid(SWIGLU_ALPHA * glu) * (lin + 1.0)
        o_ref[...] = jnp.dot(act.astype(BF16), w2b_ref[...], preferred_element_type=F32) + b2_ref[0]

    @pl.when(i >= nreal_ref[0])
    def _():
        o_ref[...] = jnp.zeros_like(o_ref)


def _moe_experts(xb, block_expert, n_real, layer, w1, b1, w2, b2):
    n_slots = xb.shape[0]
    n_blocks = n_slots // MOE_BLOCK
    grid_spec = pltpu.PrefetchScalarGridSpec(
        num_scalar_prefetch=2, grid=(n_blocks,),
        in_specs=[
            pl.BlockSpec((MOE_BLOCK, D_MODEL), lambda i, be, nr: (jnp.minimum(i, nr[0] - 1), 0)),
            pl.BlockSpec((None, None, D_MODEL, 2 * D_FF), lambda i, be, nr: (layer, be[i], 0, 0)),
            pl.BlockSpec((1, 1, 2 * D_FF), lambda i, be, nr: (be[i], 0, 0)),
            pl.BlockSpec((None, None, D_FF, D_MODEL), lambda i, be, nr: (layer, be[i], 0, 0)),
            pl.BlockSpec((1, 1, D_MODEL), lambda i, be, nr: (be[i], 0, 0)),
        ],
        out_specs=pl.BlockSpec((MOE_BLOCK, D_MODEL), lambda i, be, nr: (i, 0)),
        scratch_shapes=[pltpu.VMEM((D_MODEL, 2 * D_FF), BF16), pltpu.VMEM((D_FF, D_MODEL), BF16)],
    )
    return pl.pallas_call(
        _moe_kernel, grid_spec=grid_spec,
        out_shape=jax.ShapeDtypeStruct((n_slots, D_MODEL), F32),
        compiler_params=_cparams(("arbitrary",)), name="moe_experts",
    )(block_expert, n_real, xb, w1, b1.reshape(N_EXPERTS, 1, 2 * D_FF), w2, b2.reshape(N_EXPERTS, 1, D_MODEL))


def _combine_kernel(dest_ref, y_hbm, route_ref, res_ref, gate_ref, o_ref, buf, sem):
    i = pl.program_id(0)
    n = pl.num_programs(0)

    def issue(tile, slot):
        base = tile * (COMBINE_TOKENS * TOP_K)

        def body(t, carry):
            for k in range(TOP_K):
                pltpu.make_async_copy(y_hbm.at[pl.ds(dest_ref[base + t * TOP_K + k], 1)],
                                      buf.at[slot, k, pl.ds(t, 1)], sem.at[slot]).start()
            return carry

        lax.fori_loop(0, COMBINE_TOKENS, body, 0, unroll=4)

    @pl.when(i == 0)
    def _():
        issue(0, 0)

    @pl.when(i + 1 < n)
    def _():
        issue(i + 1, (i + 1) % 2)

    slot = i % 2
    for k in range(TOP_K):
        pltpu.make_async_copy(y_hbm.at[pl.ds(0, COMBINE_TOKENS)], buf.at[slot, k], sem.at[slot]).wait()
    route = route_ref[...]
    y = None
    for k in range(TOP_K):
        t = route[:, ROUTE_GATE_LANE + k:ROUTE_GATE_LANE + k + 1] * buf[slot, k]
        y = t if y is None else y + t
    o_ref[...] = res_ref[...] + gate_ref[0] * y


def _moe_combine(yb, dest, route, res, gate):
    n_tok = route.shape[0]
    tt = COMBINE_TOKENS
    grid_spec = pltpu.PrefetchScalarGridSpec(
        num_scalar_prefetch=1, grid=(n_tok // tt,),
        in_specs=[
            pl.BlockSpec(memory_space=pl.ANY),
            pl.BlockSpec((tt, 128), lambda i, d: (i, 0)),
            pl.BlockSpec((tt, D_MODEL), lambda i, d: (i, 0)),
            pl.BlockSpec((1, 1, D_MODEL), lambda i, d: (_mod_row(i, tt), 0, 0)),
        ],
        out_specs=pl.BlockSpec((tt, D_MODEL), lambda i, d: (i, 0)),
        scratch_shapes=[pltpu.VMEM((2, TOP_K, tt, D_MODEL), F32), pltpu.SemaphoreType.DMA((2,))],
    )
    return pl.pallas_call(
        _combine_kernel, grid_spec=grid_spec,
        out_shape=jax.ShapeDtypeStruct((n_tok, D_MODEL), F32),
        compiler_params=_cparams(("arbitrary",)), name="moe_combine",
    )(dest, yb, route, res, gate)


def _moe(h, route, counts, res, gate, layer, w1, b1, w2, b2):
    n_tok = h.shape[0]
    nk = n_tok * TOP_K
    n_blocks = -(-(nk + N_EXPERTS * (MOE_BLOCK - 1)) // MOE_BLOCK)
    n_slots = n_blocks * MOE_BLOCK
    expert = route[:, ROUTE_EXPERT_LANE:ROUTE_EXPERT_LANE + TOP_K].astype(jnp.int32)
    rank = route[:, ROUTE_RANK_LANE:ROUTE_RANK_LANE + TOP_K].astype(jnp.int32)
    counts = counts.reshape(N_EXPERTS).astype(jnp.int32)
    padded = (counts + MOE_BLOCK - 1) // MOE_BLOCK * MOE_BLOCK
    pend = jnp.cumsum(padded)
    pstart = pend - padded
    ids = jnp.arange(N_EXPERTS, dtype=jnp.int32)
    dest = rank + jnp.sum(jnp.where(expert[..., None] == ids, pstart, 0), axis=-1)
    dest = dest.reshape(nk).astype(jnp.int32)
    block_start = jnp.arange(n_blocks, dtype=jnp.int32) * MOE_BLOCK
    block_expert = jnp.minimum(jnp.sum((pend[None, :] <= block_start[:, None]).astype(jnp.int32), axis=1),
                               N_EXPERTS - 1).astype(jnp.int32)
    n_real = (pend[-1] // MOE_BLOCK).astype(jnp.int32).reshape(1)
    xb = _moe_dispatch(h, dest, (pstart + counts).astype(jnp.int32), n_real, n_slots)
    yb = _moe_experts(xb, block_expert, n_real, layer, w1, b1, w2, b2)
    return _moe_combine(yb, dest, route, res, gate)


def kernel(x, c, ctx, c_ctx, ada_w, ada_b, norm1_g, norm2_g, na_w_qkv, na_q_g, na_k_g, na_rpb, na_w_o,
           ssd_w_in, ssd_conv_w, ssd_conv_b, ssd_dt_bias, ssd_a_log, ssd_d, ssd_norm_g, ssd_w_out,
           moe_w_router, moe_b_router, moe_w1, moe_b1, moe_w2, moe_b2):
    xa = jnp.concatenate([x.reshape(N_LAT, D_MODEL), ctx.reshape(N_CTX, D_MODEL)], axis=0)
    cond = jnp.concatenate([jax.nn.silu(c), jax.nn.silu(c_ctx)[None, :],
                            jnp.zeros((16 - BATCH - 1, D_MODEL), F32)], axis=0)
    for i in range(DEPTH):
        last = i == DEPTH - 1
        mod = _matmul_bias_f32(cond, ada_w[i], ada_b[i], tn=D_MODEL, name="ada_mod")
        mt = [mod[:, t * D_MODEL:(t + 1) * D_MODEL].reshape(16, 1, D_MODEL) for t in range(6)]
        h1 = _normmod(xa, norm1_g[i], mt[0], mt[1], N_ALL)
        n_out = N_LAT if last else N_ALL
        j = i // 2
        if i % 2 == 0:
            qkv = _qkv_proj(h1, na_w_qkv[j], na_q_g[j], na_k_g[j])
            o_all = _neighbourhood_attention(qkv, _na_bias_table(na_rpb[j]))
            xa_new = _matmul_residual(o_all[:n_out], na_w_o[j], xa, mt[2], tm=512, name="na_out_proj")
        else:
            w_in = ssd_w_in[j]
            zxbc = _matmul(h1, w_in, tm=1024, tn=1024, n_cols=SSD_MAIN_W, name="ssd_in_proj")
            dt_raw = _matmul(h1, w_in[:, SSD_MAIN_W:], tm=512, tn=2 * SSD_HEADS, n_cols=2 * SSD_HEADS,
                             name="ssd_dt_proj")
            xbc = _conv_silu(zxbc, ssd_conv_w[j], ssd_conv_b[j])
            y2 = _ssd_scan(xbc, dt_raw, ssd_dt_bias[j], ssd_a_log[j])
            xa_new = _ssd_out_proj(y2, xbc, zxbc, ssd_d[j], ssd_norm_g[j], ssd_w_out[j], xa, mt[2], n_out)
        n_tok = xa_new.shape[0]
        h2, route, counts = _normmod(xa_new, norm2_g[i], mt[3], mt[4], n_tok,
                                     router=(moe_w_router[i], moe_b_router[i]))
        xa = _moe(h2, route, counts, xa_new, mt[5], i, moe_w1, moe_b1[i], moe_w2, moe_b2[i])
    return xa[:N_LAT].reshape(BATCH, SEQ, D_MODEL)
```

```python
import jax
import jax.numpy as jnp
from jax import lax
from jax.experimental import pallas as pl
from jax.experimental.pallas import tpu as pltpu

F32 = jnp.float32
BF16 = jnp.bfloat16

D_MODEL = 1024
BATCH = 8
SEQ = 2048
DEPTH = 2
GRID_W = 64
CTX_LEN = 256
N_LAT = BATCH * SEQ
N_CTX = BATCH * CTX_LEN
N_ALL = N_LAT + N_CTX

NA_HEADS = 16
NA_HEAD_DIM = 64
NA_KR = 8
NA_KC = 16
NA_ROWS = SEQ // GRID_W
NA_QROWS = 4
NA_BAND = 12
NA_QBLK = NA_QROWS * GRID_W
NA_KBLK = NA_BAND * GRID_W

SSD_D_INNER = 2048
SSD_HEAD_DIM = 64
SSD_HEADS = 32
SSD_GROUPS = 8
SSD_HPG = 4
SSD_STATE = 128
SSD_CONV_W = 5
SSD_CHUNK = 128
SSD_CONV_DIM = SSD_D_INNER + 2 * SSD_GROUPS * SSD_STATE
SSD_MAIN_W = SSD_D_INNER + SSD_CONV_DIM
SSD_GROUP_W = SSD_HPG * SSD_HEAD_DIM

N_EXPERTS = 32
TOP_K = 4
D_FF = 1024
SWIGLU_ALPHA = 1.702
SWIGLU_LIMIT = 7.0
MOE_BLOCK = 512
NORM_EPS = 1e-6
NEG_BIG = -1e30

VMEM_LIMIT = 56 * 1024 * 1024

_NT = (((1,), (1,)), ((), ()))
_TN = (((0,), (0,)), ((), ()))


def _cparams(sem):
    return pltpu.CompilerParams(dimension_semantics=sem, vmem_limit_bytes=VMEM_LIMIT)


def _mod_row(i, tm):
    return jnp.minimum((i * tm) // SEQ, BATCH)


def _split_bf16(x, pieces):
    out = []
    rem = x
    for _ in range(pieces):
        p = rem.astype(BF16)
        out.append(p)
        rem = rem - p.astype(F32)
    return out


def _dot_exact_rhs(x, sel, pieces, dims=None):
    acc = None
    for p in _split_bf16(x, pieces):
        if dims is None:
            t = jnp.dot(p, sel, preferred_element_type=F32)
        else:
            t = lax.dot_general(p, sel, dims, preferred_element_type=F32)
        acc = t if acc is None else acc + t
    return acc


def _dot_exact_lhs(sel, x, pieces):
    acc = None
    for p in _split_bf16(x, pieces):
        t = jnp.dot(sel, p, preferred_element_type=F32)
        acc = t if acc is None else acc + t
    return acc


def _normmod_kernel(x_ref, g_ref, sh_ref, sc_ref, o_ref):
    x = x_ref[...]
    ms = jnp.mean(x * x, axis=-1, keepdims=True)
    y = x * lax.rsqrt(ms + NORM_EPS) * g_ref[...]
    o_ref[...] = (y * (1.0 + sc_ref[0]) + sh_ref[0]).astype(o_ref.dtype)


ROUTE_EXPERT_LANE = 0
ROUTE_RANK_LANE = 4
ROUTE_GATE_LANE = 8


def _normmod_router_kernel(x_ref, g_ref, sh_ref, sc_ref, wr_ref, br_ref, o_ref, route_ref, count_ref):
    i = pl.program_id(0)
    x = x_ref[...]
    tm = x.shape[0]
    ms = jnp.mean(x * x, axis=-1, keepdims=True)
    y = x * lax.rsqrt(ms + NORM_EPS) * g_ref[...]
    h = y * (1.0 + sc_ref[0]) + sh_ref[0]
    o_ref[...] = h.astype(o_ref.dtype)
    h_hi, h_lo = _split_bf16(h, 2)
    w_hi, w_lo = _split_bf16(wr_ref[...], 2)
    logits = (jnp.dot(h_hi, w_hi, preferred_element_type=F32) + jnp.dot(h_hi, w_lo, preferred_element_type=F32)
              + jnp.dot(h_lo, w_hi, preferred_element_type=F32)) + br_ref[...]

    @pl.when(i == 0)
    def _():
        count_ref[...] = jnp.zeros_like(count_ref)

    lane_e = lax.broadcasted_iota(jnp.int32, logits.shape, 1).astype(F32)
    cur = logits
    vals, ids, hots = [], [], []
    for _ in range(TOP_K):
        m = jnp.max(cur, axis=-1, keepdims=True)
        idx = jnp.min(jnp.where(cur == m, lane_e, float(N_EXPERTS)), axis=-1, keepdims=True)
        hot = lane_e == idx
        vals.append(m)
        ids.append(idx)
        hots.append(hot)
        cur = jnp.where(hot, -jnp.inf, cur)
    exps = [jnp.exp(v - vals[0]) for v in vals]
    den = exps[0] + exps[1] + exps[2] + exps[3]
    picked = jnp.where(hots[0] | hots[1] | hots[2] | hots[3], 1.0, 0.0)
    row = lax.broadcasted_iota(jnp.int32, (tm, tm), 0)
    col = lax.broadcasted_iota(jnp.int32, (tm, tm), 1)
    earlier = jnp.where(col < row, 1.0, 0.0).astype(BF16)
    before = count_ref[...] + jnp.dot(earlier, picked.astype(BF16), preferred_element_type=F32)
    lane = lax.broadcasted_iota(jnp.int32, (tm, 128), 1)
    route = jnp.zeros((tm, 128), F32)
    for k in range(TOP_K):
        rank = jnp.sum(jnp.where(hots[k], before, 0.0), axis=-1, keepdims=True)
        route = jnp.where(lane == ROUTE_EXPERT_LANE + k, ids[k], route)
        route = jnp.where(lane == ROUTE_RANK_LANE + k, rank, route)
        route = jnp.where(lane == ROUTE_GATE_LANE + k, exps[k] / den, route)
    route_ref[...] = route
    count_ref[...] += jnp.sum(picked, axis=0, keepdims=True)


def _normmod(x, g, shift, scale, n_rows, router=None, tm=512):
    grid = (n_rows // tm,)
    x_spec = pl.BlockSpec((tm, D_MODEL), lambda i: (i, 0))
    g_spec = pl.BlockSpec((1, D_MODEL), lambda i: (0, 0))
    m_spec = pl.BlockSpec((1, 1, D_MODEL), lambda i: (_mod_row(i, tm), 0, 0))
    o_spec = pl.BlockSpec((tm, D_MODEL), lambda i: (i, 0))
    g2 = g.reshape(1, D_MODEL)
    if router is None:
        return pl.pallas_call(
            _normmod_kernel, grid=grid,
            in_specs=[x_spec, g_spec, m_spec, m_spec], out_specs=o_spec,
            out_shape=jax.ShapeDtypeStruct((n_rows, D_MODEL), BF16),
            compiler_params=_cparams(("arbitrary",)), name="normmod",
        )(x, g2, shift, scale)
    w_r, b_r = router
    return pl.pallas_call(
        _normmod_router_kernel, grid=grid,
        in_specs=[x_spec, g_spec, m_spec, m_spec,
                  pl.BlockSpec((D_MODEL, N_EXPERTS), lambda i: (0, 0)),
                  pl.BlockSpec((1, N_EXPERTS), lambda i: (0, 0))],
        out_specs=[o_spec, pl.BlockSpec((tm, 128), lambda i: (i, 0)),
                   pl.BlockSpec((1, N_EXPERTS), lambda i: (0, 0))],
        out_shape=[jax.ShapeDtypeStruct((n_rows, D_MODEL), F32),
                   jax.ShapeDtypeStruct((n_rows, 128), F32),
                   jax.ShapeDtypeStruct((1, N_EXPERTS), F32)],
        compiler_params=_cparams(("arbitrary",)), name="normmod_router",
    )(x, g2, shift, scale, w_r, b_r.reshape(1, N_EXPERTS))


def _mm_bias_f32_kernel(a_ref, w_ref, b_ref, o_ref):
    o_ref[...] = jnp.dot(a_ref[...], w_ref[...], preferred_element_type=F32,
                         precision=lax.Precision.HIGHEST) + b_ref[...]


def _matmul_bias_f32(a, w, bias, *, tn, name):
    m, k = a.shape
    n = w.shape[1]
    return pl.pallas_call(
        _mm_bias_f32_kernel, grid=(n // tn,),
        in_specs=[pl.BlockSpec((m, k), lambda j: (0, 0)),
                  pl.BlockSpec((k, tn), lambda j: (0, j)),
                  pl.BlockSpec((1, tn), lambda j: (0, j))],
        out_specs=pl.BlockSpec((m, tn), lambda j: (0, j)),
        out_shape=jax.ShapeDtypeStruct((m, n), F32),
        compiler_params=_cparams(("arbitrary",)), name=name,
    )(a, w, bias.reshape(1, n))


def _cast_weight_once(first, w_ref, wb_ref):
    @pl.when(first)
    def _():
        wb_ref[...] = w_ref[...].astype(BF16)


def _mm_kernel(a_ref, w_ref, o_ref, wb_ref):
    _cast_weight_once(pl.program_id(1) == 0, w_ref, wb_ref)
    o_ref[...] = jnp.dot(a_ref[...], wb_ref[...], preferred_element_type=F32).astype(o_ref.dtype)


def _mm_res_kernel(a_ref, w_ref, res_ref, gate_ref, o_ref, wb_ref):
    _cast_weight_once(pl.program_id(0) == 0, w_ref, wb_ref)
    acc = jnp.dot(a_ref[...], wb_ref[...], preferred_element_type=F32)
    o_ref[...] = res_ref[...] + gate_ref[0] * acc


def _matmul(a, w, *, tm, tn, n_cols, col0=0, out_dtype=F32, name="mm"):
    m, k = a.shape
    return pl.pallas_call(
        _mm_kernel, grid=(n_cols // tn, m // tm),
        in_specs=[pl.BlockSpec((tm, k), lambda j, i: (i, 0)),
                  pl.BlockSpec((k, tn), lambda j, i: (0, col0 + j))],
        out_specs=pl.BlockSpec((tm, tn), lambda j, i: (i, j)),
        out_shape=jax.ShapeDtypeStruct((m, n_cols), out_dtype),
        scratch_shapes=[pltpu.VMEM((k, tn), BF16)],
        compiler_params=_cparams(("arbitrary", "arbitrary")), name=name,
    )(a, w)


def _matmul_residual(a, w, res, gate, *, tm, name):
    m, k = a.shape
    n = w.shape[1]
    return pl.pallas_call(
        _mm_res_kernel, grid=(m // tm,),
        in_specs=[pl.BlockSpec((tm, k), lambda i: (i, 0)),
                  pl.BlockSpec((k, n), lambda i: (0, 0)),
                  pl.BlockSpec((tm, n), lambda i: (i, 0)),
                  pl.BlockSpec((1, 1, n), lambda i: (_mod_row(i, tm), 0, 0))],
        out_specs=pl.BlockSpec((tm, n), lambda i: (i, 0)),
        out_shape=jax.ShapeDtypeStruct((m, n), F32),
        scratch_shapes=[pltpu.VMEM((k, n), BF16)],
        compiler_params=_cparams(("arbitrary",)), name=name,
    )(a, w, res, gate)


def _qkv_kernel(a_ref, w_ref, gain_ref, seg_ref, segt_ref, o_ref, wb_ref):
    j = pl.program_id(0)
    _cast_weight_once(pl.program_id(1) == 0, w_ref, wb_ref)
    acc = jnp.dot(a_ref[...], wb_ref[...], preferred_element_type=F32)

    @pl.when(j < 2)
    def _():
        ss = _dot_exact_rhs(acc * acc, seg_ref[...], 2)
        inv = lax.rsqrt(ss * (1.0 / NA_HEAD_DIM) + NORM_EPS)
        o_ref[...] = (acc * _dot_exact_rhs(inv, segt_ref[...], 2) * gain_ref[0]).astype(o_ref.dtype)

    @pl.when(j == 2)
    def _():
        o_ref[...] = acc.astype(o_ref.dtype)


def _qkv_proj(h, w_qkv, q_g, k_g, tm=1024):
    m = h.shape[0]
    head_of_col = jnp.arange(D_MODEL) // NA_HEAD_DIM
    seg = (head_of_col[:, None] == jnp.arange(128)[None, :]).astype(BF16)
    gain = jnp.stack([jnp.tile(q_g, NA_HEADS) * NA_HEAD_DIM ** -0.5, jnp.tile(k_g, NA_HEADS),
                      jnp.ones((D_MODEL,), F32)]).reshape(3, 1, D_MODEL)
    return pl.pallas_call(
        _qkv_kernel, grid=(3, m // tm),
        in_specs=[pl.BlockSpec((tm, D_MODEL), lambda j, i: (i, 0)),
                  pl.BlockSpec((D_MODEL, D_MODEL), lambda j, i: (0, j)),
                  pl.BlockSpec((1, 1, D_MODEL), lambda j, i: (j, 0, 0)),
                  pl.BlockSpec((D_MODEL, 128), lambda j, i: (0, 0)),
                  pl.BlockSpec((128, D_MODEL), lambda j, i: (0, 0))],
        out_specs=pl.BlockSpec((tm, D_MODEL), lambda j, i: (i, j)),
        out_shape=jax.ShapeDtypeStruct((m, 3 * D_MODEL), BF16),
        scratch_shapes=[pltpu.VMEM((D_MODEL, D_MODEL), BF16)],
        compiler_params=_cparams(("arbitrary", "arbitrary")), name="qkv_proj",
    )(h, w_qkv, gain, seg, seg.T)


def _na_bias_table(rpb):
    col = jnp.arange(GRID_W)
    col_start = jnp.clip(col - NA_KC // 2, 0, GRID_W - NA_KC)
    col_ok = (col[None, :] >= col_start[:, None]) & (col[None, :] < col_start[:, None] + NA_KC)
    col_idx = jnp.clip(col[None, :] - col[:, None] + NA_KC - 1, 0, 2 * NA_KC - 2)
    by_offset = jnp.take(rpb, col_idx.reshape(-1), axis=2).reshape(NA_HEADS, 2 * NA_KR - 1, GRID_W, GRID_W)
    by_offset = jnp.where(col_ok[None, None], by_offset, NEG_BIG)
    type_blocks = (0, 1, NA_ROWS // NA_QROWS - 1)

    def assemble_kernel(src_ref, o_ref):
        for t, rb in enumerate(type_blocks):
            @pl.when(pl.program_id(0) == t)
            def _(rb=rb):
                r0 = rb * NA_QROWS
                ks = min(max(r0 - NA_KR // 2, 0), NA_ROWS - NA_BAND)
                for i in range(NA_QROWS):
                    r = r0 + i
                    rs = min(max(r - NA_KR // 2, 0), NA_ROWS - NA_KR)
                    for n in range(NA_BAND):
                        kr = ks + n
                        if rs <= kr < rs + NA_KR:
                            tile = src_ref[kr - r + NA_KR - 1]
                        else:
                            tile = jnp.full((GRID_W, GRID_W), NEG_BIG, F32)
                        o_ref[i * GRID_W:(i + 1) * GRID_W, n * GRID_W:(n + 1) * GRID_W] = tile

    return pl.pallas_call(
        assemble_kernel, grid=(len(type_blocks), NA_HEADS),
        in_specs=[pl.BlockSpec((None, 2 * NA_KR - 1, GRID_W, GRID_W), lambda t, h: (h, 0, 0, 0))],
        out_specs=pl.BlockSpec((None, None, NA_QBLK, NA_KBLK), lambda t, h: (t, h, 0, 0)),
        out_shape=jax.ShapeDtypeStruct((len(type_blocks), NA_HEADS, NA_QBLK, NA_KBLK), F32),
        compiler_params=_cparams(("arbitrary", "arbitrary")), name="na_bias_table",
    )(by_offset)


def _softmax_pv(s_list, v_list):
    m = s_list[0].max(axis=-1, keepdims=True)
    for s in s_list[1:]:
        m = jnp.maximum(m, s.max(axis=-1, keepdims=True))
    acc = None
    den = None
    for s, v in zip(s_list, v_list):
        p = jnp.exp(s - m)
        l = p.sum(axis=-1, keepdims=True)
        o = jnp.dot(p.astype(BF16), v, preferred_element_type=F32)
        acc = o if acc is None else acc + o
        den = l if den is None else den + l
    return acc / den


NA_LAT_STEPS = NA_ROWS // NA_QROWS


NA_STEP_HEADS = 4
NA_STEP_LANES = NA_STEP_HEADS * NA_HEAD_DIM


def _na_kernel(q_ref, k_ref, v_ref, kc_ref, vc_ref, tbl_ref, o_ref):
    rb = pl.program_id(2)
    lane = lax.broadcasted_iota(jnp.int32, (NA_QBLK, 2 * NA_HEAD_DIM), 1)

    def pair_lanes(pair):
        return slice(pair * 2 * NA_HEAD_DIM, (pair + 1) * 2 * NA_HEAD_DIM)

    def head_queries(q2, sub):
        sel = (lane < NA_HEAD_DIM) if sub == 0 else (lane >= NA_HEAD_DIM)
        return jnp.where(sel, q2, jnp.zeros_like(q2))

    @pl.when(rb < NA_LAT_STEPS)
    def _():
        ks = jnp.clip(rb * NA_QROWS - NA_KR // 2, 0, NA_ROWS - NA_BAND)
        kstart = pl.multiple_of(ks * GRID_W, GRID_W)
        for pair in range(NA_STEP_HEADS // 2):
            pl_ = pair_lanes(pair)
            q2 = q_ref[:, pl_]
            kb = k_ref[pl.ds(kstart, NA_KBLK), pl_]
            vb = v_ref[pl.ds(kstart, NA_KBLK), pl_]
            kc = kc_ref[:, pl_]
            vc = vc_ref[:, pl_]
            outs = []
            for sub in range(2):
                qm = head_queries(q2, sub)
                s_lat = lax.dot_general(qm, kb, _NT, preferred_element_type=F32) + tbl_ref[0, 2 * pair + sub]
                s_ctx = lax.dot_general(qm, kc, _NT, preferred_element_type=F32)
                outs.append(_softmax_pv([s_lat, s_ctx], [vb, vc]))
            o_ref[:, pl_] = jnp.where(lane < NA_HEAD_DIM, outs[0], outs[1]).astype(o_ref.dtype)

    @pl.when(rb == NA_LAT_STEPS)
    def _():
        for pair in range(NA_STEP_HEADS // 2):
            pl_ = pair_lanes(pair)
            q2 = q_ref[:, pl_]
            kc = kc_ref[:, pl_]
            vc = vc_ref[:, pl_]
            outs = []
            for sub in range(2):
                s_ctx = lax.dot_general(head_queries(q2, sub), kc, _NT, preferred_element_type=F32)
                outs.append(_softmax_pv([s_ctx], [vc]))
            o_ref[:, pl_] = jnp.where(lane < NA_HEAD_DIM, outs[0], outs[1]).astype(o_ref.dtype)


def _na_block_type(rb):
    return jnp.where(rb == 0, 0, jnp.where(rb >= NA_LAT_STEPS - 1, 2, 1))


def _neighbourhood_attention(qkv, table):
    groups = NA_HEADS // NA_STEP_HEADS
    ctx_blk0 = N_LAT // CTX_LEN

    def q_block(b, h, r):
        return (jnp.where(r < NA_LAT_STEPS, b * NA_LAT_STEPS + r, ctx_blk0 + b), h)

    return pl.pallas_call(
        _na_kernel, grid=(BATCH, groups, NA_LAT_STEPS + 1),
        in_specs=[
            pl.BlockSpec((NA_QBLK, NA_STEP_LANES), q_block),
            pl.BlockSpec((SEQ, NA_STEP_LANES), lambda b, h, r: (b, groups + h)),
            pl.BlockSpec((SEQ, NA_STEP_LANES), lambda b, h, r: (b, 2 * groups + h)),
            pl.BlockSpec((CTX_LEN, NA_STEP_LANES), lambda b, h, r: (ctx_blk0 + b, groups + h)),
            pl.BlockSpec((CTX_LEN, NA_STEP_LANES), lambda b, h, r: (ctx_blk0 + b, 2 * groups + h)),
            pl.BlockSpec((1, NA_STEP_HEADS, NA_QBLK, NA_KBLK), lambda b, h, r: (_na_block_type(r), h, 0, 0)),
        ],
        out_specs=pl.BlockSpec((NA_QBLK, NA_STEP_LANES), q_block),
        out_shape=jax.ShapeDtypeStruct((N_ALL, D_MODEL), BF16),
        compiler_params=_cparams(("arbitrary", "arbitrary", "arbitrary")), name="na_attention",
    )(qkv, qkv, qkv, qkv, qkv, table)


CONV_ROWS = 256
CONV_COLS = 2048
CONV_HALO = 8


def _conv_silu_kernel(cur_ref, prev_ref, next_ref, w_ref, b_ref, o_ref):
    i = pl.program_id(0)
    tiles_per_seq = SEQ // CONV_ROWS
    is_ctx = i >= N_LAT // CONV_ROWS
    first = jnp.logical_or(is_ctx, i % tiles_per_seq == 0)
    last = jnp.logical_or(is_ctx, i % tiles_per_seq == tiles_per_seq - 1)
    prev = jnp.where(first, 0.0, prev_ref[...])
    nxt = jnp.where(last, 0.0, next_ref[...])
    cur = cur_ref[...]
    pad = SSD_CONV_W // 2

    def taps(slab, rows):
        acc = jnp.zeros((rows, slab.shape[1]), F32) + b_ref[...]
        for t in range(SSD_CONV_W):
            lo = CONV_HALO - pad + t
            acc = acc + slab[lo:lo + rows, :] * w_ref[t:t + 1, :]
        return acc

    acc = jnp.zeros(o_ref.shape, F32) + b_ref[...]
    for t in range(SSD_CONV_W):
        shifted = cur if t == pad else pltpu.roll(cur, (pad - t) % CONV_ROWS, axis=0)
        acc = acc + shifted * w_ref[t:t + 1, :]
    top = taps(jnp.concatenate([prev, cur[:2 * CONV_HALO]], axis=0), CONV_HALO)
    bottom = taps(jnp.concatenate([cur[CONV_ROWS - 2 * CONV_HALO:], nxt], axis=0), CONV_HALO)
    acc = jnp.concatenate([top, acc[CONV_HALO:CONV_ROWS - CONV_HALO], bottom], axis=0)
    o_ref[...] = acc * jax.nn.sigmoid(acc)


def _conv_silu(zxbc, conv_w, conv_b):
    col0 = SSD_D_INNER // CONV_COLS
    n_halo_blocks = N_ALL // CONV_HALO
    per_tile = CONV_ROWS // CONV_HALO
    return pl.pallas_call(
        _conv_silu_kernel, grid=(N_ALL // CONV_ROWS, SSD_CONV_DIM // CONV_COLS),
        in_specs=[
            pl.BlockSpec((CONV_ROWS, CONV_COLS), lambda i, c: (i, col0 + c)),
            pl.BlockSpec((CONV_HALO, CONV_COLS), lambda i, c: (jnp.maximum(i * per_tile - 1, 0), col0 + c)),
            pl.BlockSpec((CONV_HALO, CONV_COLS),
                         lambda i, c: (jnp.minimum((i + 1) * per_tile, n_halo_blocks - 1), col0 + c)),
            pl.BlockSpec((SSD_CONV_W, CONV_COLS), lambda i, c: (0, c)),
            pl.BlockSpec((1, CONV_COLS), lambda i, c: (0, c)),
        ],
        out_specs=pl.BlockSpec((CONV_ROWS, CONV_COLS), lambda i, c: (i, c)),
        out_shape=jax.ShapeDtypeStruct((N_ALL, SSD_CONV_DIM), F32),
        compiler_params=_cparams(("arbitrary", "arbitrary")), name="ssd_conv_silu",
    )(zxbc, zxbc, zxbc, conv_w, conv_b.reshape(1, SSD_CONV_DIM))


SSD_STEPS = (CTX_LEN + SEQ) // SSD_CHUNK
SSD_CTX_CHUNKS = CTX_LEN // SSD_CHUNK
SSD_LAT_CHUNKS = SEQ // SSD_CHUNK


def _ssd_chunk(b, d, j):
    jj = jnp.where(d == 0, j, jnp.where(j < SSD_CTX_CHUNKS, SSD_CTX_CHUNKS - 1 - j,
                                        SSD_STEPS + SSD_CTX_CHUNKS - 1 - j))
    return jnp.where(jj < SSD_CTX_CHUNKS,
                     N_LAT // SSD_CHUNK + SSD_CTX_CHUNKS * b + jj,
                     SSD_LAT_CHUNKS * b + jj - SSD_CTX_CHUNKS)


def _ssd_kernel(xs_ref, b_ref, c_ref, dtr_ref, bias_ref, alog_ref, expand_ref, y_ref, state_ref):
    d = pl.program_id(1)
    j = pl.program_id(2)

    @pl.when(j == 0)
    def _():
        state_ref[...] = jnp.zeros_like(state_ref)

    fwd = d == 0
    dtr = dtr_ref[...]
    dt_pre = jnp.where(fwd, dtr[:, :SSD_HEADS], dtr[:, SSD_HEADS:]) + bias_ref[...]
    dt = jnp.maximum(dt_pre, 0.0) + jnp.log1p(jnp.exp(-jnp.abs(dt_pre)))
    da = dt * (-jnp.exp(alog_ref[...]))
    li = lax.broadcasted_iota(jnp.int32, (SSD_CHUNK, SSD_CHUNK), 0)
    si = lax.broadcasted_iota(jnp.int32, (SSD_CHUNK, SSD_CHUNK), 1)
    before = jnp.where(fwd, li - si, si - li) >= 0
    before_t = jnp.where(fwd, si - li, li - si) >= 0
    cum = _dot_exact_lhs(jnp.where(before, 1.0, 0.0).astype(BF16), da, 3)
    cum_t = _dot_exact_rhs(da, jnp.where(before_t, 1.0, 0.0).astype(BF16), 3, _TN)
    dt_t = _dot_exact_rhs(dt, jnp.where(li == si, 1.0, 0.0).astype(BF16), 2, _TN)
    tot = jnp.where(fwd, cum[SSD_CHUNK - 1:SSD_CHUNK, :], cum[0:1, :])
    expand = expand_ref[...]
    to_end = _dot_exact_rhs(dt * jnp.exp(tot - cum), expand, 2)
    from_start = _dot_exact_rhs(jnp.exp(cum), expand, 2)
    carry = jnp.where(fwd, from_start[SSD_CHUNK - 1:SSD_CHUNK, :], from_start[0:1, :])
    xs = xs_ref[...]
    x16 = xs.astype(BF16)
    xdec = (xs * to_end).astype(BF16)
    lane = lax.broadcasted_iota(jnp.int32, (SSD_CHUNK, 2 * SSD_HEAD_DIM), 1)
    for g in range(SSD_GROUPS):
        gs = slice(g * SSD_GROUP_W, (g + 1) * SSD_GROUP_W)
        bg = b_ref[:, g * SSD_STATE:(g + 1) * SSD_STATE].astype(BF16)
        cg = c_ref[:, g * SSD_STATE:(g + 1) * SSD_STATE].astype(BF16)
        cb = lax.dot_general(cg, bg, _NT, preferred_element_type=F32)
        st = state_ref[g]
        y_off = jnp.dot(cg, st.astype(BF16), preferred_element_type=F32) * from_start[:, gs]
        y_pairs = []
        for pr in range(SSD_HPG // 2):
            ms = []
            for r2 in range(2):
                h = g * SSD_HPG + pr * 2 + r2
                decay = jnp.exp(jnp.where(before, cum[:, h:h + 1] - cum_t[h:h + 1, :], NEG_BIG))
                ms.append((cb * decay * dt_t[h:h + 1, :]).astype(BF16))
            m2 = jnp.concatenate(ms, axis=1)
            c0 = g * SSD_GROUP_W + pr * 2 * SSD_HEAD_DIM
            slab = x16[:, c0:c0 + 2 * SSD_HEAD_DIM]
            zero = jnp.zeros_like(slab)
            xdiag = jnp.concatenate([jnp.where(lane < SSD_HEAD_DIM, slab, zero),
                                     jnp.where(lane >= SSD_HEAD_DIM, slab, zero)], axis=0)
            y_pairs.append(jnp.dot(m2, xdiag, preferred_element_type=F32))
        y_ref[:, gs] = jnp.concatenate(y_pairs, axis=1) + y_off
        s_new = lax.dot_general(bg, xdec[:, gs], _TN, preferred_element_type=F32)
        state_ref[g] = st * carry[:, gs] + s_new


def _ssd_scan(xbc, dt_raw, dt_bias, a_log):
    expand = (jnp.arange(SSD_HEADS)[:, None] == (jnp.arange(SSD_D_INNER) // SSD_HEAD_DIM)[None, :]).astype(BF16)
    bc_w = SSD_GROUPS * SSD_STATE
    b_col = SSD_D_INNER // bc_w
    return pl.pallas_call(
        _ssd_kernel, grid=(BATCH, 2, SSD_STEPS),
        in_specs=[
            pl.BlockSpec((SSD_CHUNK, SSD_D_INNER), lambda b, d, j: (_ssd_chunk(b, d, j), 0)),
            pl.BlockSpec((SSD_CHUNK, bc_w), lambda b, d, j: (_ssd_chunk(b, d, j), b_col)),
            pl.BlockSpec((SSD_CHUNK, bc_w), lambda b, d, j: (_ssd_chunk(b, d, j), b_col + 1)),
            pl.BlockSpec((SSD_CHUNK, 2 * SSD_HEADS), lambda b, d, j: (_ssd_chunk(b, d, j), 0)),
            pl.BlockSpec((None, 1, SSD_HEADS), lambda b, d, j: (d, 0, 0)),
            pl.BlockSpec((None, 1, SSD_HEADS), lambda b, d, j: (d, 0, 0)),
            pl.BlockSpec((SSD_HEADS, SSD_D_INNER), lambda b, d, j: (0, 0)),
        ],
        out_specs=pl.BlockSpec((None, SSD_CHUNK, SSD_D_INNER), lambda b, d, j: (d, _ssd_chunk(b, d, j), 0)),
        out_shape=jax.ShapeDtypeStruct((2, N_ALL, SSD_D_INNER), F32),
        scratch_shapes=[pltpu.VMEM((SSD_GROUPS, SSD_STATE, SSD_GROUP_W), F32)],
        compiler_params=_cparams(("arbitrary", "arbitrary", "arbitrary")), name="ssd_scan",
    )(xbc, xbc, xbc, dt_raw, dt_bias.reshape(2, 1, SSD_HEADS), a_log.reshape(2, 1, SSD_HEADS), expand)


def _ssd_out_kernel(yf_ref, yb_ref, xs_ref, z_ref, dskip_ref, g_ref, w_ref, res_ref, gate_ref, o_ref,
                    wb_ref, gated_ref):
    _cast_weight_once(pl.program_id(0) == 0, w_ref, wb_ref)
    z = z_ref[...]
    y = (yf_ref[...] + yb_ref[...] + dskip_ref[...] * xs_ref[...]) * (z * jax.nn.sigmoid(z))
    for grp in range(SSD_GROUPS):
        gs = slice(grp * SSD_GROUP_W, (grp + 1) * SSD_GROUP_W)
        yg = y[:, gs]
        ms = jnp.mean(yg * yg, axis=-1, keepdims=True)
        gated_ref[:, gs] = (yg * lax.rsqrt(ms + NORM_EPS) * g_ref[:, gs]).astype(BF16)
    acc = jnp.dot(gated_ref[...], wb_ref[...], preferred_element_type=F32)
    o_ref[...] = res_ref[...] + gate_ref[0] * acc


def _ssd_out_proj(y2, xbc, zxbc, d_skip, norm_g, w_out, res, gate, n_out, tm=256):
    d_cols = jnp.repeat(d_skip, SSD_HEAD_DIM).reshape(1, SSD_D_INNER)
    return pl.pallas_call(
        _ssd_out_kernel, grid=(n_out // tm,),
        in_specs=[
            pl.BlockSpec((None, tm, SSD_D_INNER), lambda i: (0, i, 0)),
            pl.BlockSpec((None, tm, SSD_D_INNER), lambda i: (1, i, 0)),
            pl.BlockSpec((tm, SSD_D_INNER), lambda i: (i, 0)),
            pl.BlockSpec((tm, SSD_D_INNER), lambda i: (i, 0)),
            pl.BlockSpec((1, SSD_D_INNER), lambda i: (0, 0)),
            pl.BlockSpec((1, SSD_D_INNER), lambda i: (0, 0)),
            pl.BlockSpec((SSD_D_INNER, D_MODEL), lambda i: (0, 0)),
            pl.BlockSpec((tm, D_MODEL), lambda i: (i, 0)),
            pl.BlockSpec((1, 1, D_MODEL), lambda i: (_mod_row(i, tm), 0, 0)),
        ],
        out_specs=pl.BlockSpec((tm, D_MODEL), lambda i: (i, 0)),
        out_shape=jax.ShapeDtypeStruct((n_out, D_MODEL), F32),
        scratch_shapes=[pltpu.VMEM((SSD_D_INNER, D_MODEL), BF16), pltpu.VMEM((tm, SSD_D_INNER), BF16)],
        compiler_params=_cparams(("arbitrary",)), name="ssd_out_proj",
    )(y2, y2, xbc, zxbc, d_cols, norm_g.reshape(1, SSD_D_INNER), w_out, res, gate)


DISPATCH_TOKENS = 1024
COMBINE_TOKENS = 256
ZERO_FILL_ROWS = MOE_BLOCK + 8


def _dispatch_kernel(dest_ref, padstart_ref, nreal_ref, h_ref, xb_ref, zeros_ref, sem):
    i = pl.program_id(0)
    base = i * (DISPATCH_TOKENS * TOP_K)

    @pl.when(i == 0)
    def _():
        zeros_ref[...] = jnp.zeros_like(zeros_ref)
        n_slots = xb_ref.shape[0]

        def fill(e):
            start = jnp.minimum(padstart_ref[e] // 8 * 8, n_slots - ZERO_FILL_ROWS)
            return pltpu.make_async_copy(zeros_ref, xb_ref.at[pl.ds(pl.multiple_of(start, 8), ZERO_FILL_ROWS)],
                                         sem.at[0])

        for e in range(N_EXPERTS):
            fill(e).start()
        for e in range(N_EXPERTS):
            fill(e).wait()

        def fill_block(blk):
            return pltpu.make_async_copy(zeros_ref.at[pl.ds(0, MOE_BLOCK)],
                                         xb_ref.at[pl.ds(pl.multiple_of(blk * MOE_BLOCK, MOE_BLOCK), MOE_BLOCK)],
                                         sem.at[0])

        def start_block(blk, carry):
            fill_block(blk).start()
            return carry

        def wait_block(blk, carry):
            fill_block(blk).wait()
            return carry

        lax.fori_loop(nreal_ref[0], n_slots // MOE_BLOCK, start_block, 0)
        lax.fori_loop(nreal_ref[0], n_slots // MOE_BLOCK, wait_block, 0)

    def body(t, carry):
        for k in range(TOP_K):
            pltpu.make_async_copy(h_ref.at[pl.ds(t, 1)], xb_ref.at[pl.ds(dest_ref[base + t * TOP_K + k], 1)],
                                  sem.at[0]).start()
        return carry

    lax.fori_loop(0, DISPATCH_TOKENS, body, 0, unroll=4)
    for k in range(TOP_K):
        pltpu.make_async_copy(h_ref, xb_ref.at[pl.ds(0, DISPATCH_TOKENS)], sem.at[0]).wait()


def _moe_dispatch(h, dest, pad_start, n_real, n_slots):
    n_tok = h.shape[0]
    tt = DISPATCH_TOKENS
    grid_spec = pltpu.PrefetchScalarGridSpec(
        num_scalar_prefetch=3, grid=(n_tok // tt,),
        in_specs=[pl.BlockSpec((tt, D_MODEL), lambda i, d, p, nr: (i, 0))],
        out_specs=pl.BlockSpec(memory_space=pl.ANY),
        scratch_shapes=[pltpu.VMEM((ZERO_FILL_ROWS, D_MODEL), F32), pltpu.SemaphoreType.DMA((1,))],
    )
    return pl.pallas_call(
        _dispatch_kernel, grid_spec=grid_spec,
        out_shape=jax.ShapeDtypeStruct((n_slots, D_MODEL), F32),
        compiler_params=_cparams(("arbitrary",)), name="moe_dispatch",
    )(dest, pad_start, n_real, h)


def _moe_kernel(be_ref, nreal_ref, x_ref, w1_ref, b1_ref, w2_ref, b2_ref, o_ref, w1b_ref, w2b_ref):
    i = pl.program_id(0)

    @pl.when(i < nreal_ref[0])
    def _():
        new_expert = jnp.logical_or(i == 0, be_ref[i] != be_ref[jnp.maximum(i - 1, 0)])
        _cast_weight_once(new_expert, w1_ref, w1b_ref)
        _cast_weight_once(new_expert, w2_ref, w2b_ref)
        h = jnp.dot(x_ref[...].astype(BF16), w1b_ref[...], preferred_element_type=F32) + b1_ref[0]
        glu = jnp.minimum(h[:, :D_FF], SWIGLU_LIMIT)
        lin = jnp.clip(h[:, D_FF:], -SWIGLU_LIMIT, SWIGLU_LIMIT)
        act = glu * jax.nn.sigmoid(SWIGLU_ALPHA * glu) * (lin + 1.0)
        o_ref[...] = jnp.dot(act.astype(BF16), w2b_ref[...], preferred_element_type=F32) + b2_ref[0]

    @pl.when(i >= nreal_ref[0])
    def _():
        o_ref[...] = jnp.zeros_like(o_ref)


def _moe_experts(xb, block_expert, n_real, layer, w1, b1, w2, b2):
    n_slots = xb.shape[0]
    n_blocks = n_slots // MOE_BLOCK
    grid_spec = pltpu.PrefetchScalarGridSpec(
        num_scalar_prefetch=2, grid=(n_blocks,),
        in_specs=[
            pl.BlockSpec((MOE_BLOCK, D_MODEL), lambda i, be, nr: (jnp.minimum(i, nr[0] - 1), 0)),
            pl.BlockSpec((None, None, D_MODEL, 2 * D_FF), lambda i, be, nr: (layer, be[i], 0, 0)),
            pl.BlockSpec((1, 1, 2 * D_FF), lambda i, be, nr: (be[i], 0, 0)),
            pl.BlockSpec((None, None, D_FF, D_MODEL), lambda i, be, nr: (layer, be[i], 0, 0)),
            pl.BlockSpec((1, 1, D_MODEL), lambda i, be, nr: (be[i], 0, 0)),
        ],
        out_specs=pl.BlockSpec((MOE_BLOCK, D_MODEL), lambda i, be, nr: (i, 0)),
        scratch_shapes=[pltpu.VMEM((D_MODEL, 2 * D_FF), BF16), pltpu.VMEM((D_FF, D_MODEL), BF16)],
    )
    return pl.pallas_call(
        _moe_kernel, grid_spec=grid_spec,
        out_shape=jax.ShapeDtypeStruct((n_slots, D_MODEL), F32),
        compiler_params=_cparams(("arbitrary",)), name="moe_experts",
    )(block_expert, n_real, xb, w1, b1.reshape(N_EXPERTS, 1, 2 * D_FF), w2, b2.reshape(N_EXPERTS, 1, D_MODEL))


def _combine_kernel(dest_ref, y_hbm, route_ref, res_ref, gate_ref, o_ref, buf, sem):
    i = pl.program_id(0)
    n = pl.num_programs(0)

    def issue(tile, slot):
        base = tile * (COMBINE_TOKENS * TOP_K)

        def body(t, carry):
            for k in range(TOP_K):
                pltpu.make_async_copy(y_hbm.at[pl.ds(dest_ref[base + t * TOP_K + k], 1)],
                                      buf.at[slot, k, pl.ds(t, 1)], sem.at[slot]).start()
            return carry

        lax.fori_loop(0, COMBINE_TOKENS, body, 0, unroll=4)

    @pl.when(i == 0)
    def _():
        issue(0, 0)

    @pl.when(i + 1 < n)
    def _():
        issue(i + 1, (i + 1) % 2)

    slot = i % 2
    for k in range(TOP_K):
        pltpu.make_async_copy(y_hbm.at[pl.ds(0, COMBINE_TOKENS)], buf.at[slot, k], sem.at[slot]).wait()
    route = route_ref[...]
    y = None
    for k in range(TOP_K):
        t = route[:, ROUTE_GATE_LANE + k:ROUTE_GATE_LANE + k + 1] * buf[slot, k]
        y = t if y is None else y + t
    o_ref[...] = res_ref[...] + gate_ref[0] * y


def _moe_combine(yb, dest, route, res, gate):
    n_tok = route.shape[0]
    tt = COMBINE_TOKENS
    grid_spec = pltpu.PrefetchScalarGridSpec(
        num_scalar_prefetch=1, grid=(n_tok // tt,),
        in_specs=[
            pl.BlockSpec(memory_space=pl.ANY),
            pl.BlockSpec((tt, 128), lambda i, d: (i, 0)),
            pl.BlockSpec((tt, D_MODEL), lambda i, d: (i, 0)),
            pl.BlockSpec((1, 1, D_MODEL), lambda i, d: (_mod_row(i, tt), 0, 0)),
        ],
        out_specs=pl.BlockSpec((tt, D_MODEL), lambda i, d: (i, 0)),
        scratch_shapes=[pltpu.VMEM((2, TOP_K, tt, D_MODEL), F32), pltpu.SemaphoreType.DMA((2,))],
    )
    return pl.pallas_call(
        _combine_kernel, grid_spec=grid_spec,
        out_shape=jax.ShapeDtypeStruct((n_tok, D_MODEL), F32),
        compiler_params=_cparams(("arbitrary",)), name="moe_combine",
    )(dest, yb, route, res, gate)


def _moe(h, route, counts, res, gate, layer, w1, b1, w2, b2):
    n_tok = h.shape[0]
    nk = n_tok * TOP_K
    n_blocks = -(-(nk + N_EXPERTS * (MOE_BLOCK - 1)) // MOE_BLOCK)
    n_slots = n_blocks * MOE_BLOCK
    expert = route[:, ROUTE_EXPERT_LANE:ROUTE_EXPERT_LANE + TOP_K].astype(jnp.int32)
    rank = route[:, ROUTE_RANK_LANE:ROUTE_RANK_LANE + TOP_K].astype(jnp.int32)
    counts = counts.reshape(N_EXPERTS).astype(jnp.int32)
    padded = (counts + MOE_BLOCK - 1) // MOE_BLOCK * MOE_BLOCK
    pend = jnp.cumsum(padded)
    pstart = pend - padded
    ids = jnp.arange(N_EXPERTS, dtype=jnp.int32)
    dest = rank + jnp.sum(jnp.where(expert[..., None] == ids, pstart, 0), axis=-1)
    dest = dest.reshape(nk).astype(jnp.int32)
    block_start = jnp.arange(n_blocks, dtype=jnp.int32) * MOE_BLOCK
    block_expert = jnp.minimum(jnp.sum((pend[None, :] <= block_start[:, None]).astype(jnp.int32), axis=1),
                               N_EXPERTS - 1).astype(jnp.int32)
    n_real = (pend[-1] // MOE_BLOCK).astype(jnp.int32).reshape(1)
    xb = _moe_dispatch(h, dest, (pstart + counts).astype(jnp.int32), n_real, n_slots)
    yb = _moe_experts(xb, block_expert, n_real, layer, w1, b1, w2, b2)
    return _moe_combine(yb, dest, route, res, gate)


def kernel(x, c, ctx, c_ctx, ada_w, ada_b, norm1_g, norm2_g, na_w_qkv, na_q_g, na_k_g, na_rpb, na_w_o,
           ssd_w_in, ssd_conv_w, ssd_conv_b, ssd_dt_bias, ssd_a_log, ssd_d, ssd_norm_g, ssd_w_out,
           moe_w_router, moe_b_router, moe_w1, moe_b1, moe_w2, moe_b2):
    xa = jnp.concatenate([x.reshape(N_LAT, D_MODEL), ctx.reshape(N_CTX, D_MODEL)], axis=0)
    cond = jnp.concatenate([jax.nn.silu(c), jax.nn.silu(c_ctx)[None, :],
                            jnp.zeros((16 - BATCH - 1, D_MODEL), F32)], axis=0)
    for i in range(DEPTH):
        last = i == DEPTH - 1
        mod = _matmul_bias_f32(cond, ada_w[i], ada_b[i], tn=D_MODEL, name="ada_mod")
        mt = [mod[:, t * D_MODEL:(t + 1) * D_MODEL].reshape(16, 1, D_MODEL) for t in range(6)]
        h1 = _normmod(xa, norm1_g[i], mt[0], mt[1], N_ALL)
        n_out = N_LAT if last else N_ALL
        j = i // 2
        if i % 2 == 0:
            qkv = _qkv_proj(h1, na_w_qkv[j], na_q_g[j], na_k_g[j])
            o_all = _neighbourhood_attention(qkv, _na_bias_table(na_rpb[j]))
            xa_new = _matmul_residual(o_all[:n_out], na_w_o[j], xa, mt[2], tm=512, name="na_out_proj")
        else:
            w_in = ssd_w_in[j]
            zxbc = _matmul(h1, w_in, tm=1024, tn=1024, n_cols=SSD_MAIN_W, name="ssd_in_proj")
            dt_raw = _matmul(h1, w_in[:, SSD_MAIN_W:], tm=512, tn=2 * SSD_HEADS, n_cols=2 * SSD_HEADS,
                             name="ssd_dt_proj")
            xbc = _conv_silu(zxbc, ssd_conv_w[j], ssd_conv_b[j])
            y2 = _ssd_scan(xbc, dt_raw, ssd_dt_bias[j], ssd_a_log[j])
            xa_new = _ssd_out_proj(y2, xbc, zxbc, ssd_d[j], ssd_norm_g[j], ssd_w_out[j], xa, mt[2], n_out)
        n_tok = xa_new.shape[0]
        h2, route, counts = _normmod(xa_new, norm2_g[i], mt[3], mt[4], n_tok,
                                     router=(moe_w_router[i], moe_b_router[i]))
        xa = _moe(h2, route, counts, xa_new, mt[5], i, moe_w1, moe_b1[i], moe_w2, moe_b2[i])
    return xa[:N_LAT].reshape(BATCH, SEQ, D_MODEL)
```

```python
import jax
import jax.numpy as jnp
from jax import lax
from jax.experimental import pallas as pl
from jax.experimental.pallas import tpu as pltpu

F32 = jnp.float32
BF16 = jnp.bfloat16

D_MODEL = 1024
BATCH = 8
SEQ = 2048
DEPTH = 2
GRID_W = 64
CTX_LEN = 256
N_LAT = BATCH * SEQ
N_CTX = BATCH * CTX_LEN
N_ALL = N_LAT + N_CTX

NA_HEADS = 16
NA_HEAD_DIM = 64
NA_KR = 8
NA_KC = 16
NA_ROWS = SEQ // GRID_W
NA_QROWS = 4
NA_BAND = 12
NA_QBLK = NA_QROWS * GRID_W
NA_KBLK = NA_BAND * GRID_W

SSD_D_INNER = 2048
SSD_HEAD_DIM = 64
SSD_HEADS = 32
SSD_GROUPS = 8
SSD_HPG = 4
SSD_STATE = 128
SSD_CONV_W = 5
SSD_CHUNK = 128
SSD_CONV_DIM = SSD_D_INNER + 2 * SSD_GROUPS * SSD_STATE
SSD_MAIN_W = SSD_D_INNER + SSD_CONV_DIM
SSD_GROUP_W = SSD_HPG * SSD_HEAD_DIM

N_EXPERTS = 32
TOP_K = 4
D_FF = 1024
SWIGLU_ALPHA = 1.702
SWIGLU_LIMIT = 7.0
MOE_BLOCK = 512
NORM_EPS = 1e-6
NEG_BIG = -1e30

VMEM_LIMIT = 56 * 1024 * 1024

_NT = (((1,), (1,)), ((), ()))
_TN = (((0,), (0,)), ((), ()))


def _cparams(sem):
    return pltpu.CompilerParams(dimension_semantics=sem, vmem_limit_bytes=VMEM_LIMIT)


def _mod_row(i, tm):
    return jnp.minimum((i * tm) // SEQ, BATCH)


def _split_bf16(x, pieces):
    out = []
    rem = x
    for _ in range(pieces):
        p = rem.astype(BF16)
        out.append(p)
        rem = rem - p.astype(F32)
    return out


def _dot_exact_rhs(x, sel, pieces, dims=None):
    acc = None
    for p in _split_bf16(x, pieces):
        if dims is None:
            t = jnp.dot(p, sel, preferred_element_type=F32)
        else:
            t = lax.dot_general(p, sel, dims, preferred_element_type=F32)
        acc = t if acc is None else acc + t
    return acc


def _dot_exact_lhs(sel, x, pieces):
    acc = None
    for p in _split_bf16(x, pieces):
        t = jnp.dot(sel, p, preferred_element_type=F32)
        acc = t if acc is None else acc + t
    return acc


def _normmod_kernel(x_ref, g_ref, sh_ref, sc_ref, o_ref):
    x = x_ref[...]
    ms = jnp.mean(x * x, axis=-1, keepdims=True)
    y = x * lax.rsqrt(ms + NORM_EPS) * g_ref[...]
    o_ref[...] = (y * (1.0 + sc_ref[0]) + sh_ref[0]).astype(o_ref.dtype)


ROUTE_EXPERT_LANE = 0
ROUTE_RANK_LANE = 4
ROUTE_GATE_LANE = 8


def _normmod_router_kernel(x_ref, g_ref, sh_ref, sc_ref, wr_ref, br_ref, o_ref, route_ref, count_ref):
    i = pl.program_id(0)
    x = x_ref[...]
    tm = x.shape[0]
    ms = jnp.mean(x * x, axis=-1, keepdims=True)
    y = x * lax.rsqrt(ms + NORM_EPS) * g_ref[...]
    h = y * (1.0 + sc_ref[0]) + sh_ref[0]
    o_ref[...] = h.astype(o_ref.dtype)
    h_hi, h_lo = _split_bf16(h, 2)
    w_hi, w_lo = _split_bf16(wr_ref[...], 2)
    logits = (jnp.dot(h_hi, w_hi, preferred_element_type=F32) + jnp.dot(h_hi, w_lo, preferred_element_type=F32)
              + jnp.dot(h_lo, w_hi, preferred_element_type=F32)) + br_ref[...]

    @pl.when(i == 0)
    def _():
        count_ref[...] = jnp.zeros_like(count_ref)

    lane_e = lax.broadcasted_iota(jnp.int32, logits.shape, 1).astype(F32)
    cur = logits
    vals, ids, hots = [], [], []
    for _ in range(TOP_K):
        m = jnp.max(cur, axis=-1, keepdims=True)
        idx = jnp.min(jnp.where(cur == m, lane_e, float(N_EXPERTS)), axis=-1, keepdims=True)
        hot = lane_e == idx
        vals.append(m)
        ids.append(idx)
        hots.append(hot)
        cur = jnp.where(hot, -jnp.inf, cur)
    exps = [jnp.exp(v - vals[0]) for v in vals]
    den = exps[0] + exps[1] + exps[2] + exps[3]
    picked = jnp.where(hots[0] | hots[1] | hots[2] | hots[3], 1.0, 0.0)
    row = lax.broadcasted_iota(jnp.int32, (tm, tm), 0)
    col = lax.broadcasted_iota(jnp.int32, (tm, tm), 1)
    earlier = jnp.where(col < row, 1.0, 0.0).astype(BF16)
    before = count_ref[...] + jnp.dot(earlier, picked.astype(BF16), preferred_element_type=F32)
    lane = lax.broadcasted_iota(jnp.int32, (tm, 128), 1)
    route = jnp.zeros((tm, 128), F32)
    for k in range(TOP_K):
        rank = jnp.sum(jnp.where(hots[k], before, 0.0), axis=-1, keepdims=True)
        route = jnp.where(lane == ROUTE_EXPERT_LANE + k, ids[k], route)
        route = jnp.where(lane == ROUTE_RANK_LANE + k, rank, route)
        route = jnp.where(lane == ROUTE_GATE_LANE + k, exps[k] / den, route)
    route_ref[...] = route
    count_ref[...] += jnp.sum(picked, axis=0, keepdims=True)


def _normmod(x, g, shift, scale, n_rows, router=None, tm=512):
    grid = (n_rows // tm,)
    x_spec = pl.BlockSpec((tm, D_MODEL), lambda i: (i, 0))
    g_spec = pl.BlockSpec((1, D_MODEL), lambda i: (0, 0))
    m_spec = pl.BlockSpec((1, 1, D_MODEL), lambda i: (_mod_row(i, tm), 0, 0))
    o_spec = pl.BlockSpec((tm, D_MODEL), lambda i: (i, 0))
    g2 = g.reshape(1, D_MODEL)
    if router is None:
        return pl.pallas_call(
            _normmod_kernel, grid=grid,
            in_specs=[x_spec, g_spec, m_spec, m_spec], out_specs=o_spec,
            out_shape=jax.ShapeDtypeStruct((n_rows, D_MODEL), BF16),
            compiler_params=_cparams(("arbitrary",)), name="normmod",
        )(x, g2, shift, scale)
    w_r, b_r = router
    return pl.pallas_call(
        _normmod_router_kernel, grid=grid,
        in_specs=[x_spec, g_spec, m_spec, m_spec,
                  pl.BlockSpec((D_MODEL, N_EXPERTS), lambda i: (0, 0)),
                  pl.BlockSpec((1, N_EXPERTS), lambda i: (0, 0))],
        out_specs=[o_spec, pl.BlockSpec((tm, 128), lambda i: (i, 0)),
                   pl.BlockSpec((1, N_EXPERTS), lambda i: (0, 0))],
        out_shape=[jax.ShapeDtypeStruct((n_rows, D_MODEL), F32),
                   jax.ShapeDtypeStruct((n_rows, 128), F32),
                   jax.ShapeDtypeStruct((1, N_EXPERTS), F32)],
        compiler_params=_cparams(("arbitrary",)), name="normmod_router",
    )(x, g2, shift, scale, w_r, b_r.reshape(1, N_EXPERTS))


def _mm_bias_f32_kernel(a_ref, w_ref, b_ref, o_ref):
    o_ref[...] = jnp.dot(a_ref[...], w_ref[...], preferred_element_type=F32,
                         precision=lax.Precision.HIGHEST) + b_ref[...]


def _matmul_bias_f32(a, w, bias, *, tn, name):
    m, k = a.shape
    n = w.shape[1]
    return pl.pallas_call(
        _mm_bias_f32_kernel, grid=(n // tn,),
        in_specs=[pl.BlockSpec((m, k), lambda j: (0, 0)),
                  pl.BlockSpec((k, tn), lambda j: (0, j)),
                  pl.BlockSpec((1, tn), lambda j: (0, j))],
        out_specs=pl.BlockSpec((m, tn), lambda j: (0, j)),
        out_shape=jax.ShapeDtypeStruct((m, n), F32),
        compiler_params=_cparams(("arbitrary",)), name=name,
    )(a, w, bias.reshape(1, n))


def _cast_weight_once(first, w_ref, wb_ref):
    @pl.when(first)
    def _():
        wb_ref[...] = w_ref[...].astype(BF16)


def _mm_kernel(a_ref, w_ref, o_ref, wb_ref):
    _cast_weight_once(pl.program_id(1) == 0, w_ref, wb_ref)
    o_ref[...] = jnp.dot(a_ref[...], wb_ref[...], preferred_element_type=F32).astype(o_ref.dtype)


def _mm_res_kernel(a_ref, w_ref, res_ref, gate_ref, o_ref, wb_ref):
    _cast_weight_once(pl.program_id(0) == 0, w_ref, wb_ref)
    acc = jnp.dot(a_ref[...], wb_ref[...], preferred_element_type=F32)
    o_ref[...] = res_ref[...] + gate_ref[0] * acc


def _matmul(a, w, *, tm, tn, n_cols, col0=0, out_dtype=F32, name="mm"):
    m, k = a.shape
    return pl.pallas_call(
        _mm_kernel, grid=(n_cols // tn, m // tm),
        in_specs=[pl.BlockSpec((tm, k), lambda j, i: (i, 0)),
                  pl.BlockSpec((k, tn), lambda j, i: (0, col0 + j))],
        out_specs=pl.BlockSpec((tm, tn), lambda j, i: (i, j)),
        out_shape=jax.ShapeDtypeStruct((m, n_cols), out_dtype),
        scratch_shapes=[pltpu.VMEM((k, tn), BF16)],
        compiler_params=_cparams(("arbitrary", "arbitrary")), name=name,
    )(a, w)


def _matmul_residual(a, w, res, gate, *, tm, name):
    m, k = a.shape
    n = w.shape[1]
    return pl.pallas_call(
        _mm_res_kernel, grid=(m // tm,),
        in_specs=[pl.BlockSpec((tm, k), lambda i: (i, 0)),
                  pl.BlockSpec((k, n), lambda i: (0, 0)),
                  pl.BlockSpec((tm, n), lambda i: (i, 0)),
                  pl.BlockSpec((1, 1, n), lambda i: (_mod_row(i, tm), 0, 0))],
        out_specs=pl.BlockSpec((tm, n), lambda i: (i, 0)),
        out_shape=jax.ShapeDtypeStruct((m, n), F32),
        scratch_shapes=[pltpu.VMEM((k, n), BF16)],
        compiler_params=_cparams(("arbitrary",)), name=name,
    )(a, w, res, gate)


def _qkv_kernel(a_ref, w_ref, gain_ref, seg_ref, segt_ref, o_ref, wb_ref):
    j = pl.program_id(0)
    _cast_weight_once(pl.program_id(1) == 0, w_ref, wb_ref)
    acc = jnp.dot(a_ref[...], wb_ref[...], preferred_element_type=F32)

    @pl.when(j < 2)
    def _():
        ss = _dot_exact_rhs(acc * acc, seg_ref[...], 2)
        inv = lax.rsqrt(ss * (1.0 / NA_HEAD_DIM) + NORM_EPS)
        o_ref[...] = (acc * _dot_exact_rhs(inv, segt_ref[...], 2) * gain_ref[0]).astype(o_ref.dtype)

    @pl.when(j == 2)
    def _():
        o_ref[...] = acc.astype(o_ref.dtype)


def _qkv_proj(h, w_qkv, q_g, k_g, tm=1024):
    m = h.shape[0]
    head_of_col = jnp.arange(D_MODEL) // NA_HEAD_DIM
    seg = (head_of_col[:, None] == jnp.arange(128)[None, :]).astype(BF16)
    gain = jnp.stack([jnp.tile(q_g, NA_HEADS) * NA_HEAD_DIM ** -0.5, jnp.tile(k_g, NA_HEADS),
                      jnp.ones((D_MODEL,), F32)]).reshape(3, 1, D_MODEL)
    return pl.pallas_call(
        _qkv_kernel, grid=(3, m // tm),
        in_specs=[pl.BlockSpec((tm, D_MODEL), lambda j, i: (i, 0)),
                  pl.BlockSpec((D_MODEL, D_MODEL), lambda j, i: (0, j)),
                  pl.BlockSpec((1, 1, D_MODEL), lambda j, i: (j, 0, 0)),
                  pl.BlockSpec((D_MODEL, 128), lambda j, i: (0, 0)),
                  pl.BlockSpec((128, D_MODEL), lambda j, i: (0, 0))],
        out_specs=pl.BlockSpec((tm, D_MODEL), lambda j, i: (i, j)),
        out_shape=jax.ShapeDtypeStruct((m, 3 * D_MODEL), BF16),
        scratch_shapes=[pltpu.VMEM((D_MODEL, D_MODEL), BF16)],
        compiler_params=_cparams(("arbitrary", "arbitrary")), name="qkv_proj",
    )(h, w_qkv, gain, seg, seg.T)


def _na_bias_table(rpb):
    col = jnp.arange(GRID_W)
    col_start = jnp.clip(col - NA_KC // 2, 0, GRID_W - NA_KC)
    col_ok = (col[None, :] >= col_start[:, None]) & (col[None, :] < col_start[:, None] + NA_KC)
    col_idx = jnp.clip(col[None, :] - col[:, None] + NA_KC - 1, 0, 2 * NA_KC - 2)
    by_offset = jnp.take(rpb, col_idx.reshape(-1), axis=2).reshape(NA_HEADS, 2 * NA_KR - 1, GRID_W, GRID_W)
    by_offset = jnp.where(col_ok[None, None], by_offset, NEG_BIG)
    type_blocks = (0, 1, NA_ROWS // NA_QROWS - 1)

    def assemble_kernel(src_ref, o_ref):
        for t, rb in enumerate(type_blocks):
            @pl.when(pl.program_id(0) == t)
            def _(rb=rb):
                r0 = rb * NA_QROWS
                ks = min(max(r0 - NA_KR // 2, 0), NA_ROWS - NA_BAND)
                for i in range(NA_QROWS):
                    r = r0 + i
                    rs = min(max(r - NA_KR // 2, 0), NA_ROWS - NA_KR)
                    for n in range(NA_BAND):
                        kr = ks + n
                        if rs <= kr < rs + NA_KR:
                            tile = src_ref[kr - r + NA_KR - 1]
                        else:
                            tile = jnp.full((GRID_W, GRID_W), NEG_BIG, F32)
                        o_ref[i * GRID_W:(i + 1) * GRID_W, n * GRID_W:(n + 1) * GRID_W] = tile

    return pl.pallas_call(
        assemble_kernel, grid=(len(type_blocks), NA_HEADS),
        in_specs=[pl.BlockSpec((None, 2 * NA_KR - 1, GRID_W, GRID_W), lambda t, h: (h, 0, 0, 0))],
        out_specs=pl.BlockSpec((None, None, NA_QBLK, NA_KBLK), lambda t, h: (t, h, 0, 0)),
        out_shape=jax.ShapeDtypeStruct((len(type_blocks), NA_HEADS, NA_QBLK, NA_KBLK), F32),
        compiler_params=_cparams(("arbitrary", "arbitrary")), name="na_bias_table",
    )(by_offset)


def _softmax_pv(s_list, v_list):
    m = s_list[0].max(axis=-1, keepdims=True)
    for s in s_list[1:]:
        m = jnp.maximum(m, s.max(axis=-1, keepdims=True))
    acc = None
    den = None
    for s, v in zip(s_list, v_list):
        p = jnp.exp(s - m)
        l = p.sum(axis=-1, keepdims=True)
        o = jnp.dot(p.astype(BF16), v, preferred_element_type=F32)
        acc = o if acc is None else acc + o
        den = l if den is None else den + l
    return acc / den


NA_LAT_STEPS = NA_ROWS // NA_QROWS


NA_STEP_HEADS = 4
NA_STEP_LANES = NA_STEP_HEADS * NA_HEAD_DIM


def _na_kernel(q_ref, k_ref, v_ref, kc_ref, vc_ref, tbl_ref, o_ref):
    rb = pl.program_id(2)
    lane = lax.broadcasted_iota(jnp.int32, (NA_QBLK, 2 * NA_HEAD_DIM), 1)

    def pair_lanes(pair):
        return slice(pair * 2 * NA_HEAD_DIM, (pair + 1) * 2 * NA_HEAD_DIM)

    def head_queries(q2, sub):
        sel = (lane < NA_HEAD_DIM) if sub == 0 else (lane >= NA_HEAD_DIM)
        return jnp.where(sel, q2, jnp.zeros_like(q2))

    @pl.when(rb < NA_LAT_STEPS)
    def _():
        ks = jnp.clip(rb * NA_QROWS - NA_KR // 2, 0, NA_ROWS - NA_BAND)
        kstart = pl.multiple_of(ks * GRID_W, GRID_W)
        for pair in range(NA_STEP_HEADS // 2):
            pl_ = pair_lanes(pair)
            q2 = q_ref[:, pl_]
            kb = k_ref[pl.ds(kstart, NA_KBLK), pl_]
            vb = v_ref[pl.ds(kstart, NA_KBLK), pl_]
            kc = kc_ref[:, pl_]
            vc = vc_ref[:, pl_]
            outs = []
            for sub in range(2):
                qm = head_queries(q2, sub)
                s_lat = lax.dot_general(qm, kb, _NT, preferred_element_type=F32) + tbl_ref[0, 2 * pair + sub]
                s_ctx = lax.dot_general(qm, kc, _NT, preferred_element_type=F32)
                outs.append(_softmax_pv([s_lat, s_ctx], [vb, vc]))
            o_ref[:, pl_] = jnp.where(lane < NA_HEAD_DIM, outs[0], outs[1]).astype(o_ref.dtype)

    @pl.when(rb == NA_LAT_STEPS)
    def _():
        for pair in range(NA_STEP_HEADS // 2):
            pl_ = pair_lanes(pair)
            q2 = q_ref[:, pl_]
            kc = kc_ref[:, pl_]
            vc = vc_ref[:, pl_]
            outs = []
            for sub in range(2):
                s_ctx = lax.dot_general(head_queries(q2, sub), kc, _NT, preferred_element_type=F32)
                outs.append(_softmax_pv([s_ctx], [vc]))
            o_ref[:, pl_] = jnp.where(lane < NA_HEAD_DIM, outs[0], outs[1]).astype(o_ref.dtype)


def _na_block_type(rb):
    return jnp.where(rb == 0, 0, jnp.where(rb >= NA_LAT_STEPS - 1, 2, 1))


def _neighbourhood_attention(qkv, table):
    groups = NA_HEADS // NA_STEP_HEADS
    ctx_blk0 = N_LAT // CTX_LEN

    def q_block(b, h, r):
        return (jnp.where(r < NA_LAT_STEPS, b * NA_LAT_STEPS + r, ctx_blk0 + b), h)

    return pl.pallas_call(
        _na_kernel, grid=(BATCH, groups, NA_LAT_STEPS + 1),
        in_specs=[
            pl.BlockSpec((NA_QBLK, NA_STEP_LANES), q_block),
            pl.BlockSpec((SEQ, NA_STEP_LANES), lambda b, h, r: (b, groups + h)),
            pl.BlockSpec((SEQ, NA_STEP_LANES), lambda b, h, r: (b, 2 * groups + h)),
            pl.BlockSpec((CTX_LEN, NA_STEP_LANES), lambda b, h, r: (ctx_blk0 + b, groups + h)),
            pl.BlockSpec((CTX_LEN, NA_STEP_LANES), lambda b, h, r: (ctx_blk0 + b, 2 * groups + h)),
            pl.BlockSpec((1, NA_STEP_HEADS, NA_QBLK, NA_KBLK), lambda b, h, r: (_na_block_type(r), h, 0, 0)),
        ],
        out_specs=pl.BlockSpec((NA_QBLK, NA_STEP_LANES), q_block),
        out_shape=jax.ShapeDtypeStruct((N_ALL, D_MODEL), BF16),
        compiler_params=_cparams(("arbitrary", "arbitrary", "arbitrary")), name="na_attention",
    )(qkv, qkv, qkv, qkv, qkv, table)


CONV_ROWS = 256
CONV_COLS = 2048
CONV_HALO = 8


def _conv_silu_kernel(cur_ref, prev_ref, next_ref, w_ref, b_ref, o_ref):
    i = pl.program_id(0)
    tiles_per_seq = SEQ // CONV_ROWS
    is_ctx = i >= N_LAT // CONV_ROWS
    first = jnp.logical_or(is_ctx, i % tiles_per_seq == 0)
    last = jnp.logical_or(is_ctx, i % tiles_per_seq == tiles_per_seq - 1)
    prev = jnp.where(first, 0.0, prev_ref[...])
    nxt = jnp.where(last, 0.0, next_ref[...])
    cur = cur_ref[...]
    pad = SSD_CONV_W // 2

    def taps(slab, rows):
        acc = jnp.zeros((rows, slab.shape[1]), F32) + b_ref[...]
        for t in range(SSD_CONV_W):
            lo = CONV_HALO - pad + t
            acc = acc + slab[lo:lo + rows, :] * w_ref[t:t + 1, :]
        return acc

    acc = jnp.zeros(o_ref.shape, F32) + b_ref[...]
    for t in range(SSD_CONV_W):
        shifted = cur if t == pad else pltpu.roll(cur, (pad - t) % CONV_ROWS, axis=0)
        acc = acc + shifted * w_ref[t:t + 1, :]
    top = taps(jnp.concatenate([prev, cur[:2 * CONV_HALO]], axis=0), CONV_HALO)
    bottom = taps(jnp.concatenate([cur[CONV_ROWS - 2 * CONV_HALO:], nxt], axis=0), CONV_HALO)
    acc = jnp.concatenate([top, acc[CONV_HALO:CONV_ROWS - CONV_HALO], bottom], axis=0)
    o_ref[...] = acc * jax.nn.sigmoid(acc)


def _conv_silu(zxbc, conv_w, conv_b):
    col0 = SSD_D_INNER // CONV_COLS
    n_halo_blocks = N_ALL // CONV_HALO
    per_tile = CONV_ROWS // CONV_HALO
    return pl.pallas_call(
        _conv_silu_kernel, grid=(N_ALL // CONV_ROWS, SSD_CONV_DIM // CONV_COLS),
        in_specs=[
            pl.BlockSpec((CONV_ROWS, CONV_COLS), lambda i, c: (i, col0 + c)),
            pl.BlockSpec((CONV_HALO, CONV_COLS), lambda i, c: (jnp.maximum(i * per_tile - 1, 0), col0 + c)),
            pl.BlockSpec((CONV_HALO, CONV_COLS),
                         lambda i, c: (jnp.minimum((i + 1) * per_tile, n_halo_blocks - 1), col0 + c)),
            pl.BlockSpec((SSD_CONV_W, CONV_COLS), lambda i, c: (0, c)),
            pl.BlockSpec((1, CONV_COLS), lambda i, c: (0, c)),
        ],
        out_specs=pl.BlockSpec((CONV_ROWS, CONV_COLS), lambda i, c: (i, c)),
        out_shape=jax.ShapeDtypeStruct((N_ALL, SSD_CONV_DIM), F32),
        compiler_params=_cparams(("arbitrary", "arbitrary")), name="ssd_conv_silu",
    )(zxbc, zxbc, zxbc, conv_w, conv_b.reshape(1, SSD_CONV_DIM))


SSD_STEPS = (CTX_LEN + SEQ) // SSD_CHUNK
SSD_CTX_CHUNKS = CTX_LEN // SSD_CHUNK
SSD_LAT_CHUNKS = SEQ // SSD_CHUNK


def _ssd_chunk(b, d, j):
    jj = jnp.where(d == 0, j, jnp.where(j < SSD_CTX_CHUNKS, SSD_CTX_CHUNKS - 1 - j,
                                        SSD_STEPS + SSD_CTX_CHUNKS - 1 - j))
    return jnp.where(jj < SSD_CTX_CHUNKS,
                     N_LAT // SSD_CHUNK + SSD_CTX_CHUNKS * b + jj,
                     SSD_LAT_CHUNKS * b + jj - SSD_CTX_CHUNKS)


def _ssd_kernel(xs_ref, b_ref, c_ref, dtr_ref, bias_ref, alog_ref, expand_ref, y_ref, state_ref):
    d = pl.program_id(1)
    j = pl.program_id(2)

    @pl.when(j == 0)
    def _():
        state_ref[...] = jnp.zeros_like(state_ref)

    fwd = d == 0
    dtr = dtr_ref[...]
    dt_pre = jnp.where(fwd, dtr[:, :SSD_HEADS], dtr[:, SSD_HEADS:]) + bias_ref[...]
    dt = jnp.maximum(dt_pre, 0.0) + jnp.log1p(jnp.exp(-jnp.abs(dt_pre)))
    da = dt * (-jnp.exp(alog_ref[...]))
    li = lax.broadcasted_iota(jnp.int32, (SSD_CHUNK, SSD_CHUNK), 0)
    si = lax.broadcasted_iota(jnp.int32, (SSD_CHUNK, SSD_CHUNK), 1)
    before = jnp.where(fwd, li - si, si - li) >= 0
    before_t = jnp.where(fwd, si - li, li - si) >= 0
    cum = _dot_exact_lhs(jnp.where(before, 1.0, 0.0).astype(BF16), da, 3)
    cum_t = _dot_exact_rhs(da, jnp.where(before_t, 1.0, 0.0).astype(BF16), 3, _TN)
    dt_t = _dot_exact_rhs(dt, jnp.where(li == si, 1.0, 0.0).astype(BF16), 2, _TN)
    tot = jnp.where(fwd, cum[SSD_CHUNK - 1:SSD_CHUNK, :], cum[0:1, :])
    expand = expand_ref[...]
    to_end = _dot_exact_rhs(dt * jnp.exp(tot - cum), expand, 2)
    from_start = _dot_exact_rhs(jnp.exp(cum), expand, 2)
    carry = jnp.where(fwd, from_start[SSD_CHUNK - 1:SSD_CHUNK, :], from_start[0:1, :])
    xs = xs_ref[...]
    x16 = xs.astype(BF16)
    xdec = (xs * to_end).astype(BF16)
    lane = lax.broadcasted_iota(jnp.int32, (SSD_CHUNK, 2 * SSD_HEAD_DIM), 1)
    for g in range(SSD_GROUPS):
        gs = slice(g * SSD_GROUP_W, (g + 1) * SSD_GROUP_W)
        bg = b_ref[:, g * SSD_STATE:(g + 1) * SSD_STATE].astype(BF16)
        cg = c_ref[:, g * SSD_STATE:(g + 1) * SSD_STATE].astype(BF16)
        cb = lax.dot_general(cg, bg, _NT, preferred_element_type=F32)
        st = state_ref[g]
        y_off = jnp.dot(cg, st.astype(BF16), preferred_element_type=F32) * from_start[:, gs]
        y_pairs = []
        for pr in range(SSD_HPG // 2):
            ms = []
            for r2 in range(2):
                h = g * SSD_HPG + pr * 2 + r2
                decay = jnp.exp(jnp.where(before, cum[:, h:h + 1] - cum_t[h:h + 1, :], NEG_BIG))
                ms.append((cb * decay * dt_t[h:h + 1, :]).astype(BF16))
            m2 = jnp.concatenate(ms, axis=1)
            c0 = g * SSD_GROUP_W + pr * 2 * SSD_HEAD_DIM
            slab = x16[:, c0:c0 + 2 * SSD_HEAD_DIM]
            zero = jnp.zeros_like(slab)
            xdiag = jnp.concatenate([jnp.where(lane < SSD_HEAD_DIM, slab, zero),
                                     jnp.where(lane >= SSD_HEAD_DIM, slab, zero)], axis=0)
            y_pairs.append(jnp.dot(m2, xdiag, preferred_element_type=F32))
        y_ref[:, gs] = jnp.concatenate(y_pairs, axis=1) + y_off
        s_new = lax.dot_general(bg, xdec[:, gs], _TN, preferred_element_type=F32)
        state_ref[g] = st * carry[:, gs] + s_new


def _ssd_scan(xbc, dt_raw, dt_bias, a_log):
    expand = (jnp.arange(SSD_HEADS)[:, None] == (jnp.arange(SSD_D_INNER) // SSD_HEAD_DIM)[None, :]).astype(BF16)
    bc_w = SSD_GROUPS * SSD_STATE
    b_col = SSD_D_INNER // bc_w
    return pl.pallas_call(
        _ssd_kernel, grid=(BATCH, 2, SSD_STEPS),
        in_specs=[
            pl.BlockSpec((SSD_CHUNK, SSD_D_INNER), lambda b, d, j: (_ssd_chunk(b, d, j), 0)),
            pl.BlockSpec((SSD_CHUNK, bc_w), lambda b, d, j: (_ssd_chunk(b, d, j), b_col)),
            pl.BlockSpec((SSD_CHUNK, bc_w), lambda b, d, j: (_ssd_chunk(b, d, j), b_col + 1)),
            pl.BlockSpec((SSD_CHUNK, 2 * SSD_HEADS), lambda b, d, j: (_ssd_chunk(b, d, j), 0)),
            pl.BlockSpec((None, 1, SSD_HEADS), lambda b, d, j: (d, 0, 0)),
            pl.BlockSpec((None, 1, SSD_HEADS), lambda b, d, j: (d, 0, 0)),
            pl.BlockSpec((SSD_HEADS, SSD_D_INNER), lambda b, d, j: (0, 0)),
        ],
        out_specs=pl.BlockSpec((None, SSD_CHUNK, SSD_D_INNER), lambda b, d, j: (d, _ssd_chunk(b, d, j), 0)),
        out_shape=jax.ShapeDtypeStruct((2, N_ALL, SSD_D_INNER), F32),
        scratch_shapes=[pltpu.VMEM((SSD_GROUPS, SSD_STATE, SSD_GROUP_W), F32)],
        compiler_params=_cparams(("arbitrary", "arbitrary", "arbitrary")), name="ssd_scan",
    )(xbc, xbc, xbc, dt_raw, dt_bias.reshape(2, 1, SSD_HEADS), a_log.reshape(2, 1, SSD_HEADS), expand)


def _ssd_out_kernel(yf_ref, yb_ref, xs_ref, z_ref, dskip_ref, g_ref, w_ref, res_ref, gate_ref, o_ref,
                    wb_ref, gated_ref):
    _cast_weight_once(pl.program_id(0) == 0, w_ref, wb_ref)
    z = z_ref[...]
    y = (yf_ref[...] + yb_ref[...] + dskip_ref[...] * xs_ref[...]) * (z * jax.nn.sigmoid(z))
    for grp in range(SSD_GROUPS):
        gs = slice(grp * SSD_GROUP_W, (grp + 1) * SSD_GROUP_W)
        yg = y[:, gs]
        ms = jnp.mean(yg * yg, axis=-1, keepdims=True)
        gated_ref[:, gs] = (yg * lax.rsqrt(ms + NORM_EPS) * g_ref[:, gs]).astype(BF16)
    acc = jnp.dot(gated_ref[...], wb_ref[...], preferred_element_type=F32)
    o_ref[...] = res_ref[...] + gate_ref[0] * acc


def _ssd_out_proj(y2, xbc, zxbc, d_skip, norm_g, w_out, res, gate, n_out, tm=256):
    d_cols = jnp.repeat(d_skip, SSD_HEAD_DIM).reshape(1, SSD_D_INNER)
    return pl.pallas_call(
        _ssd_out_kernel, grid=(n_out // tm,),
        in_specs=[
            pl.BlockSpec((None, tm, SSD_D_INNER), lambda i: (0, i, 0)),
            pl.BlockSpec((None, tm, SSD_D_INNER), lambda i: (1, i, 0)),
            pl.BlockSpec((tm, SSD_D_INNER), lambda i: (i, 0)),
            pl.BlockSpec((tm, SSD_D_INNER), lambda i: (i, 0)),
            pl.BlockSpec((1, SSD_D_INNER), lambda i: (0, 0)),
            pl.BlockSpec((1, SSD_D_INNER), lambda i: (0, 0)),
            pl.BlockSpec((SSD_D_INNER, D_MODEL), lambda i: (0, 0)),
            pl.BlockSpec((tm, D_MODEL), lambda i: (i, 0)),
            pl.BlockSpec((1, 1, D_MODEL), lambda i: (_mod_row(i, tm), 0, 0)),
        ],
        out_specs=pl.BlockSpec((tm, D_MODEL), lambda i: (i, 0)),
        out_shape=jax.ShapeDtypeStruct((n_out, D_MODEL), F32),
        scratch_shapes=[pltpu.VMEM((SSD_D_INNER, D_MODEL), BF16), pltpu.VMEM((tm, SSD_D_INNER), BF16)],
        compiler_params=_cparams(("arbitrary",)), name="ssd_out_proj",
    )(y2, y2, xbc, zxbc, d_cols, norm_g.reshape(1, SSD_D_INNER), w_out, res, gate)


DISPATCH_TOKENS = 1024
COMBINE_TOKENS = 256
ZERO_FILL_ROWS = MOE_BLOCK + 8


def _dispatch_kernel(dest_ref, padstart_ref, nreal_ref, h_ref, xb_ref, zeros_ref, sem):
    i = pl.program_id(0)
    base = i * (DISPATCH_TOKENS * TOP_K)

    @pl.when(i == 0)
    def _():
        zeros_ref[...] = jnp.zeros_like(zeros_ref)
        n_slots = xb_ref.shape[0]

        def fill(e):
            start = jnp.minimum(padstart_ref[e] // 8 * 8, n_slots - ZERO_FILL_ROWS)
            return pltpu.make_async_copy(zeros_ref, xb_ref.at[pl.ds(pl.multiple_of(start, 8), ZERO_FILL_ROWS)],
                                         sem.at[0])

        for e in range(N_EXPERTS):
            fill(e).start()
        for e in range(N_EXPERTS):
            fill(e).wait()

        def fill_block(blk):
            return pltpu.make_async_copy(zeros_ref.at[pl.ds(0, MOE_BLOCK)],
                                         xb_ref.at[pl.ds(pl.multiple_of(blk * MOE_BLOCK, MOE_BLOCK), MOE_BLOCK)],
                                         sem.at[0])

        def start_block(blk, carry):
            fill_block(blk).start()
            return carry

        def wait_block(blk, carry):
            fill_block(blk).wait()
            return carry

        lax.fori_loop(nreal_ref[0], n_slots // MOE_BLOCK, start_block, 0)
        lax.fori_loop(nreal_ref[0], n_slots // MOE_BLOCK, wait_block, 0)

    def body(t, carry):
        for k in range(TOP_K):
            pltpu.make_async_copy(h_ref.at[pl.ds(t, 1)], xb_ref.at[pl.ds(dest_ref[base + t * TOP_K + k], 1)],
                                  sem.at[0]).start()
        return carry

    lax.fori_loop(0, DISPATCH_TOKENS, body, 0, unroll=4)
    for k in range(TOP_K):
        pltpu.make_async_copy(h_ref, xb_ref.at[pl.ds(0, DISPATCH_TOKENS)], sem.at[0]).wait()


def _moe_dispatch(h, dest, pad_start, n_real, n_slots):
    n_tok = h.shape[0]
    tt = DISPATCH_TOKENS
    grid_spec = pltpu.PrefetchScalarGridSpec(
        num_scalar_prefetch=3, grid=(n_tok // tt,),
        in_specs=[pl.BlockSpec((tt, D_MODEL), lambda i, d, p, nr: (i, 0))],
        out_specs=pl.BlockSpec(memory_space=pl.ANY),
        scratch_shapes=[pltpu.VMEM((ZERO_FILL_ROWS, D_MODEL), F32), pltpu.SemaphoreType.DMA((1,))],
    )
    return pl.pallas_call(
        _dispatch_kernel, grid_spec=grid_spec,
        out_shape=jax.ShapeDtypeStruct((n_slots, D_MODEL), F32),
        compiler_params=_cparams(("arbitrary",)), name="moe_dispatch",
    )(dest, pad_start, n_real, h)


MOE_HALF = MOE_BLOCK // 2


def _moe_kernel(be_ref, nreal_ref, rows_ref, x_ref, w1_ref, b1_ref, w2_ref, b2_ref, o_ref, w1b_ref, w2b_ref):
    i = pl.program_id(0)

    def ffn(x):
        h = jnp.dot(x.astype(BF16), w1b_ref[...], preferred_element_type=F32) + b1_ref[0]
        glu = jnp.minimum(h[:, :D_FF], SWIGLU_LIMIT)
        lin = jnp.clip(h[:, D_FF:], -SWIGLU_LIMIT, SWIGLU_LIMIT)
        act = glu * jax.nn.sigmoid(SWIGLU_ALPHA * glu) * (lin + 1.0)
        return jnp.dot(act.astype(BF16), w2b_ref[...], preferred_element_type=F32) + b2_ref[0]

    @pl.when(i < nreal_ref[0])
    def _():
        new_expert = jnp.logical_or(i == 0, be_ref[i] != be_ref[jnp.maximum(i - 1, 0)])
        _cast_weight_once(new_expert, w1_ref, w1b_ref)
        _cast_weight_once(new_expert, w2_ref, w2b_ref)

        @pl.when(rows_ref[i] > MOE_HALF)
        def _():
            o_ref[...] = ffn(x_ref[...])

        @pl.when(rows_ref[i] <= MOE_HALF)
        def _():
            o_ref[:MOE_HALF, :] = ffn(x_ref[:MOE_HALF, :])
            o_ref[MOE_HALF:, :] = jnp.zeros((MOE_BLOCK - MOE_HALF, D_MODEL), F32)

    @pl.when(i >= nreal_ref[0])
    def _():
        o_ref[...] = jnp.zeros_like(o_ref)


def _moe_experts(xb, block_expert, n_real, block_rows, layer, w1, b1, w2, b2):
    n_slots = xb.shape[0]
    n_blocks = n_slots // MOE_BLOCK
    grid_spec = pltpu.PrefetchScalarGridSpec(
        num_scalar_prefetch=3, grid=(n_blocks,),
        in_specs=[
            pl.BlockSpec((MOE_BLOCK, D_MODEL), lambda i, be, nr, rows: (jnp.minimum(i, nr[0] - 1), 0)),
            pl.BlockSpec((None, None, D_MODEL, 2 * D_FF), lambda i, be, nr, rows: (layer, be[i], 0, 0)),
            pl.BlockSpec((1, 1, 2 * D_FF), lambda i, be, nr, rows: (be[i], 0, 0)),
            pl.BlockSpec((None, None, D_FF, D_MODEL), lambda i, be, nr, rows: (layer, be[i], 0, 0)),
            pl.BlockSpec((1, 1, D_MODEL), lambda i, be, nr, rows: (be[i], 0, 0)),
        ],
        out_specs=pl.BlockSpec((MOE_BLOCK, D_MODEL), lambda i, be, nr, rows: (i, 0)),
        scratch_shapes=[pltpu.VMEM((D_MODEL, 2 * D_FF), BF16), pltpu.VMEM((D_FF, D_MODEL), BF16)],
    )
    return pl.pallas_call(
        _moe_kernel, grid_spec=grid_spec,
        out_shape=jax.ShapeDtypeStruct((n_slots, D_MODEL), F32),
        compiler_params=_cparams(("arbitrary",)), name="moe_experts",
    )(block_expert, n_real, block_rows, xb, w1, b1.reshape(N_EXPERTS, 1, 2 * D_FF), w2,
      b2.reshape(N_EXPERTS, 1, D_MODEL))


def _combine_tile(dest_ref, y_hbm, route_ref, res_ref, gate_ref, buf, sem):
    i = pl.program_id(0)
    n = pl.num_programs(0)

    def issue(tile, slot):
        base = tile * (COMBINE_TOKENS * TOP_K)

        def body(t, carry):
            for k in range(TOP_K):
                pltpu.make_async_copy(y_hbm.at[pl.ds(dest_ref[base + t * TOP_K + k], 1)],
                                      buf.at[slot, k, pl.ds(t, 1)], sem.at[slot]).start()
            return carry

        lax.fori_loop(0, COMBINE_TOKENS, body, 0, unroll=4)

    @pl.when(i == 0)
    def _():
        issue(0, 0)

    @pl.when(i + 1 < n)
    def _():
        issue(i + 1, (i + 1) % 2)

    slot = i % 2
    for k in range(TOP_K):
        pltpu.make_async_copy(y_hbm.at[pl.ds(0, COMBINE_TOKENS)], buf.at[slot, k], sem.at[slot]).wait()
    route = route_ref[...]
    y = None
    for k in range(TOP_K):
        t = route[:, ROUTE_GATE_LANE + k:ROUTE_GATE_LANE + k + 1] * buf[slot, k]
        y = t if y is None else y + t
    return res_ref[...] + gate_ref[0] * y


def _combine_kernel(dest_ref, y_hbm, route_ref, res_ref, gate_ref, o_ref, buf, sem):
    o_ref[...] = _combine_tile(dest_ref, y_hbm, route_ref, res_ref, gate_ref, buf, sem)


def _combine_norm_kernel(dest_ref, y_hbm, route_ref, res_ref, gate_ref, g_ref, sh_ref, sc_ref, o_ref, h_ref,
                         buf, sem):
    x = _combine_tile(dest_ref, y_hbm, route_ref, res_ref, gate_ref, buf, sem)
    o_ref[...] = x
    ms = jnp.mean(x * x, axis=-1, keepdims=True)
    y = x * lax.rsqrt(ms + NORM_EPS) * g_ref[...]
    h_ref[...] = (y * (1.0 + sc_ref[0]) + sh_ref[0]).astype(h_ref.dtype)


def _moe_combine(yb, dest, route, res, gate, next_norm=None):
    n_tok = route.shape[0]
    tt = COMBINE_TOKENS
    row_spec = pl.BlockSpec((tt, D_MODEL), lambda i, d: (i, 0))
    mod_spec = pl.BlockSpec((1, 1, D_MODEL), lambda i, d: (_mod_row(i, tt), 0, 0))
    in_specs = [pl.BlockSpec(memory_space=pl.ANY), pl.BlockSpec((tt, 128), lambda i, d: (i, 0)), row_spec, mod_spec]
    args = [dest, yb, route, res, gate]
    x_shape = jax.ShapeDtypeStruct((n_tok, D_MODEL), F32)
    if next_norm is None:
        body, out_specs, out_shape = _combine_kernel, row_spec, x_shape
    else:
        g, shift, scale = next_norm
        body = _combine_norm_kernel
        in_specs += [pl.BlockSpec((1, D_MODEL), lambda i, d: (0, 0)), mod_spec, mod_spec]
        args += [g.reshape(1, D_MODEL), shift, scale]
        out_specs = [row_spec, row_spec]
        out_shape = [x_shape, jax.ShapeDtypeStruct((n_tok, D_MODEL), BF16)]
    grid_spec = pltpu.PrefetchScalarGridSpec(
        num_scalar_prefetch=1, grid=(n_tok // tt,), in_specs=in_specs, out_specs=out_specs,
        scratch_shapes=[pltpu.VMEM((2, TOP_K, tt, D_MODEL), F32), pltpu.SemaphoreType.DMA((2,))],
    )
    return pl.pallas_call(
        body, grid_spec=grid_spec, out_shape=out_shape,
        compiler_params=_cparams(("arbitrary",)), name="moe_combine",
    )(*args)


def _moe(h, route, counts, res, gate, layer, w1, b1, w2, b2, next_norm=None):
    n_tok = h.shape[0]
    nk = n_tok * TOP_K
    n_blocks = -(-(nk + N_EXPERTS * (MOE_BLOCK - 1)) // MOE_BLOCK)
    n_slots = n_blocks * MOE_BLOCK
    expert = route[:, ROUTE_EXPERT_LANE:ROUTE_EXPERT_LANE + TOP_K].astype(jnp.int32)
    rank = route[:, ROUTE_RANK_LANE:ROUTE_RANK_LANE + TOP_K].astype(jnp.int32)
    counts = counts.reshape(N_EXPERTS).astype(jnp.int32)
    padded = (counts + MOE_BLOCK - 1) // MOE_BLOCK * MOE_BLOCK
    pend = jnp.cumsum(padded)
    pstart = pend - padded
    ids = jnp.arange(N_EXPERTS, dtype=jnp.int32)
    dest = rank + jnp.sum(jnp.where(expert[..., None] == ids, pstart, 0), axis=-1)
    dest = dest.reshape(nk).astype(jnp.int32)
    block_start = jnp.arange(n_blocks, dtype=jnp.int32) * MOE_BLOCK
    block_expert = jnp.minimum(jnp.sum((pend[None, :] <= block_start[:, None]).astype(jnp.int32), axis=1),
                               N_EXPERTS - 1).astype(jnp.int32)
    n_real = (pend[-1] // MOE_BLOCK).astype(jnp.int32).reshape(1)
    pad_start = (pstart + counts).astype(jnp.int32)
    block_rows = jnp.clip(pad_start[block_expert] - block_start, 0, MOE_BLOCK).astype(jnp.int32)
    xb = _moe_dispatch(h, dest, pad_start, n_real, n_slots)
    yb = _moe_experts(xb, block_expert, n_real, block_rows, layer, w1, b1, w2, b2)
    return _moe_combine(yb, dest, route, res, gate, next_norm)


def kernel(x, c, ctx, c_ctx, ada_w, ada_b, norm1_g, norm2_g, na_w_qkv, na_q_g, na_k_g, na_rpb, na_w_o,
           ssd_w_in, ssd_conv_w, ssd_conv_b, ssd_dt_bias, ssd_a_log, ssd_d, ssd_norm_g, ssd_w_out,
           moe_w_router, moe_b_router, moe_w1, moe_b1, moe_w2, moe_b2):
    xa = jnp.concatenate([x.reshape(N_LAT, D_MODEL), ctx.reshape(N_CTX, D_MODEL)], axis=0)
    cond = jnp.concatenate([jax.nn.silu(c), jax.nn.silu(c_ctx)[None, :],
                            jnp.zeros((16 - BATCH - 1, D_MODEL), F32)], axis=0)
    mods = []
    for i in range(DEPTH):
        mod = _matmul_bias_f32(cond, ada_w[i], ada_b[i], tn=D_MODEL, name="ada_mod")
        mods.append([mod[:, t * D_MODEL:(t + 1) * D_MODEL].reshape(16, 1, D_MODEL) for t in range(6)])
    h1 = _normmod(xa, norm1_g[0], mods[0][0], mods[0][1], N_ALL)
    for i in range(DEPTH):
        last = i == DEPTH - 1
        mt = mods[i]
        n_out = N_LAT if last else N_ALL
        j = i // 2
        if i % 2 == 0:
            qkv = _qkv_proj(h1, na_w_qkv[j], na_q_g[j], na_k_g[j])
            o_all = _neighbourhood_attention(qkv, _na_bias_table(na_rpb[j]))
            xa_new = _matmul_residual(o_all[:n_out], na_w_o[j], xa, mt[2], tm=1024, name="na_out_proj")
        else:
            w_in = ssd_w_in[j]
            zxbc = _matmul(h1, w_in, tm=1024, tn=1024, n_cols=SSD_MAIN_W, name="ssd_in_proj")
            dt_raw = _matmul(h1, w_in[:, SSD_MAIN_W:], tm=512, tn=2 * SSD_HEADS, n_cols=2 * SSD_HEADS,
                             name="ssd_dt_proj")
            xbc = _conv_silu(zxbc, ssd_conv_w[j], ssd_conv_b[j])
            y2 = _ssd_scan(xbc, dt_raw, ssd_dt_bias[j], ssd_a_log[j])
            xa_new = _ssd_out_proj(y2, xbc, zxbc, ssd_d[j], ssd_norm_g[j], ssd_w_out[j], xa, mt[2], n_out)
        n_tok = xa_new.shape[0]
        h2, route, counts = _normmod(xa_new, norm2_g[i], mt[3], mt[4], n_tok,
                                     router=(moe_w_router[i], moe_b_router[i]))
        if last:
            xa = _moe(h2, route, counts, xa_new, mt[5], i, moe_w1, moe_b1[i], moe_w2, moe_b2[i])
        else:
            xa, h1 = _moe(h2, route, counts, xa_new, mt[5], i, moe_w1, moe_b1[i], moe_w2, moe_b2[i],
                          next_norm=(norm1_g[i + 1], mods[i + 1][0], mods[i + 1][1]))
    return xa[:N_LAT].reshape(BATCH, SEQ, D_MODEL)
```

```python
import functools

import jax
import jax.numpy as jnp
from jax import lax
from jax.experimental import pallas as pl
from jax.experimental.pallas import tpu as pltpu

F32 = jnp.float32
BF16 = jnp.bfloat16

D_MODEL = 1024
BATCH = 8
SEQ = 2048
DEPTH = 2
GRID_W = 64
CTX_LEN = 256
N_LAT = BATCH * SEQ
N_CTX = BATCH * CTX_LEN
N_ALL = N_LAT + N_CTX

NA_HEADS = 16
NA_HEAD_DIM = 64
NA_KR = 8
NA_KC = 16
NA_ROWS = SEQ // GRID_W
NA_QROWS = 4
NA_BAND = 12
NA_QBLK = NA_QROWS * GRID_W
NA_KBLK = NA_BAND * GRID_W

SSD_D_INNER = 2048
SSD_HEAD_DIM = 64
SSD_HEADS = 32
SSD_GROUPS = 8
SSD_HPG = 4
SSD_STATE = 128
SSD_CONV_W = 5
SSD_CHUNK = 128
SSD_CONV_DIM = SSD_D_INNER + 2 * SSD_GROUPS * SSD_STATE
SSD_MAIN_W = SSD_D_INNER + SSD_CONV_DIM
SSD_GROUP_W = SSD_HPG * SSD_HEAD_DIM

N_EXPERTS = 32
TOP_K = 4
D_FF = 1024
SWIGLU_ALPHA = 1.702
SWIGLU_LIMIT = 7.0
MOE_BLOCK = 512
NORM_EPS = 1e-6
NEG_BIG = -1e30

VMEM_LIMIT = 56 * 1024 * 1024

_NT = (((1,), (1,)), ((), ()))
_TN = (((0,), (0,)), ((), ()))


def _cparams(sem):
    return pltpu.CompilerParams(dimension_semantics=sem, vmem_limit_bytes=VMEM_LIMIT)


def _mod_row(i, tm):
    return jnp.minimum((i * tm) // SEQ, BATCH)


def _split_bf16(x, pieces):
    out = []
    rem = x
    for _ in range(pieces):
        p = rem.astype(BF16)
        out.append(p)
        rem = rem - p.astype(F32)
    return out


def _dot_exact_rhs(x, sel, pieces, dims=None):
    acc = None
    for p in _split_bf16(x, pieces):
        if dims is None:
            t = jnp.dot(p, sel, preferred_element_type=F32)
        else:
            t = lax.dot_general(p, sel, dims, preferred_element_type=F32)
        acc = t if acc is None else acc + t
    return acc


def _dot_exact_lhs(sel, x, pieces):
    acc = None
    for p in _split_bf16(x, pieces):
        t = jnp.dot(sel, p, preferred_element_type=F32)
        acc = t if acc is None else acc + t
    return acc


def _normmod_kernel(x_ref, g_ref, sh_ref, sc_ref, o_ref):
    x = x_ref[...]
    ms = jnp.mean(x * x, axis=-1, keepdims=True)
    y = x * lax.rsqrt(ms + NORM_EPS) * g_ref[...]
    o_ref[...] = (y * (1.0 + sc_ref[0]) + sh_ref[0]).astype(o_ref.dtype)


ROUTE_EXPERT_LANE = 0
ROUTE_RANK_LANE = 4
ROUTE_GATE_LANE = 8


def _normmod_router_kernel(x_ref, g_ref, sh_ref, sc_ref, wr_ref, br_ref, o_ref, route_ref, count_ref):
    i = pl.program_id(0)
    x = x_ref[...]
    tm = x.shape[0]
    ms = jnp.mean(x * x, axis=-1, keepdims=True)
    y = x * lax.rsqrt(ms + NORM_EPS) * g_ref[...]
    h = y * (1.0 + sc_ref[0]) + sh_ref[0]
    o_ref[...] = h.astype(o_ref.dtype)
    h_hi, h_lo = _split_bf16(h, 2)
    w_hi, w_lo = _split_bf16(wr_ref[...], 2)
    logits = (jnp.dot(h_hi, w_hi, preferred_element_type=F32) + jnp.dot(h_hi, w_lo, preferred_element_type=F32)
              + jnp.dot(h_lo, w_hi, preferred_element_type=F32)) + br_ref[...]

    @pl.when(i == 0)
    def _():
        count_ref[...] = jnp.zeros_like(count_ref)

    lane_e = lax.broadcasted_iota(jnp.int32, logits.shape, 1).astype(F32)
    cur = logits
    vals, ids, hots = [], [], []
    for _ in range(TOP_K):
        m = jnp.max(cur, axis=-1, keepdims=True)
        idx = jnp.min(jnp.where(cur == m, lane_e, float(N_EXPERTS)), axis=-1, keepdims=True)
        hot = lane_e == idx
        vals.append(m)
        ids.append(idx)
        hots.append(hot)
        cur = jnp.where(hot, -jnp.inf, cur)
    exps = [jnp.exp(v - vals[0]) for v in vals]
    den = exps[0] + exps[1] + exps[2] + exps[3]
    picked = jnp.where(hots[0] | hots[1] | hots[2] | hots[3], 1.0, 0.0)
    row = lax.broadcasted_iota(jnp.int32, (tm, tm), 0)
    col = lax.broadcasted_iota(jnp.int32, (tm, tm), 1)
    earlier = jnp.where(col < row, 1.0, 0.0).astype(BF16)
    before = count_ref[...] + jnp.dot(earlier, picked.astype(BF16), preferred_element_type=F32)
    lane = lax.broadcasted_iota(jnp.int32, (tm, 128), 1)
    route = jnp.zeros((tm, 128), F32)
    for k in range(TOP_K):
        rank = jnp.sum(jnp.where(hots[k], before, 0.0), axis=-1, keepdims=True)
        route = jnp.where(lane == ROUTE_EXPERT_LANE + k, ids[k], route)
        route = jnp.where(lane == ROUTE_RANK_LANE + k, rank, route)
        route = jnp.where(lane == ROUTE_GATE_LANE + k, exps[k] / den, route)
    route_ref[...] = route
    count_ref[...] += jnp.sum(picked, axis=0, keepdims=True)


def _normmod(x, g, shift, scale, n_rows, router=None, tm=512):
    grid = (n_rows // tm,)
    x_spec = pl.BlockSpec((tm, D_MODEL), lambda i: (i, 0))
    g_spec = pl.BlockSpec((1, D_MODEL), lambda i: (0, 0))
    m_spec = pl.BlockSpec((1, 1, D_MODEL), lambda i: (_mod_row(i, tm), 0, 0))
    o_spec = pl.BlockSpec((tm, D_MODEL), lambda i: (i, 0))
    g2 = g.reshape(1, D_MODEL)
    if router is None:
        return pl.pallas_call(
            _normmod_kernel, grid=grid,
            in_specs=[x_spec, g_spec, m_spec, m_spec], out_specs=o_spec,
            out_shape=jax.ShapeDtypeStruct((n_rows, D_MODEL), BF16),
            compiler_params=_cparams(("arbitrary",)), name="normmod",
        )(x, g2, shift, scale)
    w_r, b_r = router
    return pl.pallas_call(
        _normmod_router_kernel, grid=grid,
        in_specs=[x_spec, g_spec, m_spec, m_spec,
                  pl.BlockSpec((D_MODEL, N_EXPERTS), lambda i: (0, 0)),
                  pl.BlockSpec((1, N_EXPERTS), lambda i: (0, 0))],
        out_specs=[o_spec, pl.BlockSpec((tm, 128), lambda i: (i, 0)),
                   pl.BlockSpec((1, N_EXPERTS), lambda i: (0, 0))],
        out_shape=[jax.ShapeDtypeStruct((n_rows, D_MODEL), F32),
                   jax.ShapeDtypeStruct((n_rows, 128), F32),
                   jax.ShapeDtypeStruct((1, N_EXPERTS), F32)],
        compiler_params=_cparams(("arbitrary",)), name="normmod_router",
    )(x, g2, shift, scale, w_r, b_r.reshape(1, N_EXPERTS))


def _mm_bias_f32_kernel(a_ref, w_ref, b_ref, o_ref):
    o_ref[...] = jnp.dot(a_ref[...], w_ref[...], preferred_element_type=F32,
                         precision=lax.Precision.HIGHEST) + b_ref[...]


def _matmul_bias_f32(a, w, bias, *, tn, name):
    m, k = a.shape
    n = w.shape[1]
    return pl.pallas_call(
        _mm_bias_f32_kernel, grid=(n // tn,),
        in_specs=[pl.BlockSpec((m, k), lambda j: (0, 0)),
                  pl.BlockSpec((k, tn), lambda j: (0, j)),
                  pl.BlockSpec((1, tn), lambda j: (0, j))],
        out_specs=pl.BlockSpec((m, tn), lambda j: (0, j)),
        out_shape=jax.ShapeDtypeStruct((m, n), F32),
        compiler_params=_cparams(("arbitrary",)), name=name,
    )(a, w, bias.reshape(1, n))


def _cast_weight_once(first, w_ref, wb_ref):
    @pl.when(first)
    def _():
        wb_ref[...] = w_ref[...].astype(BF16)


def _mm_kernel(a_ref, w_ref, o_ref, wb_ref):
    _cast_weight_once(pl.program_id(1) == 0, w_ref, wb_ref)
    o_ref[...] = jnp.dot(a_ref[...], wb_ref[...], preferred_element_type=F32).astype(o_ref.dtype)


def _mm_res_kernel(a_ref, w_ref, res_ref, gate_ref, o_ref, wb_ref):
    _cast_weight_once(pl.program_id(0) == 0, w_ref, wb_ref)
    acc = jnp.dot(a_ref[...], wb_ref[...], preferred_element_type=F32)
    o_ref[...] = res_ref[...] + gate_ref[0] * acc


def _matmul(a, w, *, tm, tn, n_cols, col0=0, out_dtype=F32, name="mm"):
    m, k = a.shape
    return pl.pallas_call(
        _mm_kernel, grid=(n_cols // tn, m // tm),
        in_specs=[pl.BlockSpec((tm, k), lambda j, i: (i, 0)),
                  pl.BlockSpec((k, tn), lambda j, i: (0, col0 + j))],
        out_specs=pl.BlockSpec((tm, tn), lambda j, i: (i, j)),
        out_shape=jax.ShapeDtypeStruct((m, n_cols), out_dtype),
        scratch_shapes=[pltpu.VMEM((k, tn), BF16)],
        compiler_params=_cparams(("arbitrary", "arbitrary")), name=name,
    )(a, w)


def _matmul_residual(a, w, res, gate, *, tm, name):
    m, k = a.shape
    n = w.shape[1]
    return pl.pallas_call(
        _mm_res_kernel, grid=(m // tm,),
        in_specs=[pl.BlockSpec((tm, k), lambda i: (i, 0)),
                  pl.BlockSpec((k, n), lambda i: (0, 0)),
                  pl.BlockSpec((tm, n), lambda i: (i, 0)),
                  pl.BlockSpec((1, 1, n), lambda i: (_mod_row(i, tm), 0, 0))],
        out_specs=pl.BlockSpec((tm, n), lambda i: (i, 0)),
        out_shape=jax.ShapeDtypeStruct((m, n), F32),
        scratch_shapes=[pltpu.VMEM((k, n), BF16)],
        compiler_params=_cparams(("arbitrary",)), name=name,
    )(a, w, res, gate)


def _qkv_kernel(a_ref, w_ref, gain_ref, seg_ref, segt_ref, o_ref, wb_ref):
    j = pl.program_id(0)
    _cast_weight_once(pl.program_id(1) == 0, w_ref, wb_ref)
    acc = jnp.dot(a_ref[...], wb_ref[...], preferred_element_type=F32)

    @pl.when(j < 2)
    def _():
        ss = _dot_exact_rhs(acc * acc, seg_ref[...], 2)
        inv = lax.rsqrt(ss * (1.0 / NA_HEAD_DIM) + NORM_EPS)
        o_ref[...] = (acc * _dot_exact_rhs(inv, segt_ref[...], 2) * gain_ref[0]).astype(o_ref.dtype)

    @pl.when(j == 2)
    def _():
        o_ref[...] = acc.astype(o_ref.dtype)


def _qkv_proj(h, w_qkv, q_g, k_g, tm=1024):
    m = h.shape[0]
    head_of_col = jnp.arange(D_MODEL) // NA_HEAD_DIM
    seg = (head_of_col[:, None] == jnp.arange(128)[None, :]).astype(BF16)
    gain = jnp.stack([jnp.tile(q_g, NA_HEADS) * NA_HEAD_DIM ** -0.5, jnp.tile(k_g, NA_HEADS),
                      jnp.ones((D_MODEL,), F32)]).reshape(3, 1, D_MODEL)
    return pl.pallas_call(
        _qkv_kernel, grid=(3, m // tm),
        in_specs=[pl.BlockSpec((tm, D_MODEL), lambda j, i: (i, 0)),
                  pl.BlockSpec((D_MODEL, D_MODEL), lambda j, i: (0, j)),
                  pl.BlockSpec((1, 1, D_MODEL), lambda j, i: (j, 0, 0)),
                  pl.BlockSpec((D_MODEL, 128), lambda j, i: (0, 0)),
                  pl.BlockSpec((128, D_MODEL), lambda j, i: (0, 0))],
        out_specs=pl.BlockSpec((tm, D_MODEL), lambda j, i: (i, j)),
        out_shape=jax.ShapeDtypeStruct((m, 3 * D_MODEL), BF16),
        scratch_shapes=[pltpu.VMEM((D_MODEL, D_MODEL), BF16)],
        compiler_params=_cparams(("arbitrary", "arbitrary")), name="qkv_proj",
    )(h, w_qkv, gain, seg, seg.T)


def _na_bias_table(rpb):
    col = jnp.arange(GRID_W)
    col_start = jnp.clip(col - NA_KC // 2, 0, GRID_W - NA_KC)
    col_ok = (col[None, :] >= col_start[:, None]) & (col[None, :] < col_start[:, None] + NA_KC)
    col_idx = jnp.clip(col[None, :] - col[:, None] + NA_KC - 1, 0, 2 * NA_KC - 2)
    by_offset = jnp.take(rpb, col_idx.reshape(-1), axis=2).reshape(NA_HEADS, 2 * NA_KR - 1, GRID_W, GRID_W)
    by_offset = jnp.where(col_ok[None, None], by_offset, NEG_BIG)
    type_blocks = (0, 1, NA_ROWS // NA_QROWS - 1)

    def assemble_kernel(src_ref, o_ref):
        for t, rb in enumerate(type_blocks):
            @pl.when(pl.program_id(0) == t)
            def _(rb=rb):
                r0 = rb * NA_QROWS
                ks = min(max(r0 - NA_KR // 2, 0), NA_ROWS - NA_BAND)
                for i in range(NA_QROWS):
                    r = r0 + i
                    rs = min(max(r - NA_KR // 2, 0), NA_ROWS - NA_KR)
                    for n in range(NA_BAND):
                        kr = ks + n
                        if rs <= kr < rs + NA_KR:
                            tile = src_ref[kr - r + NA_KR - 1]
                        else:
                            tile = jnp.full((GRID_W, GRID_W), NEG_BIG, F32)
                        o_ref[i * GRID_W:(i + 1) * GRID_W, n * GRID_W:(n + 1) * GRID_W] = tile

    return pl.pallas_call(
        assemble_kernel, grid=(len(type_blocks), NA_HEADS),
        in_specs=[pl.BlockSpec((None, 2 * NA_KR - 1, GRID_W, GRID_W), lambda t, h: (h, 0, 0, 0))],
        out_specs=pl.BlockSpec((None, None, NA_QBLK, NA_KBLK), lambda t, h: (t, h, 0, 0)),
        out_shape=jax.ShapeDtypeStruct((len(type_blocks), NA_HEADS, NA_QBLK, NA_KBLK), F32),
        compiler_params=_cparams(("arbitrary", "arbitrary")), name="na_bias_table",
    )(by_offset)


def _softmax_pv(s_list, v_list):
    m = s_list[0].max(axis=-1, keepdims=True)
    for s in s_list[1:]:
        m = jnp.maximum(m, s.max(axis=-1, keepdims=True))
    acc = None
    den = None
    for s, v in zip(s_list, v_list):
        p = jnp.exp(s - m)
        l = p.sum(axis=-1, keepdims=True)
        o = jnp.dot(p.astype(BF16), v, preferred_element_type=F32)
        acc = o if acc is None else acc + o
        den = l if den is None else den + l
    return acc / den


NA_LAT_STEPS = NA_ROWS // NA_QROWS


NA_STEP_HEADS = 4
NA_STEP_LANES = NA_STEP_HEADS * NA_HEAD_DIM


def _na_kernel(q_ref, k_ref, v_ref, kc_ref, vc_ref, tbl_ref, o_ref):
    rb = pl.program_id(2)
    lane = lax.broadcasted_iota(jnp.int32, (NA_QBLK, 2 * NA_HEAD_DIM), 1)

    def pair_lanes(pair):
        return slice(pair * 2 * NA_HEAD_DIM, (pair + 1) * 2 * NA_HEAD_DIM)

    def head_queries(q2, sub):
        sel = (lane < NA_HEAD_DIM) if sub == 0 else (lane >= NA_HEAD_DIM)
        return jnp.where(sel, q2, jnp.zeros_like(q2))

    @pl.when(rb < NA_LAT_STEPS)
    def _():
        ks = jnp.clip(rb * NA_QROWS - NA_KR // 2, 0, NA_ROWS - NA_BAND)
        kstart = pl.multiple_of(ks * GRID_W, GRID_W)
        for pair in range(NA_STEP_HEADS // 2):
            pl_ = pair_lanes(pair)
            q2 = q_ref[:, pl_]
            kb = k_ref[pl.ds(kstart, NA_KBLK), pl_]
            vb = v_ref[pl.ds(kstart, NA_KBLK), pl_]
            kc = kc_ref[:, pl_]
            vc = vc_ref[:, pl_]
            outs = []
            for sub in range(2):
                qm = head_queries(q2, sub)
                s_lat = lax.dot_general(qm, kb, _NT, preferred_element_type=F32) + tbl_ref[0, 2 * pair + sub]
                s_ctx = lax.dot_general(qm, kc, _NT, preferred_element_type=F32)
                outs.append(_softmax_pv([s_lat, s_ctx], [vb, vc]))
            o_ref[:, pl_] = jnp.where(lane < NA_HEAD_DIM, outs[0], outs[1]).astype(o_ref.dtype)

    @pl.when(rb == NA_LAT_STEPS)
    def _():
        for pair in range(NA_STEP_HEADS // 2):
            pl_ = pair_lanes(pair)
            q2 = q_ref[:, pl_]
            kc = kc_ref[:, pl_]
            vc = vc_ref[:, pl_]
            outs = []
            for sub in range(2):
                s_ctx = lax.dot_general(head_queries(q2, sub), kc, _NT, preferred_element_type=F32)
                outs.append(_softmax_pv([s_ctx], [vc]))
            o_ref[:, pl_] = jnp.where(lane < NA_HEAD_DIM, outs[0], outs[1]).astype(o_ref.dtype)


def _na_block_type(rb):
    return jnp.where(rb == 0, 0, jnp.where(rb >= NA_LAT_STEPS - 1, 2, 1))


def _neighbourhood_attention(qkv, table):
    groups = NA_HEADS // NA_STEP_HEADS
    ctx_blk0 = N_LAT // CTX_LEN

    def q_block(b, h, r):
        return (jnp.where(r < NA_LAT_STEPS, b * NA_LAT_STEPS + r, ctx_blk0 + b), h)

    return pl.pallas_call(
        _na_kernel, grid=(BATCH, groups, NA_LAT_STEPS + 1),
        in_specs=[
            pl.BlockSpec((NA_QBLK, NA_STEP_LANES), q_block),
            pl.BlockSpec((SEQ, NA_STEP_LANES), lambda b, h, r: (b, groups + h)),
            pl.BlockSpec((SEQ, NA_STEP_LANES), lambda b, h, r: (b, 2 * groups + h)),
            pl.BlockSpec((CTX_LEN, NA_STEP_LANES), lambda b, h, r: (ctx_blk0 + b, groups + h)),
            pl.BlockSpec((CTX_LEN, NA_STEP_LANES), lambda b, h, r: (ctx_blk0 + b, 2 * groups + h)),
            pl.BlockSpec((1, NA_STEP_HEADS, NA_QBLK, NA_KBLK), lambda b, h, r: (_na_block_type(r), h, 0, 0)),
        ],
        out_specs=pl.BlockSpec((NA_QBLK, NA_STEP_LANES), q_block),
        out_shape=jax.ShapeDtypeStruct((N_ALL, D_MODEL), BF16),
        compiler_params=_cparams(("arbitrary", "arbitrary", "arbitrary")), name="na_attention",
    )(qkv, qkv, qkv, qkv, qkv, table)


CONV_ROWS = 256
CONV_COLS = 2048
CONV_HALO = 8


def _conv_silu_kernel(cur_ref, prev_ref, next_ref, w_ref, b_ref, o_ref):
    i = pl.program_id(0)
    tiles_per_seq = SEQ // CONV_ROWS
    is_ctx = i >= N_LAT // CONV_ROWS
    first = jnp.logical_or(is_ctx, i % tiles_per_seq == 0)
    last = jnp.logical_or(is_ctx, i % tiles_per_seq == tiles_per_seq - 1)
    prev = jnp.where(first, 0.0, prev_ref[...])
    nxt = jnp.where(last, 0.0, next_ref[...])
    cur = cur_ref[...]
    pad = SSD_CONV_W // 2

    def taps(slab, rows):
        acc = jnp.zeros((rows, slab.shape[1]), F32) + b_ref[...]
        for t in range(SSD_CONV_W):
            lo = CONV_HALO - pad + t
            acc = acc + slab[lo:lo + rows, :] * w_ref[t:t + 1, :]
        return acc

    acc = jnp.zeros(o_ref.shape, F32) + b_ref[...]
    for t in range(SSD_CONV_W):
        shifted = cur if t == pad else pltpu.roll(cur, (pad - t) % CONV_ROWS, axis=0)
        acc = acc + shifted * w_ref[t:t + 1, :]
    top = taps(jnp.concatenate([prev, cur[:2 * CONV_HALO]], axis=0), CONV_HALO)
    bottom = taps(jnp.concatenate([cur[CONV_ROWS - 2 * CONV_HALO:], nxt], axis=0), CONV_HALO)
    acc = jnp.concatenate([top, acc[CONV_HALO:CONV_ROWS - CONV_HALO], bottom], axis=0)
    o_ref[...] = acc * jax.nn.sigmoid(acc)


def _conv_silu(zxbc, conv_w, conv_b):
    col0 = SSD_D_INNER // CONV_COLS
    n_halo_blocks = N_ALL // CONV_HALO
    per_tile = CONV_ROWS // CONV_HALO
    return pl.pallas_call(
        _conv_silu_kernel, grid=(N_ALL // CONV_ROWS, SSD_CONV_DIM // CONV_COLS),
        in_specs=[
            pl.BlockSpec((CONV_ROWS, CONV_COLS), lambda i, c: (i, col0 + c)),
            pl.BlockSpec((CONV_HALO, CONV_COLS), lambda i, c: (jnp.maximum(i * per_tile - 1, 0), col0 + c)),
            pl.BlockSpec((CONV_HALO, CONV_COLS),
                         lambda i, c: (jnp.minimum((i + 1) * per_tile, n_halo_blocks - 1), col0 + c)),
            pl.BlockSpec((SSD_CONV_W, CONV_COLS), lambda i, c: (0, c)),
            pl.BlockSpec((1, CONV_COLS), lambda i, c: (0, c)),
        ],
        out_specs=pl.BlockSpec((CONV_ROWS, CONV_COLS), lambda i, c: (i, c)),
        out_shape=jax.ShapeDtypeStruct((N_ALL, SSD_CONV_DIM), F32),
        compiler_params=_cparams(("arbitrary", "arbitrary")), name="ssd_conv_silu",
    )(zxbc, zxbc, zxbc, conv_w, conv_b.reshape(1, SSD_CONV_DIM))


SSD_STEPS = (CTX_LEN + SEQ) // SSD_CHUNK
SSD_CTX_CHUNKS = CTX_LEN // SSD_CHUNK
SSD_LAT_CHUNKS = SEQ // SSD_CHUNK


def _ssd_chunk(b, d, j):
    jj = jnp.where(d == 0, j, jnp.where(j < SSD_CTX_CHUNKS, SSD_CTX_CHUNKS - 1 - j,
                                        SSD_STEPS + SSD_CTX_CHUNKS - 1 - j))
    return jnp.where(jj < SSD_CTX_CHUNKS,
                     N_LAT // SSD_CHUNK + SSD_CTX_CHUNKS * b + jj,
                     SSD_LAT_CHUNKS * b + jj - SSD_CTX_CHUNKS)


def _ssd_kernel(xs_ref, b_ref, c_ref, dtr_ref, bias_ref, alog_ref, expand_ref, y_ref, state_ref):
    d = pl.program_id(1)
    j = pl.program_id(2)

    @pl.when(j == 0)
    def _():
        state_ref[...] = jnp.zeros_like(state_ref)

    fwd = d == 0
    dtr = dtr_ref[...]
    dt_pre = jnp.where(fwd, dtr[:, :SSD_HEADS], dtr[:, SSD_HEADS:]) + bias_ref[...]
    dt = jnp.maximum(dt_pre, 0.0) + jnp.log1p(jnp.exp(-jnp.abs(dt_pre)))
    da = dt * (-jnp.exp(alog_ref[...]))
    li = lax.broadcasted_iota(jnp.int32, (SSD_CHUNK, SSD_CHUNK), 0)
    si = lax.broadcasted_iota(jnp.int32, (SSD_CHUNK, SSD_CHUNK), 1)
    before = jnp.where(fwd, li - si, si - li) >= 0
    before_t = jnp.where(fwd, si - li, li - si) >= 0
    cum = _dot_exact_lhs(jnp.where(before, 1.0, 0.0).astype(BF16), da, 3)
    cum_t = _dot_exact_rhs(da, jnp.where(before_t, 1.0, 0.0).astype(BF16), 3, _TN)
    dt_t = _dot_exact_rhs(dt, jnp.where(li == si, 1.0, 0.0).astype(BF16), 2, _TN)
    tot = jnp.where(fwd, cum[SSD_CHUNK - 1:SSD_CHUNK, :], cum[0:1, :])
    expand = expand_ref[...]
    to_end = _dot_exact_rhs(dt * jnp.exp(tot - cum), expand, 2)
    from_start = _dot_exact_rhs(jnp.exp(cum), expand, 2)
    carry = jnp.where(fwd, from_start[SSD_CHUNK - 1:SSD_CHUNK, :], from_start[0:1, :])
    xs = xs_ref[...]
    x16 = xs.astype(BF16)
    xdec = (xs * to_end).astype(BF16)
    lane = lax.broadcasted_iota(jnp.int32, (SSD_CHUNK, 2 * SSD_HEAD_DIM), 1)
    for g in range(SSD_GROUPS):
        gs = slice(g * SSD_GROUP_W, (g + 1) * SSD_GROUP_W)
        bg = b_ref[:, g * SSD_STATE:(g + 1) * SSD_STATE].astype(BF16)
        cg = c_ref[:, g * SSD_STATE:(g + 1) * SSD_STATE].astype(BF16)
        cb = lax.dot_general(cg, bg, _NT, preferred_element_type=F32)
        st = state_ref[g]
        y_off = jnp.dot(cg, st.astype(BF16), preferred_element_type=F32) * from_start[:, gs]
        y_pairs = []
        for pr in range(SSD_HPG // 2):
            ms = []
            for r2 in range(2):
                h = g * SSD_HPG + pr * 2 + r2
                decay = jnp.exp(jnp.where(before, cum[:, h:h + 1] - cum_t[h:h + 1, :], NEG_BIG))
                ms.append((cb * decay * dt_t[h:h + 1, :]).astype(BF16))
            m2 = jnp.concatenate(ms, axis=1)
            c0 = g * SSD_GROUP_W + pr * 2 * SSD_HEAD_DIM
            slab = x16[:, c0:c0 + 2 * SSD_HEAD_DIM]
            zero = jnp.zeros_like(slab)
            xdiag = jnp.concatenate([jnp.where(lane < SSD_HEAD_DIM, slab, zero),
                                     jnp.where(lane >= SSD_HEAD_DIM, slab, zero)], axis=0)
            y_pairs.append(jnp.dot(m2, xdiag, preferred_element_type=F32))
        y_ref[:, gs] = jnp.concatenate(y_pairs, axis=1) + y_off
        s_new = lax.dot_general(bg, xdec[:, gs], _TN, preferred_element_type=F32)
        state_ref[g] = st * carry[:, gs] + s_new


def _ssd_scan(xbc, dt_raw, dt_bias, a_log):
    expand = (jnp.arange(SSD_HEADS)[:, None] == (jnp.arange(SSD_D_INNER) // SSD_HEAD_DIM)[None, :]).astype(BF16)
    bc_w = SSD_GROUPS * SSD_STATE
    b_col = SSD_D_INNER // bc_w
    return pl.pallas_call(
        _ssd_kernel, grid=(BATCH, 2, SSD_STEPS),
        in_specs=[
            pl.BlockSpec((SSD_CHUNK, SSD_D_INNER), lambda b, d, j: (_ssd_chunk(b, d, j), 0)),
            pl.BlockSpec((SSD_CHUNK, bc_w), lambda b, d, j: (_ssd_chunk(b, d, j), b_col)),
            pl.BlockSpec((SSD_CHUNK, bc_w), lambda b, d, j: (_ssd_chunk(b, d, j), b_col + 1)),
            pl.BlockSpec((SSD_CHUNK, 2 * SSD_HEADS), lambda b, d, j: (_ssd_chunk(b, d, j), 0)),
            pl.BlockSpec((None, 1, SSD_HEADS), lambda b, d, j: (d, 0, 0)),
            pl.BlockSpec((None, 1, SSD_HEADS), lambda b, d, j: (d, 0, 0)),
            pl.BlockSpec((SSD_HEADS, SSD_D_INNER), lambda b, d, j: (0, 0)),
        ],
        out_specs=pl.BlockSpec((None, SSD_CHUNK, SSD_D_INNER), lambda b, d, j: (d, _ssd_chunk(b, d, j), 0)),
        out_shape=jax.ShapeDtypeStruct((2, N_ALL, SSD_D_INNER), F32),
        scratch_shapes=[pltpu.VMEM((SSD_GROUPS, SSD_STATE, SSD_GROUP_W), F32)],
        compiler_params=_cparams(("arbitrary", "arbitrary", "arbitrary")), name="ssd_scan",
    )(xbc, xbc, xbc, dt_raw, dt_bias.reshape(2, 1, SSD_HEADS), a_log.reshape(2, 1, SSD_HEADS), expand)


def _ssd_out_kernel(yf_ref, yb_ref, xs_ref, z_ref, dskip_ref, g_ref, w_ref, res_ref, gate_ref, o_ref,
                    wb_ref, gated_ref):
    _cast_weight_once(pl.program_id(0) == 0, w_ref, wb_ref)
    z = z_ref[...]
    y = (yf_ref[...] + yb_ref[...] + dskip_ref[...] * xs_ref[...]) * (z * jax.nn.sigmoid(z))
    for grp in range(SSD_GROUPS):
        gs = slice(grp * SSD_GROUP_W, (grp + 1) * SSD_GROUP_W)
        yg = y[:, gs]
        ms = jnp.mean(yg * yg, axis=-1, keepdims=True)
        gated_ref[:, gs] = (yg * lax.rsqrt(ms + NORM_EPS) * g_ref[:, gs]).astype(BF16)
    acc = jnp.dot(gated_ref[...], wb_ref[...], preferred_element_type=F32)
    o_ref[...] = res_ref[...] + gate_ref[0] * acc


def _ssd_out_proj(y2, xbc, zxbc, d_skip, norm_g, w_out, res, gate, n_out, tm=256):
    d_cols = jnp.repeat(d_skip, SSD_HEAD_DIM).reshape(1, SSD_D_INNER)
    return pl.pallas_call(
        _ssd_out_kernel, grid=(n_out // tm,),
        in_specs=[
            pl.BlockSpec((None, tm, SSD_D_INNER), lambda i: (0, i, 0)),
            pl.BlockSpec((None, tm, SSD_D_INNER), lambda i: (1, i, 0)),
            pl.BlockSpec((tm, SSD_D_INNER), lambda i: (i, 0)),
            pl.BlockSpec((tm, SSD_D_INNER), lambda i: (i, 0)),
            pl.BlockSpec((1, SSD_D_INNER), lambda i: (0, 0)),
            pl.BlockSpec((1, SSD_D_INNER), lambda i: (0, 0)),
            pl.BlockSpec((SSD_D_INNER, D_MODEL), lambda i: (0, 0)),
            pl.BlockSpec((tm, D_MODEL), lambda i: (i, 0)),
            pl.BlockSpec((1, 1, D_MODEL), lambda i: (_mod_row(i, tm), 0, 0)),
        ],
        out_specs=pl.BlockSpec((tm, D_MODEL), lambda i: (i, 0)),
        out_shape=jax.ShapeDtypeStruct((n_out, D_MODEL), F32),
        scratch_shapes=[pltpu.VMEM((SSD_D_INNER, D_MODEL), BF16), pltpu.VMEM((tm, SSD_D_INNER), BF16)],
        compiler_params=_cparams(("arbitrary",)), name="ssd_out_proj",
    )(y2, y2, xbc, zxbc, d_cols, norm_g.reshape(1, SSD_D_INNER), w_out, res, gate)


DISPATCH_TOKENS = 1024
COMBINE_TOKENS = 256
ZERO_FILL_ROWS = MOE_BLOCK + 8


def _dispatch_kernel(dest_ref, padstart_ref, nreal_ref, h_ref, xb_ref, zeros_ref, sem):
    i = pl.program_id(0)
    base = i * (DISPATCH_TOKENS * TOP_K)

    @pl.when(i == 0)
    def _():
        zeros_ref[...] = jnp.zeros_like(zeros_ref)
        n_slots = xb_ref.shape[0]

        def fill(e):
            start = jnp.minimum(padstart_ref[e] // 8 * 8, n_slots - ZERO_FILL_ROWS)
            return pltpu.make_async_copy(zeros_ref, xb_ref.at[pl.ds(pl.multiple_of(start, 8), ZERO_FILL_ROWS)],
                                         sem.at[0])

        for e in range(N_EXPERTS):
            fill(e).start()
        for e in range(N_EXPERTS):
            fill(e).wait()

        def fill_block(blk):
            return pltpu.make_async_copy(zeros_ref.at[pl.ds(0, MOE_BLOCK)],
                                         xb_ref.at[pl.ds(pl.multiple_of(blk * MOE_BLOCK, MOE_BLOCK), MOE_BLOCK)],
                                         sem.at[0])

        def start_block(blk, carry):
            fill_block(blk).start()
            return carry

        def wait_block(blk, carry):
            fill_block(blk).wait()
            return carry

        lax.fori_loop(nreal_ref[0], n_slots // MOE_BLOCK, start_block, 0)
        lax.fori_loop(nreal_ref[0], n_slots // MOE_BLOCK, wait_block, 0)

    def body(t, carry):
        for k in range(TOP_K):
            pltpu.make_async_copy(h_ref.at[pl.ds(t, 1)], xb_ref.at[pl.ds(dest_ref[base + t * TOP_K + k], 1)],
                                  sem.at[0]).start()
        return carry

    lax.fori_loop(0, DISPATCH_TOKENS, body, 0, unroll=4)
    for k in range(TOP_K):
        pltpu.make_async_copy(h_ref, xb_ref.at[pl.ds(0, DISPATCH_TOKENS)], sem.at[0]).wait()


def _moe_dispatch(h, dest, pad_start, n_real, n_slots):
    n_tok = h.shape[0]
    tt = DISPATCH_TOKENS
    grid_spec = pltpu.PrefetchScalarGridSpec(
        num_scalar_prefetch=3, grid=(n_tok // tt,),
        in_specs=[pl.BlockSpec((tt, D_MODEL), lambda i, d, p, nr: (i, 0))],
        out_specs=pl.BlockSpec(memory_space=pl.ANY),
        scratch_shapes=[pltpu.VMEM((ZERO_FILL_ROWS, D_MODEL), F32), pltpu.SemaphoreType.DMA((1,))],
    )
    return pl.pallas_call(
        _dispatch_kernel, grid_spec=grid_spec,
        out_shape=jax.ShapeDtypeStruct((n_slots, D_MODEL), F32),
        compiler_params=_cparams(("arbitrary",)), name="moe_dispatch",
    )(dest, pad_start, n_real, h)


def _moe_kernel(be_ref, nreal_ref, first_ref, slot_ref, next_ref, x_ref, w1_hbm, b1_ref, w2_hbm, b2_ref, o_ref,
                w1f_ref, w2f_ref, w1b_ref, w2b_ref, sem, *, layer):
    i = pl.program_id(0)

    def weight_copies(e, slot):
        return (pltpu.make_async_copy(w1_hbm.at[layer, e], w1f_ref.at[slot], sem.at[slot]),
                pltpu.make_async_copy(w2_hbm.at[layer, e], w2f_ref.at[slot], sem.at[slot]))

    @pl.when(i < nreal_ref[0])
    def _():
        @pl.when(i == 0)
        def _():
            for c in weight_copies(be_ref[0], 0):
                c.start()

        @pl.when(first_ref[i] == 1)
        def _():
            slot = slot_ref[i]
            for c in weight_copies(be_ref[i], slot):
                c.wait()
            w1b_ref[...] = w1f_ref[slot].astype(BF16)
            w2b_ref[...] = w2f_ref[slot].astype(BF16)

            @pl.when(next_ref[i] >= 0)
            def _():
                for c in weight_copies(next_ref[i], 1 - slot):
                    c.start()

        h = jnp.dot(x_ref[...].astype(BF16), w1b_ref[...], preferred_element_type=F32) + b1_ref[0]
        glu = jnp.minimum(h[:, :D_FF], SWIGLU_LIMIT)
        lin = jnp.clip(h[:, D_FF:], -SWIGLU_LIMIT, SWIGLU_LIMIT)
        act = glu * jax.nn.sigmoid(SWIGLU_ALPHA * glu) * (lin + 1.0)
        o_ref[...] = jnp.dot(act.astype(BF16), w2b_ref[...], preferred_element_type=F32) + b2_ref[0]

    @pl.when(i >= nreal_ref[0])
    def _():
        o_ref[...] = jnp.zeros_like(o_ref)


def _moe_experts(xb, block_expert, n_real, pend, layer, w1, b1, w2, b2):
    n_slots = xb.shape[0]
    n_blocks = n_slots // MOE_BLOCK
    first = jnp.concatenate([jnp.ones((1,), jnp.int32),
                             (block_expert[1:] != block_expert[:-1]).astype(jnp.int32)])
    slot = (jnp.cumsum(first) - 1) % 2
    next_block = pend[block_expert] // MOE_BLOCK
    next_expert = jnp.where(next_block < n_real[0], block_expert[jnp.minimum(next_block, n_blocks - 1)], -1)
    grid_spec = pltpu.PrefetchScalarGridSpec(
        num_scalar_prefetch=5, grid=(n_blocks,),
        in_specs=[
            pl.BlockSpec((MOE_BLOCK, D_MODEL), lambda i, be, nr, *_: (jnp.minimum(i, nr[0] - 1), 0)),
            pl.BlockSpec(memory_space=pl.ANY),
            pl.BlockSpec((1, 1, 2 * D_FF), lambda i, be, *_: (be[i], 0, 0)),
            pl.BlockSpec(memory_space=pl.ANY),
            pl.BlockSpec((1, 1, D_MODEL), lambda i, be, *_: (be[i], 0, 0)),
        ],
        out_specs=pl.BlockSpec((MOE_BLOCK, D_MODEL), lambda i, *_: (i, 0)),
        scratch_shapes=[pltpu.VMEM((2, D_MODEL, 2 * D_FF), F32), pltpu.VMEM((2, D_FF, D_MODEL), F32),
                        pltpu.VMEM((D_MODEL, 2 * D_FF), BF16), pltpu.VMEM((D_FF, D_MODEL), BF16),
                        pltpu.SemaphoreType.DMA((2,))],
    )
    return pl.pallas_call(
        functools.partial(_moe_kernel, layer=layer), grid_spec=grid_spec,
        out_shape=jax.ShapeDtypeStruct((n_slots, D_MODEL), F32),
        compiler_params=_cparams(("arbitrary",)), name="moe_experts",
    )(block_expert, n_real, first, slot.astype(jnp.int32), next_expert.astype(jnp.int32), xb, w1,
      b1.reshape(N_EXPERTS, 1, 2 * D_FF), w2, b2.reshape(N_EXPERTS, 1, D_MODEL))


def _combine_tile(dest_ref, y_hbm, route_ref, res_ref, gate_ref, buf, sem):
    i = pl.program_id(0)
    n = pl.num_programs(0)

    def issue(tile, slot):
        base = tile * (COMBINE_TOKENS * TOP_K)

        def body(t, carry):
            for k in range(TOP_K):
                pltpu.make_async_copy(y_hbm.at[pl.ds(dest_ref[base + t * TOP_K + k], 1)],
                                      buf.at[slot, k, pl.ds(t, 1)], sem.at[slot]).start()
            return carry

        lax.fori_loop(0, COMBINE_TOKENS, body, 0, unroll=4)

    @pl.when(i == 0)
    def _():
        issue(0, 0)

    @pl.when(i + 1 < n)
    def _():
        issue(i + 1, (i + 1) % 2)

    slot = i % 2
    for k in range(TOP_K):
        pltpu.make_async_copy(y_hbm.at[pl.ds(0, COMBINE_TOKENS)], buf.at[slot, k], sem.at[slot]).wait()
    route = route_ref[...]
    y = None
    for k in range(TOP_K):
        t = route[:, ROUTE_GATE_LANE + k:ROUTE_GATE_LANE + k + 1] * buf[slot, k]
        y = t if y is None else y + t
    return res_ref[...] + gate_ref[0] * y


def _combine_kernel(dest_ref, y_hbm, route_ref, res_ref, gate_ref, o_ref, buf, sem):
    o_ref[...] = _combine_tile(dest_ref, y_hbm, route_ref, res_ref, gate_ref, buf, sem)


def _combine_norm_kernel(dest_ref, y_hbm, route_ref, res_ref, gate_ref, g_ref, sh_ref, sc_ref, o_ref, h_ref,
                         buf, sem):
    x = _combine_tile(dest_ref, y_hbm, route_ref, res_ref, gate_ref, buf, sem)
    o_ref[...] = x
    ms = jnp.mean(x * x, axis=-1, keepdims=True)
    y = x * lax.rsqrt(ms + NORM_EPS) * g_ref[...]
    h_ref[...] = (y * (1.0 + sc_ref[0]) + sh_ref[0]).astype(h_ref.dtype)


def _moe_combine(yb, dest, route, res, gate, next_norm=None):
    n_tok = route.shape[0]
    tt = COMBINE_TOKENS
    row_spec = pl.BlockSpec((tt, D_MODEL), lambda i, d: (i, 0))
    mod_spec = pl.BlockSpec((1, 1, D_MODEL), lambda i, d: (_mod_row(i, tt), 0, 0))
    in_specs = [pl.BlockSpec(memory_space=pl.ANY), pl.BlockSpec((tt, 128), lambda i, d: (i, 0)), row_spec, mod_spec]
    args = [dest, yb, route, res, gate]
    x_shape = jax.ShapeDtypeStruct((n_tok, D_MODEL), F32)
    if next_norm is None:
        body, out_specs, out_shape = _combine_kernel, row_spec, x_shape
    else:
        g, shift, scale = next_norm
        body = _combine_norm_kernel
        in_specs += [pl.BlockSpec((1, D_MODEL), lambda i, d: (0, 0)), mod_spec, mod_spec]
        args += [g.reshape(1, D_MODEL), shift, scale]
        out_specs = [row_spec, row_spec]
        out_shape = [x_shape, jax.ShapeDtypeStruct((n_tok, D_MODEL), BF16)]
    grid_spec = pltpu.PrefetchScalarGridSpec(
        num_scalar_prefetch=1, grid=(n_tok // tt,), in_specs=in_specs, out_specs=out_specs,
        scratch_shapes=[pltpu.VMEM((2, TOP_K, tt, D_MODEL), F32), pltpu.SemaphoreType.DMA((2,))],
    )
    return pl.pallas_call(
        body, grid_spec=grid_spec, out_shape=out_shape,
        compiler_params=_cparams(("arbitrary",)), name="moe_combine",
    )(*args)


def _moe(h, route, counts, res, gate, layer, w1, b1, w2, b2, next_norm=None):
    n_tok = h.shape[0]
    nk = n_tok * TOP_K
    n_blocks = -(-(nk + N_EXPERTS * (MOE_BLOCK - 1)) // MOE_BLOCK)
    n_slots = n_blocks * MOE_BLOCK
    expert = route[:, ROUTE_EXPERT_LANE:ROUTE_EXPERT_LANE + TOP_K].astype(jnp.int32)
    rank = route[:, ROUTE_RANK_LANE:ROUTE_RANK_LANE + TOP_K].astype(jnp.int32)
    counts = counts.reshape(N_EXPERTS).astype(jnp.int32)
    padded = (counts + MOE_BLOCK - 1) // MOE_BLOCK * MOE_BLOCK
    pend = jnp.cumsum(padded)
    pstart = pend - padded
    ids = jnp.arange(N_EXPERTS, dtype=jnp.int32)
    dest = rank + jnp.sum(jnp.where(expert[..., None] == ids, pstart, 0), axis=-1)
    dest = dest.reshape(nk).astype(jnp.int32)
    block_start = jnp.arange(n_blocks, dtype=jnp.int32) * MOE_BLOCK
    block_expert = jnp.minimum(jnp.sum((pend[None, :] <= block_start[:, None]).astype(jnp.int32), axis=1),
                               N_EXPERTS - 1).astype(jnp.int32)
    n_real = (pend[-1] // MOE_BLOCK).astype(jnp.int32).reshape(1)
    pad_start = (pstart + counts).astype(jnp.int32)
    xb = _moe_dispatch(h, dest, pad_start, n_real, n_slots)
    yb = _moe_experts(xb, block_expert, n_real, pend.astype(jnp.int32), layer, w1, b1, w2, b2)
    return _moe_combine(yb, dest, route, res, gate, next_norm)


def kernel(x, c, ctx, c_ctx, ada_w, ada_b, norm1_g, norm2_g, na_w_qkv, na_q_g, na_k_g, na_rpb, na_w_o,
           ssd_w_in, ssd_conv_w, ssd_conv_b, ssd_dt_bias, ssd_a_log, ssd_d, ssd_norm_g, ssd_w_out,
           moe_w_router, moe_b_router, moe_w1, moe_b1, moe_w2, moe_b2):
    xa = jnp.concatenate([x.reshape(N_LAT, D_MODEL), ctx.reshape(N_CTX, D_MODEL)], axis=0)
    cond = jnp.concatenate([jax.nn.silu(c), jax.nn.silu(c_ctx)[None, :],
                            jnp.zeros((16 - BATCH - 1, D_MODEL), F32)], axis=0)
    mods = []
    for i in range(DEPTH):
        mod = _matmul_bias_f32(cond, ada_w[i], ada_b[i], tn=D_MODEL, name="ada_mod")
        mods.append([mod[:, t * D_MODEL:(t + 1) * D_MODEL].reshape(16, 1, D_MODEL) for t in range(6)])
    h1 = _normmod(xa, norm1_g[0], mods[0][0], mods[0][1], N_ALL)
    for i in range(DEPTH):
        last = i == DEPTH - 1
        mt = mods[i]
        n_out = N_LAT if last else N_ALL
        j = i // 2
        if i % 2 == 0:
            qkv = _qkv_proj(h1, na_w_qkv[j], na_q_g[j], na_k_g[j])
            o_all = _neighbourhood_attention(qkv, _na_bias_table(na_rpb[j]))
            xa_new = _matmul_residual(o_all[:n_out], na_w_o[j], xa, mt[2], tm=1024, name="na_out_proj")
        else:
            w_in = ssd_w_in[j]
            zxbc = _matmul(h1, w_in, tm=1024, tn=1024, n_cols=SSD_MAIN_W, name="ssd_in_proj")
            dt_raw = _matmul(h1, w_in[:, SSD_MAIN_W:], tm=512, tn=2 * SSD_HEADS, n_cols=2 * SSD_HEADS,
                             name="ssd_dt_proj")
            xbc = _conv_silu(zxbc, ssd_conv_w[j], ssd_conv_b[j])
            y2 = _ssd_scan(xbc, dt_raw, ssd_dt_bias[j], ssd_a_log[j])
            xa_new = _ssd_out_proj(y2, xbc, zxbc, ssd_d[j], ssd_norm_g[j], ssd_w_out[j], xa, mt[2], n_out)
        n_tok = xa_new.shape[0]
        h2, route, counts = _normmod(xa_new, norm2_g[i], mt[3], mt[4], n_tok,
                                     router=(moe_w_router[i], moe_b_router[i]))
        if last:
            xa = _moe(h2, route, counts, xa_new, mt[5], i, moe_w1, moe_b1[i], moe_w2, moe_b2[i])
        else:
            xa, h1 = _moe(h2, route, counts, xa_new, mt[5], i, moe_w1, moe_b1[i], moe_w2, moe_b2[i],
                          next_norm=(norm1_g[i + 1], mods[i + 1][0], mods[i + 1][1]))
    return xa[:N_LAT].reshape(BATCH, SEQ, D_MODEL)
```

```python
import functools

import jax
import jax.numpy as jnp
from jax import lax
from jax.experimental import pallas as pl
from jax.experimental.pallas import tpu as pltpu

F32 = jnp.float32
BF16 = jnp.bfloat16

D_MODEL = 1024
BATCH = 8
SEQ = 2048
DEPTH = 2
GRID_W = 64
CTX_LEN = 256
N_LAT = BATCH * SEQ
N_CTX = BATCH * CTX_LEN
N_ALL = N_LAT + N_CTX

NA_HEADS = 16
NA_HEAD_DIM = 64
NA_KR = 8
NA_KC = 16
NA_ROWS = SEQ // GRID_W
NA_QROWS = 4
NA_BAND = 12
NA_QBLK = NA_QROWS * GRID_W
NA_KBLK = NA_BAND * GRID_W

SSD_D_INNER = 2048
SSD_HEAD_DIM = 64
SSD_HEADS = 32
SSD_GROUPS = 8
SSD_HPG = 4
SSD_STATE = 128
SSD_CONV_W = 5
SSD_CHUNK = 128
SSD_CONV_DIM = SSD_D_INNER + 2 * SSD_GROUPS * SSD_STATE
SSD_MAIN_W = SSD_D_INNER + SSD_CONV_DIM
SSD_GROUP_W = SSD_HPG * SSD_HEAD_DIM

N_EXPERTS = 32
TOP_K = 4
D_FF = 1024
SWIGLU_ALPHA = 1.702
SWIGLU_LIMIT = 7.0
MOE_BLOCK = 512
NORM_EPS = 1e-6
NEG_BIG = -1e30

VMEM_LIMIT = 56 * 1024 * 1024

_NT = (((1,), (1,)), ((), ()))
_TN = (((0,), (0,)), ((), ()))


def _cparams(sem):
    return pltpu.CompilerParams(dimension_semantics=sem, vmem_limit_bytes=VMEM_LIMIT)


def _mod_row(i, tm):
    return jnp.minimum((i * tm) // SEQ, BATCH)


def _split_bf16(x, pieces):
    out = []
    rem = x
    for _ in range(pieces):
        p = rem.astype(BF16)
        out.append(p)
        rem = rem - p.astype(F32)
    return out


def _dot_exact_rhs(x, sel, pieces, dims=None):
    acc = None
    for p in _split_bf16(x, pieces):
        if dims is None:
            t = jnp.dot(p, sel, preferred_element_type=F32)
        else:
            t = lax.dot_general(p, sel, dims, preferred_element_type=F32)
        acc = t if acc is None else acc + t
    return acc


def _dot_exact_lhs(sel, x, pieces):
    acc = None
    for p in _split_bf16(x, pieces):
        t = jnp.dot(sel, p, preferred_element_type=F32)
        acc = t if acc is None else acc + t
    return acc


def _normmod_kernel(x_ref, g_ref, sh_ref, sc_ref, o_ref):
    x = x_ref[...]
    ms = jnp.mean(x * x, axis=-1, keepdims=True)
    y = x * lax.rsqrt(ms + NORM_EPS) * g_ref[...]
    o_ref[...] = (y * (1.0 + sc_ref[0]) + sh_ref[0]).astype(o_ref.dtype)


ROUTE_EXPERT_LANE = 0
ROUTE_RANK_LANE = 4
ROUTE_GATE_LANE = 8


def _normmod_router_kernel(x_ref, g_ref, sh_ref, sc_ref, wr_ref, br_ref, o_ref, route_ref, count_ref):
    i = pl.program_id(0)
    x = x_ref[...]
    tm = x.shape[0]
    ms = jnp.mean(x * x, axis=-1, keepdims=True)
    y = x * lax.rsqrt(ms + NORM_EPS) * g_ref[...]
    h = y * (1.0 + sc_ref[0]) + sh_ref[0]
    o_ref[...] = h.astype(o_ref.dtype)
    h_hi, h_lo = _split_bf16(h, 2)
    w_hi, w_lo = _split_bf16(wr_ref[...], 2)
    logits = (jnp.dot(h_hi, w_hi, preferred_element_type=F32) + jnp.dot(h_hi, w_lo, preferred_element_type=F32)
              + jnp.dot(h_lo, w_hi, preferred_element_type=F32)) + br_ref[...]

    @pl.when(i == 0)
    def _():
        count_ref[...] = jnp.zeros_like(count_ref)

    lane_e = lax.broadcasted_iota(jnp.int32, logits.shape, 1).astype(F32)
    cur = logits
    vals, ids, hots = [], [], []
    for _ in range(TOP_K):
        m = jnp.max(cur, axis=-1, keepdims=True)
        idx = jnp.min(jnp.where(cur == m, lane_e, float(N_EXPERTS)), axis=-1, keepdims=True)
        hot = lane_e == idx
        vals.append(m)
        ids.append(idx)
        hots.append(hot)
        cur = jnp.where(hot, -jnp.inf, cur)
    exps = [jnp.exp(v - vals[0]) for v in vals]
    den = exps[0] + exps[1] + exps[2] + exps[3]
    picked = jnp.where(hots[0] | hots[1] | hots[2] | hots[3], 1.0, 0.0)
    row = lax.broadcasted_iota(jnp.int32, (tm, tm), 0)
    col = lax.broadcasted_iota(jnp.int32, (tm, tm), 1)
    earlier = jnp.where(col < row, 1.0, 0.0).astype(BF16)
    before = count_ref[...] + jnp.dot(earlier, picked.astype(BF16), preferred_element_type=F32)
    lane = lax.broadcasted_iota(jnp.int32, (tm, 128), 1)
    route = jnp.zeros((tm, 128), F32)
    for k in range(TOP_K):
        rank = jnp.sum(jnp.where(hots[k], before, 0.0), axis=-1, keepdims=True)
        route = jnp.where(lane == ROUTE_EXPERT_LANE + k, ids[k], route)
        route = jnp.where(lane == ROUTE_RANK_LANE + k, rank, route)
        route = jnp.where(lane == ROUTE_GATE_LANE + k, exps[k] / den, route)
    route_ref[...] = route
    count_ref[...] += jnp.sum(picked, axis=0, keepdims=True)


def _normmod(x, g, shift, scale, n_rows, router=None, tm=512):
    grid = (n_rows // tm,)
    x_spec = pl.BlockSpec((tm, D_MODEL), lambda i: (i, 0))
    g_spec = pl.BlockSpec((1, D_MODEL), lambda i: (0, 0))
    m_spec = pl.BlockSpec((1, 1, D_MODEL), lambda i: (_mod_row(i, tm), 0, 0))
    o_spec = pl.BlockSpec((tm, D_MODEL), lambda i: (i, 0))
    g2 = g.reshape(1, D_MODEL)
    if router is None:
        return pl.pallas_call(
            _normmod_kernel, grid=grid,
            in_specs=[x_spec, g_spec, m_spec, m_spec], out_specs=o_spec,
            out_shape=jax.ShapeDtypeStruct((n_rows, D_MODEL), BF16),
            compiler_params=_cparams(("arbitrary",)), name="normmod",
        )(x, g2, shift, scale)
    w_r, b_r = router
    return pl.pallas_call(
        _normmod_router_kernel, grid=grid,
        in_specs=[x_spec, g_spec, m_spec, m_spec,
                  pl.BlockSpec((D_MODEL, N_EXPERTS), lambda i: (0, 0)),
                  pl.BlockSpec((1, N_EXPERTS), lambda i: (0, 0))],
        out_specs=[o_spec, pl.BlockSpec((tm, 128), lambda i: (i, 0)),
                   pl.BlockSpec((1, N_EXPERTS), lambda i: (0, 0))],
        out_shape=[jax.ShapeDtypeStruct((n_rows, D_MODEL), F32),
                   jax.ShapeDtypeStruct((n_rows, 128), F32),
                   jax.ShapeDtypeStruct((1, N_EXPERTS), F32)],
        compiler_params=_cparams(("arbitrary",)), name="normmod_router",
    )(x, g2, shift, scale, w_r, b_r.reshape(1, N_EXPERTS))


def _mm_bias_f32_kernel(a_ref, w_ref, b_ref, o_ref):
    o_ref[...] = jnp.dot(a_ref[...], w_ref[...], preferred_element_type=F32,
                         precision=lax.Precision.HIGHEST) + b_ref[...]


def _matmul_bias_f32(a, w, bias, *, tn, name):
    m, k = a.shape
    n = w.shape[1]
    return pl.pallas_call(
        _mm_bias_f32_kernel, grid=(n // tn,),
        in_specs=[pl.BlockSpec((m, k), lambda j: (0, 0)),
                  pl.BlockSpec((k, tn), lambda j: (0, j)),
                  pl.BlockSpec((1, tn), lambda j: (0, j))],
        out_specs=pl.BlockSpec((m, tn), lambda j: (0, j)),
        out_shape=jax.ShapeDtypeStruct((m, n), F32),
        compiler_params=_cparams(("arbitrary",)), name=name,
    )(a, w, bias.reshape(1, n))


def _cast_weight_once(first, w_ref, wb_ref):
    @pl.when(first)
    def _():
        wb_ref[...] = w_ref[...].astype(BF16)


def _mm_kernel(a_ref, w_ref, o_ref, wb_ref):
    _cast_weight_once(pl.program_id(1) == 0, w_ref, wb_ref)
    o_ref[...] = jnp.dot(a_ref[...], wb_ref[...], preferred_element_type=F32).astype(o_ref.dtype)


def _mm_res_kernel(a_ref, w_ref, res_ref, gate_ref, o_ref, wb_ref):
    _cast_weight_once(pl.program_id(0) == 0, w_ref, wb_ref)
    acc = jnp.dot(a_ref[...], wb_ref[...], preferred_element_type=F32)
    o_ref[...] = res_ref[...] + gate_ref[0] * acc


def _matmul(a, w, *, tm, tn, n_cols, col0=0, out_dtype=F32, name="mm"):
    m, k = a.shape
    return pl.pallas_call(
        _mm_kernel, grid=(n_cols // tn, m // tm),
        in_specs=[pl.BlockSpec((tm, k), lambda j, i: (i, 0)),
                  pl.BlockSpec((k, tn), lambda j, i: (0, col0 + j))],
        out_specs=pl.BlockSpec((tm, tn), lambda j, i: (i, j)),
        out_shape=jax.ShapeDtypeStruct((m, n_cols), out_dtype),
        scratch_shapes=[pltpu.VMEM((k, tn), BF16)],
        compiler_params=_cparams(("arbitrary", "arbitrary")), name=name,
    )(a, w)


def _matmul_residual(a, w, res, gate, *, tm, name):
    m, k = a.shape
    n = w.shape[1]
    return pl.pallas_call(
        _mm_res_kernel, grid=(m // tm,),
        in_specs=[pl.BlockSpec((tm, k), lambda i: (i, 0)),
                  pl.BlockSpec((k, n), lambda i: (0, 0)),
                  pl.BlockSpec((tm, n), lambda i: (i, 0)),
                  pl.BlockSpec((1, 1, n), lambda i: (_mod_row(i, tm), 0, 0))],
        out_specs=pl.BlockSpec((tm, n), lambda i: (i, 0)),
        out_shape=jax.ShapeDtypeStruct((m, n), F32),
        scratch_shapes=[pltpu.VMEM((k, n), BF16)],
        compiler_params=_cparams(("arbitrary",)), name=name,
    )(a, w, res, gate)


def _qkv_kernel(a_ref, w_ref, gain_ref, seg_ref, segt_ref, o_ref, wb_ref):
    j = pl.program_id(0)
    _cast_weight_once(pl.program_id(1) == 0, w_ref, wb_ref)
    acc = jnp.dot(a_ref[...], wb_ref[...], preferred_element_type=F32)

    @pl.when(j < 2)
    def _():
        ss = _dot_exact_rhs(acc * acc, seg_ref[...], 2)
        inv = lax.rsqrt(ss * (1.0 / NA_HEAD_DIM) + NORM_EPS)
        o_ref[...] = (acc * _dot_exact_rhs(inv, segt_ref[...], 2) * gain_ref[0]).astype(o_ref.dtype)

    @pl.when(j == 2)
    def _():
        o_ref[...] = acc.astype(o_ref.dtype)


def _qkv_proj(h, w_qkv, q_g, k_g, tm=1024):
    m = h.shape[0]
    head_of_col = jnp.arange(D_MODEL) // NA_HEAD_DIM
    seg = (head_of_col[:, None] == jnp.arange(128)[None, :]).astype(BF16)
    gain = jnp.stack([jnp.tile(q_g, NA_HEADS) * NA_HEAD_DIM ** -0.5, jnp.tile(k_g, NA_HEADS),
                      jnp.ones((D_MODEL,), F32)]).reshape(3, 1, D_MODEL)
    return pl.pallas_call(
        _qkv_kernel, grid=(3, m // tm),
        in_specs=[pl.BlockSpec((tm, D_MODEL), lambda j, i: (i, 0)),
                  pl.BlockSpec((D_MODEL, D_MODEL), lambda j, i: (0, j)),
                  pl.BlockSpec((1, 1, D_MODEL), lambda j, i: (j, 0, 0)),
                  pl.BlockSpec((D_MODEL, 128), lambda j, i: (0, 0)),
                  pl.BlockSpec((128, D_MODEL), lambda j, i: (0, 0))],
        out_specs=pl.BlockSpec((tm, D_MODEL), lambda j, i: (i, j)),
        out_shape=jax.ShapeDtypeStruct((m, 3 * D_MODEL), BF16),
        scratch_shapes=[pltpu.VMEM((D_MODEL, D_MODEL), BF16)],
        compiler_params=_cparams(("arbitrary", "arbitrary")), name="qkv_proj",
    )(h, w_qkv, gain, seg, seg.T)


def _na_bias_table(rpb):
    col = jnp.arange(GRID_W)
    col_start = jnp.clip(col - NA_KC // 2, 0, GRID_W - NA_KC)
    col_ok = (col[None, :] >= col_start[:, None]) & (col[None, :] < col_start[:, None] + NA_KC)
    col_idx = jnp.clip(col[None, :] - col[:, None] + NA_KC - 1, 0, 2 * NA_KC - 2)
    by_offset = jnp.take(rpb, col_idx.reshape(-1), axis=2).reshape(NA_HEADS, 2 * NA_KR - 1, GRID_W, GRID_W)
    by_offset = jnp.where(col_ok[None, None], by_offset, NEG_BIG)
    type_blocks = (0, 1, NA_ROWS // NA_QROWS - 1)

    def assemble_kernel(src_ref, o_ref):
        for t, rb in enumerate(type_blocks):
            @pl.when(pl.program_id(0) == t)
            def _(rb=rb):
                r0 = rb * NA_QROWS
                ks = min(max(r0 - NA_KR // 2, 0), NA_ROWS - NA_BAND)
                for i in range(NA_QROWS):
                    r = r0 + i
                    rs = min(max(r - NA_KR // 2, 0), NA_ROWS - NA_KR)
                    for n in range(NA_BAND):
                        kr = ks + n
                        if rs <= kr < rs + NA_KR:
                            tile = src_ref[kr - r + NA_KR - 1]
                        else:
                            tile = jnp.full((GRID_W, GRID_W), NEG_BIG, F32)
                        o_ref[i * GRID_W:(i + 1) * GRID_W, n * GRID_W:(n + 1) * GRID_W] = tile

    return pl.pallas_call(
        assemble_kernel, grid=(len(type_blocks), NA_HEADS),
        in_specs=[pl.BlockSpec((None, 2 * NA_KR - 1, GRID_W, GRID_W), lambda t, h: (h, 0, 0, 0))],
        out_specs=pl.BlockSpec((None, None, NA_QBLK, NA_KBLK), lambda t, h: (t, h, 0, 0)),
        out_shape=jax.ShapeDtypeStruct((len(type_blocks), NA_HEADS, NA_QBLK, NA_KBLK), F32),
        compiler_params=_cparams(("arbitrary", "arbitrary")), name="na_bias_table",
    )(by_offset)


def _softmax_pv(s_list, v_list):
    m = s_list[0].max(axis=-1, keepdims=True)
    for s in s_list[1:]:
        m = jnp.maximum(m, s.max(axis=-1, keepdims=True))
    acc = None
    den = None
    for s, v in zip(s_list, v_list):
        p = jnp.exp(s - m)
        l = p.sum(axis=-1, keepdims=True)
        o = jnp.dot(p.astype(BF16), v, preferred_element_type=F32)
        acc = o if acc is None else acc + o
        den = l if den is None else den + l
    return acc / den


NA_LAT_STEPS = NA_ROWS // NA_QROWS


NA_STEP_HEADS = 8
NA_STEP_LANES = NA_STEP_HEADS * NA_HEAD_DIM


def _na_kernel(q_ref, k_ref, v_ref, kc_ref, vc_ref, tbl_ref, o_ref):
    rb = pl.program_id(2)
    lane = lax.broadcasted_iota(jnp.int32, (NA_QBLK, 2 * NA_HEAD_DIM), 1)

    def pair_lanes(pair):
        return slice(pair * 2 * NA_HEAD_DIM, (pair + 1) * 2 * NA_HEAD_DIM)

    def head_queries(q2, sub):
        sel = (lane < NA_HEAD_DIM) if sub == 0 else (lane >= NA_HEAD_DIM)
        return jnp.where(sel, q2, jnp.zeros_like(q2))

    @pl.when(rb < NA_LAT_STEPS)
    def _():
        ks = jnp.clip(rb * NA_QROWS - NA_KR // 2, 0, NA_ROWS - NA_BAND)
        kstart = pl.multiple_of(ks * GRID_W, GRID_W)
        for pair in range(NA_STEP_HEADS // 2):
            pl_ = pair_lanes(pair)
            q2 = q_ref[:, pl_]
            kb = k_ref[pl.ds(kstart, NA_KBLK), pl_]
            vb = v_ref[pl.ds(kstart, NA_KBLK), pl_]
            kc = kc_ref[:, pl_]
            vc = vc_ref[:, pl_]
            outs = []
            for sub in range(2):
                qm = head_queries(q2, sub)
                s_lat = lax.dot_general(qm, kb, _NT, preferred_element_type=F32) + tbl_ref[0, 2 * pair + sub]
                s_ctx = lax.dot_general(qm, kc, _NT, preferred_element_type=F32)
                outs.append(_softmax_pv([s_lat, s_ctx], [vb, vc]))
            o_ref[:, pl_] = jnp.where(lane < NA_HEAD_DIM, outs[0], outs[1]).astype(o_ref.dtype)

    @pl.when(rb == NA_LAT_STEPS)
    def _():
        for pair in range(NA_STEP_HEADS // 2):
            pl_ = pair_lanes(pair)
            q2 = q_ref[:, pl_]
            kc = kc_ref[:, pl_]
            vc = vc_ref[:, pl_]
            outs = []
            for sub in range(2):
                s_ctx = lax.dot_general(head_queries(q2, sub), kc, _NT, preferred_element_type=F32)
                outs.append(_softmax_pv([s_ctx], [vc]))
            o_ref[:, pl_] = jnp.where(lane < NA_HEAD_DIM, outs[0], outs[1]).astype(o_ref.dtype)


def _na_block_type(rb):
    return jnp.where(rb == 0, 0, jnp.where(rb >= NA_LAT_STEPS - 1, 2, 1))


def _neighbourhood_attention(qkv, table):
    groups = NA_HEADS // NA_STEP_HEADS
    ctx_blk0 = N_LAT // CTX_LEN

    def q_block(b, h, r):
        return (jnp.where(r < NA_LAT_STEPS, b * NA_LAT_STEPS + r, ctx_blk0 + b), h)

    return pl.pallas_call(
        _na_kernel, grid=(BATCH, groups, NA_LAT_STEPS + 1),
        in_specs=[
            pl.BlockSpec((NA_QBLK, NA_STEP_LANES), q_block),
            pl.BlockSpec((SEQ, NA_STEP_LANES), lambda b, h, r: (b, groups + h)),
            pl.BlockSpec((SEQ, NA_STEP_LANES), lambda b, h, r: (b, 2 * groups + h)),
            pl.BlockSpec((CTX_LEN, NA_STEP_LANES), lambda b, h, r: (ctx_blk0 + b, groups + h)),
            pl.BlockSpec((CTX_LEN, NA_STEP_LANES), lambda b, h, r: (ctx_blk0 + b, 2 * groups + h)),
            pl.BlockSpec((1, NA_STEP_HEADS, NA_QBLK, NA_KBLK), lambda b, h, r: (_na_block_type(r), h, 0, 0)),
        ],
        out_specs=pl.BlockSpec((NA_QBLK, NA_STEP_LANES), q_block),
        out_shape=jax.ShapeDtypeStruct((N_ALL, D_MODEL), BF16),
        compiler_params=_cparams(("arbitrary", "arbitrary", "arbitrary")), name="na_attention",
    )(qkv, qkv, qkv, qkv, qkv, table)


CONV_ROWS = 256
CONV_COLS = 2048
CONV_HALO = 8


def _conv_silu_kernel(cur_ref, prev_ref, next_ref, w_ref, b_ref, o_ref):
    i = pl.program_id(0)
    tiles_per_seq = SEQ // CONV_ROWS
    is_ctx = i >= N_LAT // CONV_ROWS
    first = jnp.logical_or(is_ctx, i % tiles_per_seq == 0)
    last = jnp.logical_or(is_ctx, i % tiles_per_seq == tiles_per_seq - 1)
    prev = jnp.where(first, 0.0, prev_ref[...])
    nxt = jnp.where(last, 0.0, next_ref[...])
    cur = cur_ref[...]
    pad = SSD_CONV_W // 2

    def taps(slab, rows):
        acc = jnp.zeros((rows, slab.shape[1]), F32) + b_ref[...]
        for t in range(SSD_CONV_W):
            lo = CONV_HALO - pad + t
            acc = acc + slab[lo:lo + rows, :] * w_ref[t:t + 1, :]
        return acc

    acc = jnp.zeros(o_ref.shape, F32) + b_ref[...]
    for t in range(SSD_CONV_W):
        shifted = cur if t == pad else pltpu.roll(cur, (pad - t) % CONV_ROWS, axis=0)
        acc = acc + shifted * w_ref[t:t + 1, :]
    top = taps(jnp.concatenate([prev, cur[:2 * CONV_HALO]], axis=0), CONV_HALO)
    bottom = taps(jnp.concatenate([cur[CONV_ROWS - 2 * CONV_HALO:], nxt], axis=0), CONV_HALO)
    acc = jnp.concatenate([top, acc[CONV_HALO:CONV_ROWS - CONV_HALO], bottom], axis=0)
    o_ref[...] = acc * jax.nn.sigmoid(acc)


def _conv_silu(zxbc, conv_w, conv_b):
    col0 = SSD_D_INNER // CONV_COLS
    n_halo_blocks = N_ALL // CONV_HALO
    per_tile = CONV_ROWS // CONV_HALO
    return pl.pallas_call(
        _conv_silu_kernel, grid=(N_ALL // CONV_ROWS, SSD_CONV_DIM // CONV_COLS),
        in_specs=[
            pl.BlockSpec((CONV_ROWS, CONV_COLS), lambda i, c: (i, col0 + c)),
            pl.BlockSpec((CONV_HALO, CONV_COLS), lambda i, c: (jnp.maximum(i * per_tile - 1, 0), col0 + c)),
            pl.BlockSpec((CONV_HALO, CONV_COLS),
                         lambda i, c: (jnp.minimum((i + 1) * per_tile, n_halo_blocks - 1), col0 + c)),
            pl.BlockSpec((SSD_CONV_W, CONV_COLS), lambda i, c: (0, c)),
            pl.BlockSpec((1, CONV_COLS), lambda i, c: (0, c)),
        ],
        out_specs=pl.BlockSpec((CONV_ROWS, CONV_COLS), lambda i, c: (i, c)),
        out_shape=jax.ShapeDtypeStruct((N_ALL, SSD_CONV_DIM), F32),
        compiler_params=_cparams(("arbitrary", "arbitrary")), name="ssd_conv_silu",
    )(zxbc, zxbc, zxbc, conv_w, conv_b.reshape(1, SSD_CONV_DIM))


SSD_STEPS = (CTX_LEN + SEQ) // SSD_CHUNK
SSD_CTX_CHUNKS = CTX_LEN // SSD_CHUNK
SSD_LAT_CHUNKS = SEQ // SSD_CHUNK


def _ssd_chunk(b, d, j):
    jj = jnp.where(d == 0, j, jnp.where(j < SSD_CTX_CHUNKS, SSD_CTX_CHUNKS - 1 - j,
                                        SSD_STEPS + SSD_CTX_CHUNKS - 1 - j))
    return jnp.where(jj < SSD_CTX_CHUNKS,
                     N_LAT // SSD_CHUNK + SSD_CTX_CHUNKS * b + jj,
                     SSD_LAT_CHUNKS * b + jj - SSD_CTX_CHUNKS)


def _ssd_kernel(xs_ref, b_ref, c_ref, dtr_ref, bias_ref, alog_ref, expand_ref, y_ref, state_ref):
    d = pl.program_id(1)
    j = pl.program_id(2)

    @pl.when(j == 0)
    def _():
        state_ref[...] = jnp.zeros_like(state_ref)

    fwd = d == 0
    dtr = dtr_ref[...]
    dt_pre = jnp.where(fwd, dtr[:, :SSD_HEADS], dtr[:, SSD_HEADS:]) + bias_ref[...]
    dt = jnp.maximum(dt_pre, 0.0) + jnp.log1p(jnp.exp(-jnp.abs(dt_pre)))
    da = dt * (-jnp.exp(alog_ref[...]))
    li = lax.broadcasted_iota(jnp.int32, (SSD_CHUNK, SSD_CHUNK), 0)
    si = lax.broadcasted_iota(jnp.int32, (SSD_CHUNK, SSD_CHUNK), 1)
    before = jnp.where(fwd, li - si, si - li) >= 0
    before_t = jnp.where(fwd, si - li, li - si) >= 0
    cum = _dot_exact_lhs(jnp.where(before, 1.0, 0.0).astype(BF16), da, 3)
    cum_t = _dot_exact_rhs(da, jnp.where(before_t, 1.0, 0.0).astype(BF16), 3, _TN)
    dt_t = _dot_exact_rhs(dt, jnp.where(li == si, 1.0, 0.0).astype(BF16), 2, _TN)
    tot = jnp.where(fwd, cum[SSD_CHUNK - 1:SSD_CHUNK, :], cum[0:1, :])
    expand = expand_ref[...]
    to_end = _dot_exact_rhs(dt * jnp.exp(tot - cum), expand, 2)
    from_start = _dot_exact_rhs(jnp.exp(cum), expand, 2)
    carry = jnp.where(fwd, from_start[SSD_CHUNK - 1:SSD_CHUNK, :], from_start[0:1, :])
    xs = xs_ref[...]
    x16 = xs.astype(BF16)
    xdec = (xs * to_end).astype(BF16)
    lane = lax.broadcasted_iota(jnp.int32, (SSD_CHUNK, 2 * SSD_HEAD_DIM), 1)
    for g in range(SSD_GROUPS):
        gs = slice(g * SSD_GROUP_W, (g + 1) * SSD_GROUP_W)
        bg = b_ref[:, g * SSD_STATE:(g + 1) * SSD_STATE].astype(BF16)
        cg = c_ref[:, g * SSD_STATE:(g + 1) * SSD_STATE].astype(BF16)
        cb = lax.dot_general(cg, bg, _NT, preferred_element_type=F32)
        st = state_ref[g]
        y_off = jnp.dot(cg, st.astype(BF16), preferred_element_type=F32) * from_start[:, gs]
        y_pairs = []
        for pr in range(SSD_HPG // 2):
            ms = []
            for r2 in range(2):
                h = g * SSD_HPG + pr * 2 + r2
                decay = jnp.exp(jnp.where(before, cum[:, h:h + 1] - cum_t[h:h + 1, :], NEG_BIG))
                ms.append((cb * decay * dt_t[h:h + 1, :]).astype(BF16))
            m2 = jnp.concatenate(ms, axis=1)
            c0 = g * SSD_GROUP_W + pr * 2 * SSD_HEAD_DIM
            slab = x16[:, c0:c0 + 2 * SSD_HEAD_DIM]
            zero = jnp.zeros_like(slab)
            xdiag = jnp.concatenate([jnp.where(lane < SSD_HEAD_DIM, slab, zero),
                                     jnp.where(lane >= SSD_HEAD_DIM, slab, zero)], axis=0)
            y_pairs.append(jnp.dot(m2, xdiag, preferred_element_type=F32))
        y_ref[:, gs] = jnp.concatenate(y_pairs, axis=1) + y_off
        s_new = lax.dot_general(bg, xdec[:, gs], _TN, preferred_element_type=F32)
        state_ref[g] = st * carry[:, gs] + s_new


def _ssd_scan(xbc, dt_raw, dt_bias, a_log):
    expand = (jnp.arange(SSD_HEADS)[:, None] == (jnp.arange(SSD_D_INNER) // SSD_HEAD_DIM)[None, :]).astype(BF16)
    bc_w = SSD_GROUPS * SSD_STATE
    b_col = SSD_D_INNER // bc_w
    return pl.pallas_call(
        _ssd_kernel, grid=(BATCH, 2, SSD_STEPS),
        in_specs=[
            pl.BlockSpec((SSD_CHUNK, SSD_D_INNER), lambda b, d, j: (_ssd_chunk(b, d, j), 0)),
            pl.BlockSpec((SSD_CHUNK, bc_w), lambda b, d, j: (_ssd_chunk(b, d, j), b_col)),
            pl.BlockSpec((SSD_CHUNK, bc_w), lambda b, d, j: (_ssd_chunk(b, d, j), b_col + 1)),
            pl.BlockSpec((SSD_CHUNK, 2 * SSD_HEADS), lambda b, d, j: (_ssd_chunk(b, d, j), 0)),
            pl.BlockSpec((None, 1, SSD_HEADS), lambda b, d, j: (d, 0, 0)),
            pl.BlockSpec((None, 1, SSD_HEADS), lambda b, d, j: (d, 0, 0)),
            pl.BlockSpec((SSD_HEADS, SSD_D_INNER), lambda b, d, j: (0, 0)),
        ],
        out_specs=pl.BlockSpec((None, SSD_CHUNK, SSD_D_INNER), lambda b, d, j: (d, _ssd_chunk(b, d, j), 0)),
        out_shape=jax.ShapeDtypeStruct((2, N_ALL, SSD_D_INNER), F32),
        scratch_shapes=[pltpu.VMEM((SSD_GROUPS, SSD_STATE, SSD_GROUP_W), F32)],
        compiler_params=_cparams(("arbitrary", "arbitrary", "arbitrary")), name="ssd_scan",
    )(xbc, xbc, xbc, dt_raw, dt_bias.reshape(2, 1, SSD_HEADS), a_log.reshape(2, 1, SSD_HEADS), expand)


def _ssd_out_kernel(yf_ref, yb_ref, xs_ref, z_ref, dskip_ref, g_ref, w_ref, res_ref, gate_ref, o_ref,
                    wb_ref, gated_ref):
    _cast_weight_once(pl.program_id(0) == 0, w_ref, wb_ref)
    z = z_ref[...]
    y = (yf_ref[...] + yb_ref[...] + dskip_ref[...] * xs_ref[...]) * (z * jax.nn.sigmoid(z))
    for grp in range(SSD_GROUPS):
        gs = slice(grp * SSD_GROUP_W, (grp + 1) * SSD_GROUP_W)
        yg = y[:, gs]
        ms = jnp.mean(yg * yg, axis=-1, keepdims=True)
        gated_ref[:, gs] = (yg * lax.rsqrt(ms + NORM_EPS) * g_ref[:, gs]).astype(BF16)
    acc = jnp.dot(gated_ref[...], wb_ref[...], preferred_element_type=F32)
    o_ref[...] = res_ref[...] + gate_ref[0] * acc


def _ssd_out_proj(y2, xbc, zxbc, d_skip, norm_g, w_out, res, gate, n_out, tm=256):
    d_cols = jnp.repeat(d_skip, SSD_HEAD_DIM).reshape(1, SSD_D_INNER)
    return pl.pallas_call(
        _ssd_out_kernel, grid=(n_out // tm,),
        in_specs=[
            pl.BlockSpec((None, tm, SSD_D_INNER), lambda i: (0, i, 0)),
            pl.BlockSpec((None, tm, SSD_D_INNER), lambda i: (1, i, 0)),
            pl.BlockSpec((tm, SSD_D_INNER), lambda i: (i, 0)),
            pl.BlockSpec((tm, SSD_D_INNER), lambda i: (i, 0)),
            pl.BlockSpec((1, SSD_D_INNER), lambda i: (0, 0)),
            pl.BlockSpec((1, SSD_D_INNER), lambda i: (0, 0)),
            pl.BlockSpec((SSD_D_INNER, D_MODEL), lambda i: (0, 0)),
            pl.BlockSpec((tm, D_MODEL), lambda i: (i, 0)),
            pl.BlockSpec((1, 1, D_MODEL), lambda i: (_mod_row(i, tm), 0, 0)),
        ],
        out_specs=pl.BlockSpec((tm, D_MODEL), lambda i: (i, 0)),
        out_shape=jax.ShapeDtypeStruct((n_out, D_MODEL), F32),
        scratch_shapes=[pltpu.VMEM((SSD_D_INNER, D_MODEL), BF16), pltpu.VMEM((tm, SSD_D_INNER), BF16)],
        compiler_params=_cparams(("arbitrary",)), name="ssd_out_proj",
    )(y2, y2, xbc, zxbc, d_cols, norm_g.reshape(1, SSD_D_INNER), w_out, res, gate)


DISPATCH_TOKENS = 1024
COMBINE_TOKENS = 256
ZERO_FILL_ROWS = MOE_BLOCK + 8


def _dispatch_kernel(dest_ref, padstart_ref, nreal_ref, h_ref, xb_ref, zeros_ref, sem):
    i = pl.program_id(0)
    base = i * (DISPATCH_TOKENS * TOP_K)

    @pl.when(i == 0)
    def _():
        zeros_ref[...] = jnp.zeros_like(zeros_ref)
        n_slots = xb_ref.shape[0]

        def fill(e):
            start = jnp.minimum(padstart_ref[e] // 8 * 8, n_slots - ZERO_FILL_ROWS)
            return pltpu.make_async_copy(zeros_ref, xb_ref.at[pl.ds(pl.multiple_of(start, 8), ZERO_FILL_ROWS)],
                                         sem.at[0])

        for e in range(N_EXPERTS):
            fill(e).start()
        for e in range(N_EXPERTS):
            fill(e).wait()

        def fill_block(blk):
            return pltpu.make_async_copy(zeros_ref.at[pl.ds(0, MOE_BLOCK)],
                                         xb_ref.at[pl.ds(pl.multiple_of(blk * MOE_BLOCK, MOE_BLOCK), MOE_BLOCK)],
                                         sem.at[0])

        def start_block(blk, carry):
            fill_block(blk).start()
            return carry

        def wait_block(blk, carry):
            fill_block(blk).wait()
            return carry

        lax.fori_loop(nreal_ref[0], n_slots // MOE_BLOCK, start_block, 0)
        lax.fori_loop(nreal_ref[0], n_slots // MOE_BLOCK, wait_block, 0)

    def body(t, carry):
        for k in range(TOP_K):
            pltpu.make_async_copy(h_ref.at[pl.ds(t, 1)], xb_ref.at[pl.ds(dest_ref[base + t * TOP_K + k], 1)],
                                  sem.at[0]).start()
        return carry

    lax.fori_loop(0, DISPATCH_TOKENS, body, 0, unroll=4)
    for k in range(TOP_K):
        pltpu.make_async_copy(h_ref, xb_ref.at[pl.ds(0, DISPATCH_TOKENS)], sem.at[0]).wait()


def _moe_dispatch(h, dest, pad_start, n_real, n_slots):
    n_tok = h.shape[0]
    tt = DISPATCH_TOKENS
    grid_spec = pltpu.PrefetchScalarGridSpec(
        num_scalar_prefetch=3, grid=(n_tok // tt,),
        in_specs=[pl.BlockSpec((tt, D_MODEL), lambda i, d, p, nr: (i, 0))],
        out_specs=pl.BlockSpec(memory_space=pl.ANY),
        scratch_shapes=[pltpu.VMEM((ZERO_FILL_ROWS, D_MODEL), F32), pltpu.SemaphoreType.DMA((1,))],
    )
    return pl.pallas_call(
        _dispatch_kernel, grid_spec=grid_spec,
        out_shape=jax.ShapeDtypeStruct((n_slots, D_MODEL), F32),
        compiler_params=_cparams(("arbitrary",)), name="moe_dispatch",
    )(dest, pad_start, n_real, h)


def _moe_kernel(be_ref, nreal_ref, first_ref, slot_ref, next_ref, x_ref, w1_hbm, b1_ref, w2_hbm, b2_ref, o_ref,
                w1f_ref, w2f_ref, w1b_ref, w2b_ref, sem, *, layer):
    i = pl.program_id(0)

    def weight_copies(e, slot):
        return (pltpu.make_async_copy(w1_hbm.at[layer, e], w1f_ref.at[slot], sem.at[slot]),
                pltpu.make_async_copy(w2_hbm.at[layer, e], w2f_ref.at[slot], sem.at[slot]))

    @pl.when(i < nreal_ref[0])
    def _():
        @pl.when(i == 0)
        def _():
            for c in weight_copies(be_ref[0], 0):
                c.start()

        @pl.when(first_ref[i] == 1)
        def _():
            slot = slot_ref[i]
            for c in weight_copies(be_ref[i], slot):
                c.wait()
            w1b_ref[...] = w1f_ref[slot].astype(BF16)
            w2b_ref[...] = w2f_ref[slot].astype(BF16)

            @pl.when(next_ref[i] >= 0)
            def _():
                for c in weight_copies(next_ref[i], 1 - slot):
                    c.start()

        h = jnp.dot(x_ref[...].astype(BF16), w1b_ref[...], preferred_element_type=F32) + b1_ref[0]
        glu = jnp.minimum(h[:, :D_FF], SWIGLU_LIMIT)
        lin = jnp.clip(h[:, D_FF:], -SWIGLU_LIMIT, SWIGLU_LIMIT)
        act = glu * jax.nn.sigmoid(SWIGLU_ALPHA * glu) * (lin + 1.0)
        o_ref[...] = jnp.dot(act.astype(BF16), w2b_ref[...], preferred_element_type=F32) + b2_ref[0]

    @pl.when(i >= nreal_ref[0])
    def _():
        o_ref[...] = jnp.zeros_like(o_ref)


def _moe_experts(xb, block_expert, n_real, pend, layer, w1, b1, w2, b2):
    n_slots = xb.shape[0]
    n_blocks = n_slots // MOE_BLOCK
    first = jnp.concatenate([jnp.ones((1,), jnp.int32),
                             (block_expert[1:] != block_expert[:-1]).astype(jnp.int32)])
    slot = (jnp.cumsum(first) - 1) % 2
    next_block = pend[block_expert] // MOE_BLOCK
    next_expert = jnp.where(next_block < n_real[0], block_expert[jnp.minimum(next_block, n_blocks - 1)], -1)
    grid_spec = pltpu.PrefetchScalarGridSpec(
        num_scalar_prefetch=5, grid=(n_blocks,),
        in_specs=[
            pl.BlockSpec((MOE_BLOCK, D_MODEL), lambda i, be, nr, *_: (jnp.minimum(i, nr[0] - 1), 0)),
            pl.BlockSpec(memory_space=pl.ANY),
            pl.BlockSpec((1, 1, 2 * D_FF), lambda i, be, *_: (be[i], 0, 0)),
            pl.BlockSpec(memory_space=pl.ANY),
            pl.BlockSpec((1, 1, D_MODEL), lambda i, be, *_: (be[i], 0, 0)),
        ],
        out_specs=pl.BlockSpec((MOE_BLOCK, D_MODEL), lambda i, *_: (i, 0)),
        scratch_shapes=[pltpu.VMEM((2, D_MODEL, 2 * D_FF), F32), pltpu.VMEM((2, D_FF, D_MODEL), F32),
                        pltpu.VMEM((D_MODEL, 2 * D_FF), BF16), pltpu.VMEM((D_FF, D_MODEL), BF16),
                        pltpu.SemaphoreType.DMA((2,))],
    )
    return pl.pallas_call(
        functools.partial(_moe_kernel, layer=layer), grid_spec=grid_spec,
        out_shape=jax.ShapeDtypeStruct((n_slots, D_MODEL), F32),
        compiler_params=_cparams(("arbitrary",)), name="moe_experts",
    )(block_expert, n_real, first, slot.astype(jnp.int32), next_expert.astype(jnp.int32), xb, w1,
      b1.reshape(N_EXPERTS, 1, 2 * D_FF), w2, b2.reshape(N_EXPERTS, 1, D_MODEL))


def _combine_tile(dest_ref, y_hbm, route_ref, res_ref, gate_ref, buf, sem):
    i = pl.program_id(0)
    n = pl.num_programs(0)

    def issue(tile, slot):
        base = tile * (COMBINE_TOKENS * TOP_K)

        def body(t, carry):
            for k in range(TOP_K):
                pltpu.make_async_copy(y_hbm.at[pl.ds(dest_ref[base + t * TOP_K + k], 1)],
                                      buf.at[slot, k, pl.ds(t, 1)], sem.at[slot]).start()
            return carry

        lax.fori_loop(0, COMBINE_TOKENS, body, 0, unroll=4)

    @pl.when(i == 0)
    def _():
        issue(0, 0)

    @pl.when(i + 1 < n)
    def _():
        issue(i + 1, (i + 1) % 2)

    slot = i % 2
    for k in range(TOP_K):
        pltpu.make_async_copy(y_hbm.at[pl.ds(0, COMBINE_TOKENS)], buf.at[slot, k], sem.at[slot]).wait()
    route = route_ref[...]
    y = None
    for k in range(TOP_K):
        t = route[:, ROUTE_GATE_LANE + k:ROUTE_GATE_LANE + k + 1] * buf[slot, k]
        y = t if y is None else y + t
    return res_ref[...] + gate_ref[0] * y


def _combine_kernel(dest_ref, y_hbm, route_ref, res_ref, gate_ref, o_ref, buf, sem):
    o_ref[...] = _combine_tile(dest_ref, y_hbm, route_ref, res_ref, gate_ref, buf, sem)


def _combine_norm_kernel(dest_ref, y_hbm, route_ref, res_ref, gate_ref, g_ref, sh_ref, sc_ref, o_ref, h_ref,
                         buf, sem):
    x = _combine_tile(dest_ref, y_hbm, route_ref, res_ref, gate_ref, buf, sem)
    o_ref[...] = x
    ms = jnp.mean(x * x, axis=-1, keepdims=True)
    y = x * lax.rsqrt(ms + NORM_EPS) * g_ref[...]
    h_ref[...] = (y * (1.0 + sc_ref[0]) + sh_ref[0]).astype(h_ref.dtype)


def _moe_combine(yb, dest, route, res, gate, next_norm=None):
    n_tok = route.shape[0]
    tt = COMBINE_TOKENS
    row_spec = pl.BlockSpec((tt, D_MODEL), lambda i, d: (i, 0))
    mod_spec = pl.BlockSpec((1, 1, D_MODEL), lambda i, d: (_mod_row(i, tt), 0, 0))
    in_specs = [pl.BlockSpec(memory_space=pl.ANY), pl.BlockSpec((tt, 128), lambda i, d: (i, 0)), row_spec, mod_spec]
    args = [dest, yb, route, res, gate]
    x_shape = jax.ShapeDtypeStruct((n_tok, D_MODEL), F32)
    if next_norm is None:
        body, out_specs, out_shape = _combine_kernel, row_spec, x_shape
    else:
        g, shift, scale = next_norm
        body = _combine_norm_kernel
        in_specs += [pl.BlockSpec((1, D_MODEL), lambda i, d: (0, 0)), mod_spec, mod_spec]
        args += [g.reshape(1, D_MODEL), shift, scale]
        out_specs = [row_spec, row_spec]
        out_shape = [x_shape, jax.ShapeDtypeStruct((n_tok, D_MODEL), BF16)]
    grid_spec = pltpu.PrefetchScalarGridSpec(
        num_scalar_prefetch=1, grid=(n_tok // tt,), in_specs=in_specs, out_specs=out_specs,
        scratch_shapes=[pltpu.VMEM((2, TOP_K, tt, D_MODEL), F32), pltpu.SemaphoreType.DMA((2,))],
    )
    return pl.pallas_call(
        body, grid_spec=grid_spec, out_shape=out_shape,
        compiler_params=_cparams(("arbitrary",)), name="moe_combine",
    )(*args)


def _moe(h, route, counts, res, gate, layer, w1, b1, w2, b2, next_norm=None):
    n_tok = h.shape[0]
    nk = n_tok * TOP_K
    n_blocks = -(-(nk + N_EXPERTS * (MOE_BLOCK - 1)) // MOE_BLOCK)
    n_slots = n_blocks * MOE_BLOCK
    expert = route[:, ROUTE_EXPERT_LANE:ROUTE_EXPERT_LANE + TOP_K].astype(jnp.int32)
    rank = route[:, ROUTE_RANK_LANE:ROUTE_RANK_LANE + TOP_K].astype(jnp.int32)
    counts = counts.reshape(N_EXPERTS).astype(jnp.int32)
    padded = (counts + MOE_BLOCK - 1) // MOE_BLOCK * MOE_BLOCK
    pend = jnp.cumsum(padded)
    pstart = pend - padded
    ids = jnp.arange(N_EXPERTS, dtype=jnp.int32)
    dest = rank + jnp.sum(jnp.where(expert[..., None] == ids, pstart, 0), axis=-1)
    dest = dest.reshape(nk).astype(jnp.int32)
    block_start = jnp.arange(n_blocks, dtype=jnp.int32) * MOE_BLOCK
    block_expert = jnp.minimum(jnp.sum((pend[None, :] <= block_start[:, None]).astype(jnp.int32), axis=1),
                               N_EXPERTS - 1).astype(jnp.int32)
    n_real = (pend[-1] // MOE_BLOCK).astype(jnp.int32).reshape(1)
    pad_start = (pstart + counts).astype(jnp.int32)
    xb = _moe_dispatch(h, dest, pad_start, n_real, n_slots)
    yb = _moe_experts(xb, block_expert, n_real, pend.astype(jnp.int32), layer, w1, b1, w2, b2)
    return _moe_combine(yb, dest, route, res, gate, next_norm)


def kernel(x, c, ctx, c_ctx, ada_w, ada_b, norm1_g, norm2_g, na_w_qkv, na_q_g, na_k_g, na_rpb, na_w_o,
           ssd_w_in, ssd_conv_w, ssd_conv_b, ssd_dt_bias, ssd_a_log, ssd_d, ssd_norm_g, ssd_w_out,
           moe_w_router, moe_b_router, moe_w1, moe_b1, moe_w2, moe_b2):
    xa = jnp.concatenate([x.reshape(N_LAT, D_MODEL), ctx.reshape(N_CTX, D_MODEL)], axis=0)
    cond = jnp.concatenate([jax.nn.silu(c), jax.nn.silu(c_ctx)[None, :],
                            jnp.zeros((16 - BATCH - 1, D_MODEL), F32)], axis=0)
    mods = []
    for i in range(DEPTH):
        mod = _matmul_bias_f32(cond, ada_w[i], ada_b[i], tn=D_MODEL, name="ada_mod")
        mods.append([mod[:, t * D_MODEL:(t + 1) * D_MODEL].reshape(16, 1, D_MODEL) for t in range(6)])
    h1 = _normmod(xa, norm1_g[0], mods[0][0], mods[0][1], N_ALL)
    for i in range(DEPTH):
        last = i == DEPTH - 1
        mt = mods[i]
        n_out = N_LAT if last else N_ALL
        j = i // 2
        if i % 2 == 0:
            qkv = _qkv_proj(h1, na_w_qkv[j], na_q_g[j], na_k_g[j])
            o_all = _neighbourhood_attention(qkv, _na_bias_table(na_rpb[j]))
            xa_new = _matmul_residual(o_all[:n_out], na_w_o[j], xa, mt[2], tm=1024, name="na_out_proj")
        else:
            w_in = ssd_w_in[j]
            zxbc = _matmul(h1, w_in, tm=1024, tn=1024, n_cols=SSD_MAIN_W, name="ssd_in_proj")
            dt_raw = _matmul(h1, w_in[:, SSD_MAIN_W:], tm=512, tn=2 * SSD_HEADS, n_cols=2 * SSD_HEADS,
                             name="ssd_dt_proj")
            xbc = _conv_silu(zxbc, ssd_conv_w[j], ssd_conv_b[j])
            y2 = _ssd_scan(xbc, dt_raw, ssd_dt_bias[j], ssd_a_log[j])
            xa_new = _ssd_out_proj(y2, xbc, zxbc, ssd_d[j], ssd_norm_g[j], ssd_w_out[j], xa, mt[2], n_out)
        n_tok = xa_new.shape[0]
        h2, route, counts = _normmod(xa_new, norm2_g[i], mt[3], mt[4], n_tok,
                                     router=(moe_w_router[i], moe_b_router[i]))
        if last:
            xa = _moe(h2, route, counts, xa_new, mt[5], i, moe_w1, moe_b1[i], moe_w2, moe_b2[i])
        else:
            xa, h1 = _moe(h2, route, counts, xa_new, mt[5], i, moe_w1, moe_b1[i], moe_w2, moe_b2[i],
                          next_norm=(norm1_g[i + 1], mods[i + 1][0], mods[i + 1][1]))
    return xa[:N_LAT].reshape(BATCH, SEQ, D_MODEL)
```

```python
import functools

import jax
import jax.numpy as jnp
from jax import lax
from jax.experimental import pallas as pl
from jax.experimental.pallas import tpu as pltpu

F32 = jnp.float32
BF16 = jnp.bfloat16

D_MODEL = 1024
BATCH = 8
SEQ = 2048
DEPTH = 2
GRID_W = 64
CTX_LEN = 256
N_LAT = BATCH * SEQ
N_CTX = BATCH * CTX_LEN
N_ALL = N_LAT + N_CTX

NA_HEADS = 16
NA_HEAD_DIM = 64
NA_KR = 8
NA_KC = 16
NA_ROWS = SEQ // GRID_W
NA_QROWS = 4
NA_BAND = 12
NA_QBLK = NA_QROWS * GRID_W
NA_KBLK = NA_BAND * GRID_W

SSD_D_INNER = 2048
SSD_HEAD_DIM = 64
SSD_HEADS = 32
SSD_GROUPS = 8
SSD_HPG = 4
SSD_STATE = 128
SSD_CONV_W = 5
SSD_CHUNK = 128
SSD_CONV_DIM = SSD_D_INNER + 2 * SSD_GROUPS * SSD_STATE
SSD_MAIN_W = SSD_D_INNER + SSD_CONV_DIM
SSD_GROUP_W = SSD_HPG * SSD_HEAD_DIM

N_EXPERTS = 32
TOP_K = 4
D_FF = 1024
SWIGLU_ALPHA = 1.702
SWIGLU_LIMIT = 7.0
MOE_BLOCK = 512
NORM_EPS = 1e-6
NEG_BIG = -1e30

VMEM_LIMIT = 56 * 1024 * 1024

_NT = (((1,), (1,)), ((), ()))
_TN = (((0,), (0,)), ((), ()))


def _cparams(sem):
    return pltpu.CompilerParams(dimension_semantics=sem, vmem_limit_bytes=VMEM_LIMIT)


def _mod_row(i, tm):
    return jnp.minimum((i * tm) // SEQ, BATCH)


def _split_bf16(x, pieces):
    out = []
    rem = x
    for _ in range(pieces):
        p = rem.astype(BF16)
        out.append(p)
        rem = rem - p.astype(F32)
    return out


def _dot_exact_rhs(x, sel, pieces, dims=None):
    acc = None
    for p in _split_bf16(x, pieces):
        if dims is None:
            t = jnp.dot(p, sel, preferred_element_type=F32)
        else:
            t = lax.dot_general(p, sel, dims, preferred_element_type=F32)
        acc = t if acc is None else acc + t
    return acc


def _dot_exact_lhs(sel, x, pieces):
    acc = None
    for p in _split_bf16(x, pieces):
        t = jnp.dot(sel, p, preferred_element_type=F32)
        acc = t if acc is None else acc + t
    return acc


def _normmod_kernel(x_ref, g_ref, sh_ref, sc_ref, o_ref):
    x = x_ref[...]
    ms = jnp.mean(x * x, axis=-1, keepdims=True)
    y = x * lax.rsqrt(ms + NORM_EPS) * g_ref[...]
    o_ref[...] = (y * (1.0 + sc_ref[0]) + sh_ref[0]).astype(o_ref.dtype)


ROUTE_EXPERT_LANE = 0
ROUTE_RANK_LANE = 4
ROUTE_GATE_LANE = 8


def _normmod_router_kernel(x_ref, g_ref, sh_ref, sc_ref, wr_ref, br_ref, o_ref, route_ref, count_ref):
    i = pl.program_id(0)
    x = x_ref[...]
    tm = x.shape[0]
    ms = jnp.mean(x * x, axis=-1, keepdims=True)
    y = x * lax.rsqrt(ms + NORM_EPS) * g_ref[...]
    h = y * (1.0 + sc_ref[0]) + sh_ref[0]
    o_ref[...] = h.astype(o_ref.dtype)
    h_hi, h_lo = _split_bf16(h, 2)
    w_hi, w_lo = _split_bf16(wr_ref[...], 2)
    logits = (jnp.dot(h_hi, w_hi, preferred_element_type=F32) + jnp.dot(h_hi, w_lo, preferred_element_type=F32)
              + jnp.dot(h_lo, w_hi, preferred_element_type=F32)) + br_ref[...]

    @pl.when(i == 0)
    def _():
        count_ref[...] = jnp.zeros_like(count_ref)

    lane_e = lax.broadcasted_iota(jnp.int32, logits.shape, 1).astype(F32)
    cur = logits
    vals, ids, hots = [], [], []
    for _ in range(TOP_K):
        m = jnp.max(cur, axis=-1, keepdims=True)
        idx = jnp.min(jnp.where(cur == m, lane_e, float(N_EXPERTS)), axis=-1, keepdims=True)
        hot = lane_e == idx
        vals.append(m)
        ids.append(idx)
        hots.append(hot)
        cur = jnp.where(hot, -jnp.inf, cur)
    exps = [jnp.exp(v - vals[0]) for v in vals]
    den = exps[0] + exps[1] + exps[2] + exps[3]
    picked = jnp.where(hots[0] | hots[1] | hots[2] | hots[3], 1.0, 0.0)
    row = lax.broadcasted_iota(jnp.int32, (tm, tm), 0)
    col = lax.broadcasted_iota(jnp.int32, (tm, tm), 1)
    earlier = jnp.where(col < row, 1.0, 0.0).astype(BF16)
    before = count_ref[...] + jnp.dot(earlier, picked.astype(BF16), preferred_element_type=F32)
    lane = lax.broadcasted_iota(jnp.int32, (tm, 128), 1)
    route = jnp.zeros((tm, 128), F32)
    for k in range(TOP_K):
        rank = jnp.sum(jnp.where(hots[k], before, 0.0), axis=-1, keepdims=True)
        route = jnp.where(lane == ROUTE_EXPERT_LANE + k, ids[k], route)
        route = jnp.where(lane == ROUTE_RANK_LANE + k, rank, route)
        route = jnp.where(lane == ROUTE_GATE_LANE + k, exps[k] / den, route)
    route_ref[...] = route
    count_ref[...] += jnp.sum(picked, axis=0, keepdims=True)


def _normmod(x, g, shift, scale, n_rows, router=None, tm=512):
    grid = (n_rows // tm,)
    x_spec = pl.BlockSpec((tm, D_MODEL), lambda i: (i, 0))
    g_spec = pl.BlockSpec((1, D_MODEL), lambda i: (0, 0))
    m_spec = pl.BlockSpec((1, 1, D_MODEL), lambda i: (_mod_row(i, tm), 0, 0))
    o_spec = pl.BlockSpec((tm, D_MODEL), lambda i: (i, 0))
    g2 = g.reshape(1, D_MODEL)
    if router is None:
        return pl.pallas_call(
            _normmod_kernel, grid=grid,
            in_specs=[x_spec, g_spec, m_spec, m_spec], out_specs=o_spec,
            out_shape=jax.ShapeDtypeStruct((n_rows, D_MODEL), BF16),
            compiler_params=_cparams(("arbitrary",)), name="normmod",
        )(x, g2, shift, scale)
    w_r, b_r = router
    return pl.pallas_call(
        _normmod_router_kernel, grid=grid,
        in_specs=[x_spec, g_spec, m_spec, m_spec,
                  pl.BlockSpec((D_MODEL, N_EXPERTS), lambda i: (0, 0)),
                  pl.BlockSpec((1, N_EXPERTS), lambda i: (0, 0))],
        out_specs=[o_spec, pl.BlockSpec((tm, 128), lambda i: (i, 0)),
                   pl.BlockSpec((1, N_EXPERTS), lambda i: (0, 0))],
        out_shape=[jax.ShapeDtypeStruct((n_rows, D_MODEL), F32),
                   jax.ShapeDtypeStruct((n_rows, 128), F32),
                   jax.ShapeDtypeStruct((1, N_EXPERTS), F32)],
        compiler_params=_cparams(("arbitrary",)), name="normmod_router",
    )(x, g2, shift, scale, w_r, b_r.reshape(1, N_EXPERTS))


def _mm_bias_f32_kernel(a_ref, w_ref, b_ref, o_ref):
    o_ref[...] = jnp.dot(a_ref[...], w_ref[...], preferred_element_type=F32,
                         precision=lax.Precision.HIGHEST) + b_ref[...]


def _matmul_bias_f32(a, w, bias, *, tn, name):
    m, k = a.shape
    n = w.shape[1]
    return pl.pallas_call(
        _mm_bias_f32_kernel, grid=(n // tn,),
        in_specs=[pl.BlockSpec((m, k), lambda j: (0, 0)),
                  pl.BlockSpec((k, tn), lambda j: (0, j)),
                  pl.BlockSpec((1, tn), lambda j: (0, j))],
        out_specs=pl.BlockSpec((m, tn), lambda j: (0, j)),
        out_shape=jax.ShapeDtypeStruct((m, n), F32),
        compiler_params=_cparams(("arbitrary",)), name=name,
    )(a, w, bias.reshape(1, n))


def _cast_weight_once(first, w_ref, wb_ref):
    @pl.when(first)
    def _():
        wb_ref[...] = w_ref[...].astype(BF16)


def _mm_kernel(a_ref, w_ref, o_ref, wb_ref):
    _cast_weight_once(pl.program_id(1) == 0, w_ref, wb_ref)
    o_ref[...] = jnp.dot(a_ref[...], wb_ref[...], preferred_element_type=F32).astype(o_ref.dtype)


def _mm_res_kernel(a_ref, w_ref, res_ref, gate_ref, o_ref, wb_ref):
    _cast_weight_once(pl.program_id(0) == 0, w_ref, wb_ref)
    acc = jnp.dot(a_ref[...], wb_ref[...], preferred_element_type=F32)
    o_ref[...] = res_ref[...] + gate_ref[0] * acc


def _matmul(a, w, *, tm, tn, n_cols, col0=0, out_dtype=F32, name="mm"):
    m, k = a.shape
    return pl.pallas_call(
        _mm_kernel, grid=(n_cols // tn, m // tm),
        in_specs=[pl.BlockSpec((tm, k), lambda j, i: (i, 0)),
                  pl.BlockSpec((k, tn), lambda j, i: (0, col0 + j))],
        out_specs=pl.BlockSpec((tm, tn), lambda j, i: (i, j)),
        out_shape=jax.ShapeDtypeStruct((m, n_cols), out_dtype),
        scratch_shapes=[pltpu.VMEM((k, tn), BF16)],
        compiler_params=_cparams(("arbitrary", "arbitrary")), name=name,
    )(a, w)


def _matmul_residual(a, w, res, gate, *, tm, name):
    m, k = a.shape
    n = w.shape[1]
    return pl.pallas_call(
        _mm_res_kernel, grid=(m // tm,),
        in_specs=[pl.BlockSpec((tm, k), lambda i: (i, 0)),
                  pl.BlockSpec((k, n), lambda i: (0, 0)),
                  pl.BlockSpec((tm, n), lambda i: (i, 0)),
                  pl.BlockSpec((1, 1, n), lambda i: (_mod_row(i, tm), 0, 0))],
        out_specs=pl.BlockSpec((tm, n), lambda i: (i, 0)),
        out_shape=jax.ShapeDtypeStruct((m, n), F32),
        scratch_shapes=[pltpu.VMEM((k, n), BF16)],
        compiler_params=_cparams(("arbitrary",)), name=name,
    )(a, w, res, gate)


def _qkv_kernel(a_ref, w_ref, gain_ref, seg_ref, segt_ref, o_ref, wb_ref):
    j = pl.program_id(0)
    _cast_weight_once(pl.program_id(1) == 0, w_ref, wb_ref)
    acc = jnp.dot(a_ref[...], wb_ref[...], preferred_element_type=F32)

    @pl.when(j < 2)
    def _():
        ss = _dot_exact_rhs(acc * acc, seg_ref[...], 2)
        inv = lax.rsqrt(ss * (1.0 / NA_HEAD_DIM) + NORM_EPS)
        o_ref[...] = (acc * _dot_exact_rhs(inv, segt_ref[...], 2) * gain_ref[0]).astype(o_ref.dtype)

    @pl.when(j == 2)
    def _():
        o_ref[...] = acc.astype(o_ref.dtype)


def _qkv_proj(h, w_qkv, q_g, k_g, tm=1024):
    m = h.shape[0]
    head_of_col = jnp.arange(D_MODEL) // NA_HEAD_DIM
    seg = (head_of_col[:, None] == jnp.arange(128)[None, :]).astype(BF16)
    gain = jnp.stack([jnp.tile(q_g, NA_HEADS) * NA_HEAD_DIM ** -0.5, jnp.tile(k_g, NA_HEADS),
                      jnp.ones((D_MODEL,), F32)]).reshape(3, 1, D_MODEL)
    return pl.pallas_call(
        _qkv_kernel, grid=(3, m // tm),
        in_specs=[pl.BlockSpec((tm, D_MODEL), lambda j, i: (i, 0)),
                  pl.BlockSpec((D_MODEL, D_MODEL), lambda j, i: (0, j)),
                  pl.BlockSpec((1, 1, D_MODEL), lambda j, i: (j, 0, 0)),
                  pl.BlockSpec((D_MODEL, 128), lambda j, i: (0, 0)),
                  pl.BlockSpec((128, D_MODEL), lambda j, i: (0, 0))],
        out_specs=pl.BlockSpec((tm, D_MODEL), lambda j, i: (i, j)),
        out_shape=jax.ShapeDtypeStruct((m, 3 * D_MODEL), BF16),
        scratch_shapes=[pltpu.VMEM((D_MODEL, D_MODEL), BF16)],
        compiler_params=_cparams(("arbitrary", "arbitrary")), name="qkv_proj",
    )(h, w_qkv, gain, seg, seg.T)


def _na_bias_table(rpb):
    col = jnp.arange(GRID_W)
    col_start = jnp.clip(col - NA_KC // 2, 0, GRID_W - NA_KC)
    col_ok = (col[None, :] >= col_start[:, None]) & (col[None, :] < col_start[:, None] + NA_KC)
    col_idx = jnp.clip(col[None, :] - col[:, None] + NA_KC - 1, 0, 2 * NA_KC - 2)
    by_offset = jnp.take(rpb, col_idx.reshape(-1), axis=2).reshape(NA_HEADS, 2 * NA_KR - 1, GRID_W, GRID_W)
    by_offset = jnp.where(col_ok[None, None], by_offset, NEG_BIG)
    type_blocks = (0, 1, NA_ROWS // NA_QROWS - 1)

    def assemble_kernel(src_ref, o_ref):
        for t, rb in enumerate(type_blocks):
            @pl.when(pl.program_id(0) == t)
            def _(rb=rb):
                r0 = rb * NA_QROWS
                ks = min(max(r0 - NA_KR // 2, 0), NA_ROWS - NA_BAND)
                for i in range(NA_QROWS):
                    r = r0 + i
                    rs = min(max(r - NA_KR // 2, 0), NA_ROWS - NA_KR)
                    for n in range(NA_BAND):
                        kr = ks + n
                        if rs <= kr < rs + NA_KR:
                            tile = src_ref[kr - r + NA_KR - 1]
                        else:
                            tile = jnp.full((GRID_W, GRID_W), NEG_BIG, F32)
                        o_ref[i * GRID_W:(i + 1) * GRID_W, n * GRID_W:(n + 1) * GRID_W] = tile

    return pl.pallas_call(
        assemble_kernel, grid=(len(type_blocks), NA_HEADS),
        in_specs=[pl.BlockSpec((None, 2 * NA_KR - 1, GRID_W, GRID_W), lambda t, h: (h, 0, 0, 0))],
        out_specs=pl.BlockSpec((None, None, NA_QBLK, NA_KBLK), lambda t, h: (t, h, 0, 0)),
        out_shape=jax.ShapeDtypeStruct((len(type_blocks), NA_HEADS, NA_QBLK, NA_KBLK), F32),
        compiler_params=_cparams(("arbitrary", "arbitrary")), name="na_bias_table",
    )(by_offset)


def _softmax_pv(s_list, v_list):
    m = s_list[0].max(axis=-1, keepdims=True)
    for s in s_list[1:]:
        m = jnp.maximum(m, s.max(axis=-1, keepdims=True))
    acc = None
    den = None
    for s, v in zip(s_list, v_list):
        p = jnp.exp(s - m)
        l = p.sum(axis=-1, keepdims=True)
        o = jnp.dot(p.astype(BF16), v, preferred_element_type=F32)
        acc = o if acc is None else acc + o
        den = l if den is None else den + l
    return acc / den


NA_LAT_STEPS = NA_ROWS // NA_QROWS


NA_STEP_HEADS = 8
NA_STEP_LANES = NA_STEP_HEADS * NA_HEAD_DIM


def _na_kernel(q_ref, k_ref, v_ref, kc_ref, vc_ref, tbl_ref, o_ref):
    rb = pl.program_id(2)
    lane = lax.broadcasted_iota(jnp.int32, (NA_QBLK, 2 * NA_HEAD_DIM), 1)

    def pair_lanes(pair):
        return slice(pair * 2 * NA_HEAD_DIM, (pair + 1) * 2 * NA_HEAD_DIM)

    def head_queries(q2, sub):
        sel = (lane < NA_HEAD_DIM) if sub == 0 else (lane >= NA_HEAD_DIM)
        return jnp.where(sel, q2, jnp.zeros_like(q2))

    @pl.when(rb < NA_LAT_STEPS)
    def _():
        ks = jnp.clip(rb * NA_QROWS - NA_KR // 2, 0, NA_ROWS - NA_BAND)
        kstart = pl.multiple_of(ks * GRID_W, GRID_W)
        for pair in range(NA_STEP_HEADS // 2):
            pl_ = pair_lanes(pair)
            q2 = q_ref[:, pl_]
            kb = k_ref[pl.ds(kstart, NA_KBLK), pl_]
            vb = v_ref[pl.ds(kstart, NA_KBLK), pl_]
            kc = kc_ref[:, pl_]
            vc = vc_ref[:, pl_]
            outs = []
            for sub in range(2):
                qm = head_queries(q2, sub)
                s_lat = lax.dot_general(qm, kb, _NT, preferred_element_type=F32) + tbl_ref[0, 2 * pair + sub]
                s_ctx = lax.dot_general(qm, kc, _NT, preferred_element_type=F32)
                outs.append(_softmax_pv([s_lat, s_ctx], [vb, vc]))
            o_ref[:, pl_] = jnp.where(lane < NA_HEAD_DIM, outs[0], outs[1]).astype(o_ref.dtype)

    @pl.when(rb == NA_LAT_STEPS)
    def _():
        for pair in range(NA_STEP_HEADS // 2):
            pl_ = pair_lanes(pair)
            q2 = q_ref[:, pl_]
            kc = kc_ref[:, pl_]
            vc = vc_ref[:, pl_]
            outs = []
            for sub in range(2):
                s_ctx = lax.dot_general(head_queries(q2, sub), kc, _NT, preferred_element_type=F32)
                outs.append(_softmax_pv([s_ctx], [vc]))
            o_ref[:, pl_] = jnp.where(lane < NA_HEAD_DIM, outs[0], outs[1]).astype(o_ref.dtype)


def _na_block_type(rb):
    return jnp.where(rb == 0, 0, jnp.where(rb >= NA_LAT_STEPS - 1, 2, 1))


def _neighbourhood_attention(qkv, table):
    groups = NA_HEADS // NA_STEP_HEADS
    ctx_blk0 = N_LAT // CTX_LEN

    def q_block(b, h, r):
        return (jnp.where(r < NA_LAT_STEPS, b * NA_LAT_STEPS + r, ctx_blk0 + b), h)

    return pl.pallas_call(
        _na_kernel, grid=(BATCH, groups, NA_LAT_STEPS + 1),
        in_specs=[
            pl.BlockSpec((NA_QBLK, NA_STEP_LANES), q_block),
            pl.BlockSpec((SEQ, NA_STEP_LANES), lambda b, h, r: (b, groups + h)),
            pl.BlockSpec((SEQ, NA_STEP_LANES), lambda b, h, r: (b, 2 * groups + h)),
            pl.BlockSpec((CTX_LEN, NA_STEP_LANES), lambda b, h, r: (ctx_blk0 + b, groups + h)),
            pl.BlockSpec((CTX_LEN, NA_STEP_LANES), lambda b, h, r: (ctx_blk0 + b, 2 * groups + h)),
            pl.BlockSpec((1, NA_STEP_HEADS, NA_QBLK, NA_KBLK), lambda b, h, r: (_na_block_type(r), h, 0, 0)),
        ],
        out_specs=pl.BlockSpec((NA_QBLK, NA_STEP_LANES), q_block),
        out_shape=jax.ShapeDtypeStruct((N_ALL, D_MODEL), BF16),
        compiler_params=_cparams(("arbitrary", "arbitrary", "arbitrary")), name="na_attention",
    )(qkv, qkv, qkv, qkv, qkv, table)


CONV_ROWS = 256
CONV_COLS = 2048
CONV_HALO = 8


def _conv_silu_kernel(cur_ref, prev_ref, next_ref, w_ref, b_ref, o_ref):
    i = pl.program_id(0)
    tiles_per_seq = SEQ // CONV_ROWS
    is_ctx = i >= N_LAT // CONV_ROWS
    first = jnp.logical_or(is_ctx, i % tiles_per_seq == 0)
    last = jnp.logical_or(is_ctx, i % tiles_per_seq == tiles_per_seq - 1)
    prev = jnp.where(first, 0.0, prev_ref[...])
    nxt = jnp.where(last, 0.0, next_ref[...])
    cur = cur_ref[...]
    pad = SSD_CONV_W // 2

    def taps(slab, rows):
        acc = jnp.zeros((rows, slab.shape[1]), F32) + b_ref[...]
        for t in range(SSD_CONV_W):
            lo = CONV_HALO - pad + t
            acc = acc + slab[lo:lo + rows, :] * w_ref[t:t + 1, :]
        return acc

    acc = jnp.zeros(o_ref.shape, F32) + b_ref[...]
    for t in range(SSD_CONV_W):
        shifted = cur if t == pad else pltpu.roll(cur, (pad - t) % CONV_ROWS, axis=0)
        acc = acc + shifted * w_ref[t:t + 1, :]
    top = taps(jnp.concatenate([prev, cur[:2 * CONV_HALO]], axis=0), CONV_HALO)
    bottom = taps(jnp.concatenate([cur[CONV_ROWS - 2 * CONV_HALO:], nxt], axis=0), CONV_HALO)
    acc = jnp.concatenate([top, acc[CONV_HALO:CONV_ROWS - CONV_HALO], bottom], axis=0)
    o_ref[...] = acc * jax.nn.sigmoid(acc)


def _conv_silu(zxbc, conv_w, conv_b):
    col0 = SSD_D_INNER // CONV_COLS
    n_halo_blocks = N_ALL // CONV_HALO
    per_tile = CONV_ROWS // CONV_HALO
    return pl.pallas_call(
        _conv_silu_kernel, grid=(N_ALL // CONV_ROWS, SSD_CONV_DIM // CONV_COLS),
        in_specs=[
            pl.BlockSpec((CONV_ROWS, CONV_COLS), lambda i, c: (i, col0 + c)),
            pl.BlockSpec((CONV_HALO, CONV_COLS), lambda i, c: (jnp.maximum(i * per_tile - 1, 0), col0 + c)),
            pl.BlockSpec((CONV_HALO, CONV_COLS),
                         lambda i, c: (jnp.minimum((i + 1) * per_tile, n_halo_blocks - 1), col0 + c)),
            pl.BlockSpec((SSD_CONV_W, CONV_COLS), lambda i, c: (0, c)),
            pl.BlockSpec((1, CONV_COLS), lambda i, c: (0, c)),
        ],
        out_specs=pl.BlockSpec((CONV_ROWS, CONV_COLS), lambda i, c: (i, c)),
        out_shape=jax.ShapeDtypeStruct((N_ALL, SSD_CONV_DIM), F32),
        compiler_params=_cparams(("arbitrary", "arbitrary")), name="ssd_conv_silu",
    )(zxbc, zxbc, zxbc, conv_w, conv_b.reshape(1, SSD_CONV_DIM))


SSD_STEPS = (CTX_LEN + SEQ) // SSD_CHUNK
SSD_CTX_CHUNKS = CTX_LEN // SSD_CHUNK
SSD_LAT_CHUNKS = SEQ // SSD_CHUNK


def _ssd_chunk(b, d, j):
    jj = jnp.where(d == 0, j, jnp.where(j < SSD_CTX_CHUNKS, SSD_CTX_CHUNKS - 1 - j,
                                        SSD_STEPS + SSD_CTX_CHUNKS - 1 - j))
    return jnp.where(jj < SSD_CTX_CHUNKS,
                     N_LAT // SSD_CHUNK + SSD_CTX_CHUNKS * b + jj,
                     SSD_LAT_CHUNKS * b + jj - SSD_CTX_CHUNKS)


def _ssd_kernel(xs_ref, b_ref, c_ref, dtr_ref, bias_ref, alog_ref, expand_ref, y_ref, state_ref):
    d = pl.program_id(1)
    j = pl.program_id(2)

    @pl.when(j == 0)
    def _():
        state_ref[...] = jnp.zeros_like(state_ref)

    fwd = d == 0
    dtr = dtr_ref[...]
    dt_pre = jnp.where(fwd, dtr[:, :SSD_HEADS], dtr[:, SSD_HEADS:]) + bias_ref[...]
    dt = jnp.maximum(dt_pre, 0.0) + jnp.log1p(jnp.exp(-jnp.abs(dt_pre)))
    da = dt * (-jnp.exp(alog_ref[...]))
    li = lax.broadcasted_iota(jnp.int32, (SSD_CHUNK, SSD_CHUNK), 0)
    si = lax.broadcasted_iota(jnp.int32, (SSD_CHUNK, SSD_CHUNK), 1)
    before = jnp.where(fwd, li - si, si - li) >= 0
    before_t = jnp.where(fwd, si - li, li - si) >= 0
    cum = _dot_exact_lhs(jnp.where(before, 1.0, 0.0).astype(BF16), da, 3)
    cum_t = _dot_exact_rhs(da, jnp.where(before_t, 1.0, 0.0).astype(BF16), 3, _TN)
    dt_t = _dot_exact_rhs(dt, jnp.where(li == si, 1.0, 0.0).astype(BF16), 2, _TN)
    tot = jnp.where(fwd, cum[SSD_CHUNK - 1:SSD_CHUNK, :], cum[0:1, :])
    expand = expand_ref[...]
    to_end = _dot_exact_rhs(dt * jnp.exp(tot - cum), expand, 2)
    from_start = _dot_exact_rhs(jnp.exp(cum), expand, 2)
    carry = jnp.where(fwd, from_start[SSD_CHUNK - 1:SSD_CHUNK, :], from_start[0:1, :])
    xs = xs_ref[...]
    x16 = xs.astype(BF16)
    xdec = (xs * to_end).astype(BF16)
    lane = lax.broadcasted_iota(jnp.int32, (SSD_CHUNK, 2 * SSD_HEAD_DIM), 1)
    for g in range(SSD_GROUPS):
        gs = slice(g * SSD_GROUP_W, (g + 1) * SSD_GROUP_W)
        bg = b_ref[:, g * SSD_STATE:(g + 1) * SSD_STATE].astype(BF16)
        cg = c_ref[:, g * SSD_STATE:(g + 1) * SSD_STATE].astype(BF16)
        cb = lax.dot_general(cg, bg, _NT, preferred_element_type=F32)
        st = state_ref[g]
        y_off = jnp.dot(cg, st.astype(BF16), preferred_element_type=F32) * from_start[:, gs]
        y_pairs = []
        for pr in range(SSD_HPG // 2):
            ms = []
            for r2 in range(2):
                h = g * SSD_HPG + pr * 2 + r2
                decay = jnp.exp(jnp.where(before, cum[:, h:h + 1] - cum_t[h:h + 1, :], NEG_BIG))
                ms.append((cb * decay * dt_t[h:h + 1, :]).astype(BF16))
            m2 = jnp.concatenate(ms, axis=1)
            c0 = g * SSD_GROUP_W + pr * 2 * SSD_HEAD_DIM
            slab = x16[:, c0:c0 + 2 * SSD_HEAD_DIM]
            zero = jnp.zeros_like(slab)
            xdiag = jnp.concatenate([jnp.where(lane < SSD_HEAD_DIM, slab, zero),
                                     jnp.where(lane >= SSD_HEAD_DIM, slab, zero)], axis=0)
            y_pairs.append(jnp.dot(m2, xdiag, preferred_element_type=F32))
        y_ref[:, gs] = jnp.concatenate(y_pairs, axis=1) + y_off
        s_new = lax.dot_general(bg, xdec[:, gs], _TN, preferred_element_type=F32)
        state_ref[g] = st * carry[:, gs] + s_new


def _ssd_scan(xbc, dt_raw, dt_bias, a_log):
    expand = (jnp.arange(SSD_HEADS)[:, None] == (jnp.arange(SSD_D_INNER) // SSD_HEAD_DIM)[None, :]).astype(BF16)
    bc_w = SSD_GROUPS * SSD_STATE
    b_col = SSD_D_INNER // bc_w
    return pl.pallas_call(
        _ssd_kernel, grid=(BATCH, 2, SSD_STEPS),
        in_specs=[
            pl.BlockSpec((SSD_CHUNK, SSD_D_INNER), lambda b, d, j: (_ssd_chunk(b, d, j), 0)),
            pl.BlockSpec((SSD_CHUNK, bc_w), lambda b, d, j: (_ssd_chunk(b, d, j), b_col)),
            pl.BlockSpec((SSD_CHUNK, bc_w), lambda b, d, j: (_ssd_chunk(b, d, j), b_col + 1)),
            pl.BlockSpec((SSD_CHUNK, 2 * SSD_HEADS), lambda b, d, j: (_ssd_chunk(b, d, j), 0)),
            pl.BlockSpec((None, 1, SSD_HEADS), lambda b, d, j: (d, 0, 0)),
            pl.BlockSpec((None, 1, SSD_HEADS), lambda b, d, j: (d, 0, 0)),
            pl.BlockSpec((SSD_HEADS, SSD_D_INNER), lambda b, d, j: (0, 0)),
        ],
        out_specs=pl.BlockSpec((None, SSD_CHUNK, SSD_D_INNER), lambda b, d, j: (d, _ssd_chunk(b, d, j), 0)),
        out_shape=jax.ShapeDtypeStruct((2, N_ALL, SSD_D_INNER), F32),
        scratch_shapes=[pltpu.VMEM((SSD_GROUPS, SSD_STATE, SSD_GROUP_W), F32)],
        compiler_params=_cparams(("arbitrary", "arbitrary", "arbitrary")), name="ssd_scan",
    )(xbc, xbc, xbc, dt_raw, dt_bias.reshape(2, 1, SSD_HEADS), a_log.reshape(2, 1, SSD_HEADS), expand)


def _ssd_out_kernel(yf_ref, yb_ref, xs_ref, z_ref, dskip_ref, g_ref, w_ref, res_ref, gate_ref, o_ref,
                    wb_ref, gated_ref):
    _cast_weight_once(pl.program_id(0) == 0, w_ref, wb_ref)
    z = z_ref[...]
    y = (yf_ref[...] + yb_ref[...] + dskip_ref[...] * xs_ref[...]) * (z * jax.nn.sigmoid(z))
    for grp in range(SSD_GROUPS):
        gs = slice(grp * SSD_GROUP_W, (grp + 1) * SSD_GROUP_W)
        yg = y[:, gs]
        ms = jnp.mean(yg * yg, axis=-1, keepdims=True)
        gated_ref[:, gs] = (yg * lax.rsqrt(ms + NORM_EPS) * g_ref[:, gs]).astype(BF16)
    acc = jnp.dot(gated_ref[...], wb_ref[...], preferred_element_type=F32)
    o_ref[...] = res_ref[...] + gate_ref[0] * acc


def _ssd_out_proj(y2, xbc, zxbc, d_skip, norm_g, w_out, res, gate, n_out, tm=256):
    d_cols = jnp.repeat(d_skip, SSD_HEAD_DIM).reshape(1, SSD_D_INNER)
    return pl.pallas_call(
        _ssd_out_kernel, grid=(n_out // tm,),
        in_specs=[
            pl.BlockSpec((None, tm, SSD_D_INNER), lambda i: (0, i, 0)),
            pl.BlockSpec((None, tm, SSD_D_INNER), lambda i: (1, i, 0)),
            pl.BlockSpec((tm, SSD_D_INNER), lambda i: (i, 0)),
            pl.BlockSpec((tm, SSD_D_INNER), lambda i: (i, 0)),
            pl.BlockSpec((1, SSD_D_INNER), lambda i: (0, 0)),
            pl.BlockSpec((1, SSD_D_INNER), lambda i: (0, 0)),
            pl.BlockSpec((SSD_D_INNER, D_MODEL), lambda i: (0, 0)),
            pl.BlockSpec((tm, D_MODEL), lambda i: (i, 0)),
            pl.BlockSpec((1, 1, D_MODEL), lambda i: (_mod_row(i, tm), 0, 0)),
        ],
        out_specs=pl.BlockSpec((tm, D_MODEL), lambda i: (i, 0)),
        out_shape=jax.ShapeDtypeStruct((n_out, D_MODEL), F32),
        scratch_shapes=[pltpu.VMEM((SSD_D_INNER, D_MODEL), BF16), pltpu.VMEM((tm, SSD_D_INNER), BF16)],
        compiler_params=_cparams(("arbitrary",)), name="ssd_out_proj",
    )(y2, y2, xbc, zxbc, d_cols, norm_g.reshape(1, SSD_D_INNER), w_out, res, gate)


DISPATCH_TOKENS = 1024
COMBINE_TOKENS = 256
ZERO_FILL_ROWS = MOE_BLOCK + 8


def _dispatch_kernel(dest_ref, padstart_ref, nreal_ref, h_ref, xb_ref, zeros_ref, sem):
    i = pl.program_id(0)
    base = i * (DISPATCH_TOKENS * TOP_K)

    @pl.when(i == 0)
    def _():
        zeros_ref[...] = jnp.zeros_like(zeros_ref)
        n_slots = xb_ref.shape[0]

        def fill(e):
            start = jnp.minimum(padstart_ref[e] // 8 * 8, n_slots - ZERO_FILL_ROWS)
            return pltpu.make_async_copy(zeros_ref, xb_ref.at[pl.ds(pl.multiple_of(start, 8), ZERO_FILL_ROWS)],
                                         sem.at[0])

        for e in range(N_EXPERTS):
            fill(e).start()
        for e in range(N_EXPERTS):
            fill(e).wait()

        def fill_block(blk):
            return pltpu.make_async_copy(zeros_ref.at[pl.ds(0, MOE_BLOCK)],
                                         xb_ref.at[pl.ds(pl.multiple_of(blk * MOE_BLOCK, MOE_BLOCK), MOE_BLOCK)],
                                         sem.at[0])

        def start_block(blk, carry):
            fill_block(blk).start()
            return carry

        def wait_block(blk, carry):
            fill_block(blk).wait()
            return carry

        lax.fori_loop(nreal_ref[0], n_slots // MOE_BLOCK, start_block, 0)
        lax.fori_loop(nreal_ref[0], n_slots // MOE_BLOCK, wait_block, 0)

    def body(t, carry):
        for k in range(TOP_K):
            pltpu.make_async_copy(h_ref.at[pl.ds(t, 1)], xb_ref.at[pl.ds(dest_ref[base + t * TOP_K + k], 1)],
                                  sem.at[0]).start(priority=k % 2)
        return carry

    lax.fori_loop(0, DISPATCH_TOKENS, body, 0, unroll=4)
    for k in range(TOP_K):
        pltpu.make_async_copy(h_ref, xb_ref.at[pl.ds(0, DISPATCH_TOKENS)], sem.at[0]).wait()


def _moe_dispatch(h, dest, pad_start, n_real, n_slots):
    n_tok = h.shape[0]
    tt = DISPATCH_TOKENS
    grid_spec = pltpu.PrefetchScalarGridSpec(
        num_scalar_prefetch=3, grid=(n_tok // tt,),
        in_specs=[pl.BlockSpec((tt, D_MODEL), lambda i, d, p, nr: (i, 0))],
        out_specs=pl.BlockSpec(memory_space=pl.ANY),
        scratch_shapes=[pltpu.VMEM((ZERO_FILL_ROWS, D_MODEL), F32), pltpu.SemaphoreType.DMA((1,))],
    )
    return pl.pallas_call(
        _dispatch_kernel, grid_spec=grid_spec,
        out_shape=jax.ShapeDtypeStruct((n_slots, D_MODEL), F32),
        compiler_params=_cparams(("arbitrary",)), name="moe_dispatch",
    )(dest, pad_start, n_real, h)


def _moe_kernel(be_ref, nreal_ref, first_ref, slot_ref, next_ref, x_ref, w1_hbm, b1_ref, w2_hbm, b2_ref, o_ref,
                w1f_ref, w2f_ref, w1b_ref, w2b_ref, sem, *, layer):
    i = pl.program_id(0)

    def weight_copies(e, slot):
        return (pltpu.make_async_copy(w1_hbm.at[layer, e], w1f_ref.at[slot], sem.at[slot]),
                pltpu.make_async_copy(w2_hbm.at[layer, e], w2f_ref.at[slot], sem.at[slot]))

    @pl.when(i < nreal_ref[0])
    def _():
        @pl.when(i == 0)
        def _():
            for c in weight_copies(be_ref[0], 0):
                c.start()

        @pl.when(first_ref[i] == 1)
        def _():
            slot = slot_ref[i]
            for c in weight_copies(be_ref[i], slot):
                c.wait()
            w1b_ref[...] = w1f_ref[slot].astype(BF16)
            w2b_ref[...] = w2f_ref[slot].astype(BF16)

            @pl.when(next_ref[i] >= 0)
            def _():
                for c in weight_copies(next_ref[i], 1 - slot):
                    c.start()

        h = jnp.dot(x_ref[...].astype(BF16), w1b_ref[...], preferred_element_type=F32) + b1_ref[0]
        glu = jnp.minimum(h[:, :D_FF], SWIGLU_LIMIT)
        lin = jnp.clip(h[:, D_FF:], -SWIGLU_LIMIT, SWIGLU_LIMIT)
        act = glu * jax.nn.sigmoid(SWIGLU_ALPHA * glu) * (lin + 1.0)
        o_ref[...] = jnp.dot(act.astype(BF16), w2b_ref[...], preferred_element_type=F32) + b2_ref[0]

    @pl.when(i >= nreal_ref[0])
    def _():
        o_ref[...] = jnp.zeros_like(o_ref)


def _moe_experts(xb, block_expert, n_real, pend, layer, w1, b1, w2, b2):
    n_slots = xb.shape[0]
    n_blocks = n_slots // MOE_BLOCK
    first = jnp.concatenate([jnp.ones((1,), jnp.int32),
                             (block_expert[1:] != block_expert[:-1]).astype(jnp.int32)])
    slot = (jnp.cumsum(first) - 1) % 2
    next_block = pend[block_expert] // MOE_BLOCK
    next_expert = jnp.where(next_block < n_real[0], block_expert[jnp.minimum(next_block, n_blocks - 1)], -1)
    grid_spec = pltpu.PrefetchScalarGridSpec(
        num_scalar_prefetch=5, grid=(n_blocks,),
        in_specs=[
            pl.BlockSpec((MOE_BLOCK, D_MODEL), lambda i, be, nr, *_: (jnp.minimum(i, nr[0] - 1), 0)),
            pl.BlockSpec(memory_space=pl.ANY),
            pl.BlockSpec((1, 1, 2 * D_FF), lambda i, be, *_: (be[i], 0, 0)),
            pl.BlockSpec(memory_space=pl.ANY),
            pl.BlockSpec((1, 1, D_MODEL), lambda i, be, *_: (be[i], 0, 0)),
        ],
        out_specs=pl.BlockSpec((MOE_BLOCK, D_MODEL), lambda i, *_: (i, 0)),
        scratch_shapes=[pltpu.VMEM((2, D_MODEL, 2 * D_FF), F32), pltpu.VMEM((2, D_FF, D_MODEL), F32),
                        pltpu.VMEM((D_MODEL, 2 * D_FF), BF16), pltpu.VMEM((D_FF, D_MODEL), BF16),
                        pltpu.SemaphoreType.DMA((2,))],
    )
    return pl.pallas_call(
        functools.partial(_moe_kernel, layer=layer), grid_spec=grid_spec,
        out_shape=jax.ShapeDtypeStruct((n_slots, D_MODEL), F32),
        compiler_params=_cparams(("arbitrary",)), name="moe_experts",
    )(block_expert, n_real, first, slot.astype(jnp.int32), next_expert.astype(jnp.int32), xb, w1,
      b1.reshape(N_EXPERTS, 1, 2 * D_FF), w2, b2.reshape(N_EXPERTS, 1, D_MODEL))


def _combine_tile(dest_ref, y_hbm, route_ref, res_ref, gate_ref, buf, sem):
    i = pl.program_id(0)
    n = pl.num_programs(0)

    def issue(tile, slot):
        base = tile * (COMBINE_TOKENS * TOP_K)

        def body(t, carry):
            for k in range(TOP_K):
                pltpu.make_async_copy(y_hbm.at[pl.ds(dest_ref[base + t * TOP_K + k], 1)],
                                      buf.at[slot, k, pl.ds(t, 1)], sem.at[slot]).start(priority=k % 2)
            return carry

        lax.fori_loop(0, COMBINE_TOKENS, body, 0, unroll=4)

    @pl.when(i == 0)
    def _():
        issue(0, 0)

    @pl.when(i + 1 < n)
    def _():
        issue(i + 1, (i + 1) % 2)

    slot = i % 2
    for k in range(TOP_K):
        pltpu.make_async_copy(y_hbm.at[pl.ds(0, COMBINE_TOKENS)], buf.at[slot, k], sem.at[slot]).wait()
    route = route_ref[...]
    y = None
    for k in range(TOP_K):
        t = route[:, ROUTE_GATE_LANE + k:ROUTE_GATE_LANE + k + 1] * buf[slot, k]
        y = t if y is None else y + t
    return res_ref[...] + gate_ref[0] * y


def _combine_kernel(dest_ref, y_hbm, route_ref, res_ref, gate_ref, o_ref, buf, sem):
    o_ref[...] = _combine_tile(dest_ref, y_hbm, route_ref, res_ref, gate_ref, buf, sem)


def _combine_norm_kernel(dest_ref, y_hbm, route_ref, res_ref, gate_ref, g_ref, sh_ref, sc_ref, o_ref, h_ref,
                         buf, sem):
    x = _combine_tile(dest_ref, y_hbm, route_ref, res_ref, gate_ref, buf, sem)
    o_ref[...] = x
    ms = jnp.mean(x * x, axis=-1, keepdims=True)
    y = x * lax.rsqrt(ms + NORM_EPS) * g_ref[...]
    h_ref[...] = (y * (1.0 + sc_ref[0]) + sh_ref[0]).astype(h_ref.dtype)


def _moe_combine(yb, dest, route, res, gate, next_norm=None):
    n_tok = route.shape[0]
    tt = COMBINE_TOKENS
    row_spec = pl.BlockSpec((tt, D_MODEL), lambda i, d: (i, 0))
    mod_spec = pl.BlockSpec((1, 1, D_MODEL), lambda i, d: (_mod_row(i, tt), 0, 0))
    in_specs = [pl.BlockSpec(memory_space=pl.ANY), pl.BlockSpec((tt, 128), lambda i, d: (i, 0)), row_spec, mod_spec]
    args = [dest, yb, route, res, gate]
    x_shape = jax.ShapeDtypeStruct((n_tok, D_MODEL), F32)
    if next_norm is None:
        body, out_specs, out_shape = _combine_kernel, row_spec, x_shape
    else:
        g, shift, scale = next_norm
        body = _combine_norm_kernel
        in_specs += [pl.BlockSpec((1, D_MODEL), lambda i, d: (0, 0)), mod_spec, mod_spec]
        args += [g.reshape(1, D_MODEL), shift, scale]
        out_specs = [row_spec, row_spec]
        out_shape = [x_shape, jax.ShapeDtypeStruct((n_tok, D_MODEL), BF16)]
    grid_spec = pltpu.PrefetchScalarGridSpec(
        num_scalar_prefetch=1, grid=(n_tok // tt,), in_specs=in_specs, out_specs=out_specs,
        scratch_shapes=[pltpu.VMEM((2, TOP_K, tt, D_MODEL), F32), pltpu.SemaphoreType.DMA((2,))],
    )
    return pl.pallas_call(
        body, grid_spec=grid_spec, out_shape=out_shape,
        compiler_params=_cparams(("arbitrary",)), name="moe_combine",
    )(*args)


def _moe(h, route, counts, res, gate, layer, w1, b1, w2, b2, next_norm=None):
    n_tok = h.shape[0]
    nk = n_tok * TOP_K
    n_blocks = -(-(nk + N_EXPERTS * (MOE_BLOCK - 1)) // MOE_BLOCK)
    n_slots = n_blocks * MOE_BLOCK
    expert = route[:, ROUTE_EXPERT_LANE:ROUTE_EXPERT_LANE + TOP_K].astype(jnp.int32)
    rank = route[:, ROUTE_RANK_LANE:ROUTE_RANK_LANE + TOP_K].astype(jnp.int32)
    counts = counts.reshape(N_EXPERTS).astype(jnp.int32)
    padded = (counts + MOE_BLOCK - 1) // MOE_BLOCK * MOE_BLOCK
    pend = jnp.cumsum(padded)
    pstart = pend - padded
    ids = jnp.arange(N_EXPERTS, dtype=jnp.int32)
    dest = rank + jnp.sum(jnp.where(expert[..., None] == ids, pstart, 0), axis=-1)
    dest = dest.reshape(nk).astype(jnp.int32)
    block_start = jnp.arange(n_blocks, dtype=jnp.int32) * MOE_BLOCK
    block_expert = jnp.minimum(jnp.sum((pend[None, :] <= block_start[:, None]).astype(jnp.int32), axis=1),
                               N_EXPERTS - 1).astype(jnp.int32)
    n_real = (pend[-1] // MOE_BLOCK).astype(jnp.int32).reshape(1)
    pad_start = (pstart + counts).astype(jnp.int32)
    xb = _moe_dispatch(h, dest, pad_start, n_real, n_slots)
    yb = _moe_experts(xb, block_expert, n_real, pend.astype(jnp.int32), layer, w1, b1, w2, b2)
    return _moe_combine(yb, dest, route, res, gate, next_norm)


def kernel(x, c, ctx, c_ctx, ada_w, ada_b, norm1_g, norm2_g, na_w_qkv, na_q_g, na_k_g, na_rpb, na_w_o,
           ssd_w_in, ssd_conv_w, ssd_conv_b, ssd_dt_bias, ssd_a_log, ssd_d, ssd_norm_g, ssd_w_out,
           moe_w_router, moe_b_router, moe_w1, moe_b1, moe_w2, moe_b2):
    xa = jnp.concatenate([x.reshape(N_LAT, D_MODEL), ctx.reshape(N_CTX, D_MODEL)], axis=0)
    cond = jnp.concatenate([jax.nn.silu(c), jax.nn.silu(c_ctx)[None, :],
                            jnp.zeros((16 - BATCH - 1, D_MODEL), F32)], axis=0)
    mods = []
    for i in range(DEPTH):
        mod = _matmul_bias_f32(cond, ada_w[i], ada_b[i], tn=D_MODEL, name="ada_mod")
        mods.append([mod[:, t * D_MODEL:(t + 1) * D_MODEL].reshape(16, 1, D_MODEL) for t in range(6)])
    h1 = _normmod(xa, norm1_g[0], mods[0][0], mods[0][1], N_ALL)
    for i in range(DEPTH):
        last = i == DEPTH - 1
        mt = mods[i]
        n_out = N_LAT if last else N_ALL
        j = i // 2
        if i % 2 == 0:
            qkv = _qkv_proj(h1, na_w_qkv[j], na_q_g[j], na_k_g[j])
            o_all = _neighbourhood_attention(qkv, _na_bias_table(na_rpb[j]))
            xa_new = _matmul_residual(o_all[:n_out], na_w_o[j], xa, mt[2], tm=1024, name="na_out_proj")
        else:
            w_in = ssd_w_in[j]
            zxbc = _matmul(h1, w_in, tm=1024, tn=1024, n_cols=SSD_MAIN_W, name="ssd_in_proj")
            dt_raw = _matmul(h1, w_in[:, SSD_MAIN_W:], tm=512, tn=2 * SSD_HEADS, n_cols=2 * SSD_HEADS,
                             name="ssd_dt_proj")
            xbc = _conv_silu(zxbc, ssd_conv_w[j], ssd_conv_b[j])
            y2 = _ssd_scan(xbc, dt_raw, ssd_dt_bias[j], ssd_a_log[j])
            xa_new = _ssd_out_proj(y2, xbc, zxbc, ssd_d[j], ssd_norm_g[j], ssd_w_out[j], xa, mt[2], n_out)
        n_tok = xa_new.shape[0]
        h2, route, counts = _normmod(xa_new, norm2_g[i], mt[3], mt[4], n_tok,
                                     router=(moe_w_router[i], moe_b_router[i]))
        if last:
            xa = _moe(h2, route, counts, xa_new, mt[5], i, moe_w1, moe_b1[i], moe_w2, moe_b2[i])
        else:
            xa, h1 = _moe(h2, route, counts, xa_new, mt[5], i, moe_w1, moe_b1[i], moe_w2, moe_b2[i],
                          next_norm=(norm1_g[i + 1], mods[i + 1][0], mods[i + 1][1]))
    return xa[:N_LAT].reshape(BATCH, SEQ, D_MODEL)
```
